```python
import math
import jax
import jax.numpy as jnp
from jax import lax
import numpy as np

D_MODEL = 1024
BATCH = 8
SEQ = 4096
DEPTH = 2

CTX_LEN = 256
GRID_W = 64
EPS = 1e-6

ATT_HEADS = 8
ATT_KV_HEADS = 2
ATT_GROUP = ATT_HEADS // ATT_KV_HEADS
HEAD_DIM = 128
ROPE_PAIRS_PER_AXIS = HEAD_DIM // 4
ROPE_THETA = 10000.0
Q_BLOCK = 128

GLA_HEADS = 4
GLA_DK = 64
GLA_DV = 128
GLA_RANK = 16
GLA_TAU = 16.0
GLA_CHUNK = 64

HY_WIDTH = 512
HY_ORDER = 2
HY_BANDS = 16
HY_EMB = 1 + 2 * HY_BANDS
HY_FFN = 64
HY_DECAY_SLOW = -math.log(1e-2) / 1.5
HY_DECAY_FAST = -math.log(1e-2) / 0.3

N_EXPERTS = 16
N_GROUPS = 4
EXPERTS_PER_GROUP = N_EXPERTS // N_GROUPS
TOP_K = 2
D_EXPERT = 512

N_BRANCH = 3
ATT_Q_W = ATT_HEADS * HEAD_DIM
ATT_KV_W = ATT_KV_HEADS * HEAD_DIM
GLA_K_W = GLA_HEADS * GLA_DK
GLA_V_W = GLA_HEADS * GLA_DV
KV_SPLITS = (ATT_KV_W, ATT_KV_W, GLA_K_W, GLA_V_W, 2 * GLA_RANK)
REST_SPLITS = (ATT_Q_W, GLA_K_W, GLA_V_W, 3 * HY_WIDTH, N_BRANCH * D_MODEL)
N_KV_COLS = sum(KV_SPLITS)
N_IN = N_KV_COLS + sum(REST_SPLITS)

kernel_name = 'hybrid_gated_hyena_gla_gqa_moe_dit'


def rms_norm(x, g):
    xf = x.astype(jnp.float32)
    y = xf * lax.rsqrt(jnp.mean(xf * xf, axis=-1, keepdims=True) + EPS)
    return (y * g.astype(jnp.float32)).astype(x.dtype)


def split_cols(t, sizes):
    cuts = [int(v) for v in np.cumsum(sizes)[:-1]]
    return jnp.split(t, cuts, axis=-1)


def modulate(h, shift, scale):
    return h * (1 + scale[..., None, :]) + shift[..., None, :]


def to_heads(t, n_heads):
    b, n, w = t.shape
    return t.reshape(b, n, n_heads, w // n_heads).transpose(0, 2, 1, 3)


def q_heads(t):
    b, n, _ = t.shape
    return t.reshape(b, n, ATT_KV_HEADS, ATT_GROUP, HEAD_DIM).transpose(0, 2, 3, 1, 4)


def tflip(t):
    return jnp.flip(t, axis=2)


def axial_rope_tables(n_tokens):
    rows = n_tokens // GRID_W
    row = jnp.broadcast_to(jnp.arange(rows)[:, None], (rows, GRID_W)).reshape(-1).astype(jnp.float32)
    col = jnp.broadcast_to(jnp.arange(GRID_W)[None, :], (rows, GRID_W)).reshape(-1).astype(jnp.float32)
    inv_freq = ROPE_THETA ** (-jnp.arange(ROPE_PAIRS_PER_AXIS, dtype=jnp.float32) / ROPE_PAIRS_PER_AXIS)
    ang = jnp.concatenate([row[:, None] * inv_freq, col[:, None] * inv_freq], axis=-1)
    return jnp.cos(ang), jnp.sin(ang)


def apply_rope(x, cos, sin):
    xf = x.astype(jnp.float32)
    x1, x2 = xf[..., 0::2], xf[..., 1::2]
    out = jnp.stack([x1 * cos - x2 * sin, x1 * sin + x2 * cos], axis=-1)
    return out.reshape(x.shape).astype(x.dtype)


def gqa_latent(q, k, v):
    b, hkv, g, s, hd = q.shape
    nb = s // Q_BLOCK
    qb = jnp.moveaxis(q.reshape(b, hkv, g, nb, Q_BLOCK, hd), 3, 0)
    scale = HEAD_DIM ** -0.5

    def one_block(qblk):
        sc = jnp.einsum('bkgqd,bknd->bkgqn', qblk, k).astype(jnp.float32) * scale
        p = jax.nn.softmax(sc, axis=-1).astype(v.dtype)
        return jnp.einsum('bkgqn,bknd->bkgqd', p, v)

    o = lax.map(one_block, qb)
    return o.transpose(1, 0, 4, 2, 3, 5).reshape(b, s, hkv * g * hd)


def gqa_context(q, k, v):
    b, hkv, g, n, hd = q.shape
    sc = jnp.einsum('bkgqd,bknd->bkgqn', q, k).astype(jnp.float32) * HEAD_DIM ** -0.5
    p = jax.nn.softmax(sc, axis=-1).astype(v.dtype)
    o = jnp.einsum('bkgqn,bknd->bkgqd', p, v)
    return o.transpose(0, 3, 1, 2, 4).reshape(b, n, hkv * g * hd)


def gla_q(t):
    return to_heads(t, GLA_HEADS).astype(jnp.float32) * GLA_DK ** -0.5


def gla_log_decay(a_low, w_a2, b_a):
    z = (a_low @ w_a2 + b_a).astype(jnp.float32)
    return to_heads(jax.nn.log_sigmoid(z) / GLA_TAU, GLA_HEADS)


def gla_chunked(q, k, v, log_a, s0):
    b, h, n_tok, dk = q.shape
    dv = v.shape[-1]
    n = n_tok // GLA_CHUNK

    def rs(t):
        return t.reshape(b, h, n, GLA_CHUNK, t.shape[-1])

    q, k, v, log_a = rs(q), rs(k), rs(v), rs(log_a)
    cum = jnp.cumsum(log_a, axis=3)
    cum_last = cum[:, :, :, -1:, :]
    q_dec = q * jnp.exp(cum)
    k_inv = k * jnp.exp(-cum)
    k_to_end = k * jnp.exp(cum_last - cum)
    mask = jnp.tril(jnp.ones((GLA_CHUNK, GLA_CHUNK), dtype=bool))
    att = jnp.where(mask, jnp.einsum('bhnqd,bhnsd->bhnqs', q_dec, k_inv), 0.0)
    o_intra = jnp.einsum('bhnqs,bhnsv->bhnqv', att, v)
    chunk_state = jnp.einsum('bhnsd,bhnsv->bhndv', k_to_end, v)
    chunk_decay = jnp.exp(cum_last[:, :, :, 0, :])

    def step(state, xs):
        dec, cs = xs
        return dec[..., None] * state + cs, state

    s_final, s_prev = lax.scan(step, s0, (jnp.moveaxis(chunk_decay, 2, 0), jnp.moveaxis(chunk_state, 2, 0)))
    s_prev = jnp.moveaxis(s_prev, 0, 2)
    o_inter = jnp.einsum('bhnqd,bhndv->bhnqv', q_dec, s_prev)
    return (o_intra + o_inter).reshape(b, h, n_tok, dv), s_final


def gla_final_state(k, v, log_a):
    cum = jnp.cumsum(log_a, axis=2)
    w = jnp.exp(cum[:, :, -1:, :] - cum)
    return jnp.einsum('bhld,bhlv->bhdv', k * w, v)


def gla_bidir(q, k, v, la_f, la_b, s0_f, s0_b):
    o_f, s_f = gla_chunked(q, k, v, la_f, s0_f)
    o_b, s_b = gla_chunked(tflip(q), tflip(k), tflip(v), tflip(la_b), s0_b)
    return o_f + tflip(o_b), s_f, s_b


def gla_output(o, og, norm_g):
    b, h, n, dv = o.shape
    on = rms_norm(o, norm_g).transpose(0, 2, 1, 3).reshape(b, n, h * dv)
    return (on * jax.nn.silu(og.astype(jnp.float32))).astype(og.dtype)


def hyena_filters(n, P):
    f32 = jnp.float32
    t = jnp.arange(n, dtype=f32)
    t_norm = t / n
    bands = jnp.linspace(1e-4, HY_BANDS - 1, HY_BANDS, dtype=f32)
    phase = (2 * math.pi / n) * t[:, None] * bands[None, :]
    z = jnp.concatenate([t_norm[:, None], jnp.cos(phase), -jnp.sin(phase)], axis=-1)
    freq = P['hy_sin_freq'].astype(f32)
    hid = jnp.sin(freq * (z @ P['hy_pos_w1'].astype(f32) + P['hy_pos_b1'].astype(f32)))
    hid = jnp.sin(freq * (hid @ P['hy_pos_w2'].astype(f32) + P['hy_pos_b2'].astype(f32)))
    filt = (hid @ P['hy_pos_w3'].astype(f32)).reshape(n, HY_ORDER, 2, HY_WIDTH)
    filt = filt * jnp.exp(-t_norm[:, None, None, None] * jnp.abs(P['hy_decay'].astype(f32)))
    fwd, bwd = filt[:, :, 0], filt[:, :, 1]
    taps = jnp.concatenate([fwd, jnp.zeros_like(fwd[:1]), jnp.flip(bwd[1:], axis=0)], axis=0)
    taps = taps / (jnp.sum(jnp.abs(taps), axis=0, keepdims=True) + EPS)
    return jnp.fft.rfft(taps, axis=0)


def short_conv(u, w, b):
    up = jnp.pad(u, ((0, 0), (1, 1), (0, 0)))
    return up[:, :-2] * w[0] + up[:, 1:-1] * w[1] + up[:, 2:] * w[2] + b


def hyena_mix(u, P):
    n = u.shape[1]
    filt = hyena_filters(n, P)
    uc = short_conv(u, P['hy_conv_w'], P['hy_conv_b']).astype(jnp.float32)
    x1, x2, z = jnp.split(uc, 3, axis=-1)
    skip = P['hy_skip'].astype(jnp.float32)
    for o, gate in enumerate((x1, x2)):
        zf = jnp.fft.rfft(z, n=2 * n, axis=1)
        conv = jnp.fft.irfft(zf * filt[None, :, o], n=2 * n, axis=1)[:, :n]
        z = gate * (conv + skip[o] * z)
    return z.astype(u.dtype)


def merge_branches(y_hy, y_gla, y_att, br_g, P):
    g_hy, g_gla, g_att = jnp.split(br_g, N_BRANCH, axis=-1)
    m = (jax.nn.sigmoid(g_hy) * (y_hy @ P['w_br_hy'])
         + jax.nn.sigmoid(g_gla) * (y_gla @ P['w_br_gla'])
         + jax.nn.sigmoid(g_att) * (y_att @ P['w_br_att']))
    return m @ P['w_out']


def token_mixer(h, hc, P, last):
    f32 = jnp.float32
    b, s, _ = h.shape
    cos, sin = axial_rope_tables(s)
    proj = h @ P['w_in']
    a_k, a_v, g_k, g_v, g_a = split_cols(proj[..., :N_KV_COLS], KV_SPLITS)
    a_q, g_q, g_og, hy_u, br_g = split_cols(proj[..., N_KV_COLS:], REST_SPLITS)
    cproj = hc @ (P['w_in'][:, :N_KV_COLS] if last else P['w_in'])
    c_ak, c_av, c_gk, c_gv, c_ga = split_cols(cproj[..., :N_KV_COLS], KV_SPLITS)

    k_ctx = rms_norm(to_heads(c_ak, ATT_KV_HEADS), P['k_norm_g'])
    v_ctx = to_heads(c_av, ATT_KV_HEADS)
    k_lat = apply_rope(rms_norm(to_heads(a_k, ATT_KV_HEADS), P['k_norm_g']), cos, sin)
    q_lat = apply_rope(rms_norm(q_heads(a_q), P['q_norm_g']), cos, sin)
    keys = jnp.concatenate([k_lat, k_ctx], axis=2)
    vals = jnp.concatenate([to_heads(a_v, ATT_KV_HEADS), v_ctx], axis=2)
    y_att = gqa_latent(q_lat, keys, vals)

    c_k = to_heads(c_gk, GLA_HEADS).astype(f32)
    c_v = to_heads(c_gv, GLA_HEADS).astype(f32)
    c_la_f = gla_log_decay(c_ga[..., :GLA_RANK], P['gla_wa2'][0], P['gla_ba'][0])
    c_la_b = gla_log_decay(c_ga[..., GLA_RANK:], P['gla_wa2'][1], P['gla_ba'][1])
    if last:
        s_f = gla_final_state(c_k, c_v, c_la_f)
        s_b = gla_final_state(tflip(c_k), tflip(c_v), tflip(c_la_b))
    else:
        c_aq, c_gq, c_og, c_hy, c_brg = split_cols(cproj[..., N_KV_COLS:], REST_SPLITS)
        s0 = jnp.zeros((b, GLA_HEADS, GLA_DK, GLA_DV), f32)
        c_o, s_f, s_b = gla_bidir(gla_q(c_gq), c_k, c_v, c_la_f, c_la_b, s0, s0)
    la_f = gla_log_decay(g_a[..., :GLA_RANK], P['gla_wa2'][0], P['gla_ba'][0])
    la_b = gla_log_decay(g_a[..., GLA_RANK:], P['gla_wa2'][1], P['gla_ba'][1])
    o_lat, _, _ = gla_bidir(gla_q(g_q), to_heads(g_k, GLA_HEADS).astype(f32),
                            to_heads(g_v, GLA_HEADS).astype(f32), la_f, la_b, s_f, s_b)
    y_gla = gla_output(o_lat, g_og, P['gla_norm_g'])

    y_hy = hyena_mix(hy_u, P)

    out = merge_branches(y_hy, y_gla, y_att, br_g, P)
    if last:
        return out, None
    yc_att = gqa_context(rms_norm(q_heads(c_aq), P['q_norm_g']), k_ctx, v_ctx)
    yc_gla = gla_output(c_o, c_og, P['gla_norm_g'])
    yc_hy = hyena_mix(c_hy, P)
    return out, merge_branches(yc_hy, yc_gla, yc_att, c_brg, P)


def grouped_moe(h, router_w, router_b, w_gate, w_up, w_down):
    f32 = jnp.float32
    shape = h.shape
    ht = h.reshape(-1, shape[-1])
    scores = jax.nn.sigmoid((ht @ router_w).astype(f32))
    sel = scores + router_b.astype(f32)
    group_score = lax.top_k(sel.reshape(-1, N_GROUPS, EXPERTS_PER_GROUP), TOP_K)[0].sum(-1)
    group = jnp.argmax(group_score, axis=-1)
    in_group = (jnp.arange(N_EXPERTS) // EXPERTS_PER_GROUP)[None, :] == group[:, None]
    _, idx = lax.top_k(jnp.where(in_group, sel, -jnp.inf), TOP_K)
    w = jnp.take_along_axis(scores, idx, axis=-1)
    w = w / jnp.sum(w, axis=-1, keepdims=True)
    gates = jnp.sum(jax.nn.one_hot(idx, N_EXPERTS, dtype=f32) * w[..., None], axis=1).astype(h.dtype)
    out = jnp.zeros_like(ht)
    for e in range(N_EXPERTS):
        hid = jax.nn.silu(ht @ w_gate[e]) * (ht @ w_up[e])
        out = out + gates[:, e:e + 1] * (hid @ w_down[e])
    return out.reshape(shape)


def setup_inputs(seed: int = 0) -> dict:
    key = jax.random.key(seed)
    keys = iter(jax.random.split(key, 48))

    def nrm(shape, scale):
        return jax.random.normal(next(keys), shape, jnp.float32) * scale

    D = D_MODEL
    decay_base = jnp.linspace(HY_DECAY_SLOW, HY_DECAY_FAST, HY_WIDTH, dtype=jnp.float32)
    return {
        'x': nrm((BATCH, SEQ, D), 1.0),
        'c': nrm((BATCH, D), 1.0),
        'ctx': nrm((BATCH, CTX_LEN, D), 1.0),
        'c_ctx': nrm((D,), 1.0),
        'w_mod': nrm((DEPTH, D, 6 * D), 0.5 * D ** -0.5),
        'b_mod': nrm((DEPTH, 6 * D), 0.02),
        'norm1_g': 1.0 + nrm((DEPTH, D), 0.05),
        'norm2_g': 1.0 + nrm((DEPTH, D), 0.05),
        'w_in': nrm((DEPTH, D, N_IN), D ** -0.5),
        'q_norm_g': 1.0 + nrm((DEPTH, HEAD_DIM), 0.05),
        'k_norm_g': 1.0 + nrm((DEPTH, HEAD_DIM), 0.05),
        'gla_wa2': nrm((DEPTH, 2, GLA_RANK, GLA_K_W), GLA_RANK ** -0.5),
        'gla_ba': nrm((DEPTH, 2, GLA_K_W), 0.02),
        'gla_norm_g': 1.0 + nrm((DEPTH, GLA_DV), 0.05),
        'hy_conv_w': nrm((DEPTH, 3, 3 * HY_WIDTH), 3 ** -0.5),
        'hy_conv_b': nrm((DEPTH, 3 * HY_WIDTH), 0.02),
        'hy_pos_w1': nrm((DEPTH, HY_EMB, HY_FFN), HY_EMB ** -0.5),
        'hy_pos_b1': nrm((DEPTH, HY_FFN), 0.02),
        'hy_sin_freq': 1.0 + nrm((DEPTH, HY_FFN), 0.05),
        'hy_pos_w2': nrm((DEPTH, HY_FFN, HY_FFN), HY_FFN ** -0.5),
        'hy_pos_b2': nrm((DEPTH, HY_FFN), 0.02),
        'hy_pos_w3': nrm((DEPTH, HY_FFN, HY_ORDER * 2 * HY_WIDTH), HY_FFN ** -0.5),
        'hy_decay': decay_base * (1.0 + nrm((DEPTH, HY_ORDER, 2, HY_WIDTH), 0.05)),
        'hy_skip': nrm((DEPTH, HY_ORDER, HY_WIDTH), 1.0),
        'w_br_hy': nrm((DEPTH, HY_WIDTH, D), HY_WIDTH ** -0.5),
        'w_br_gla': nrm((DEPTH, GLA_V_W, D), GLA_V_W ** -0.5),
        'w_br_att': nrm((DEPTH, ATT_Q_W, D), ATT_Q_W ** -0.5),
        'w_out': nrm((DEPTH, D, D), D ** -0.5),
        'router_w': nrm((D, N_EXPERTS), D ** -0.5),
        'router_b': nrm((N_EXPERTS,), 0.01),
        'moe_w_gate': nrm((DEPTH, N_EXPERTS, D, D_EXPERT), D ** -0.5),
        'moe_w_up': nrm((DEPTH, N_EXPERTS, D, D_EXPERT), D ** -0.5),
        'moe_w_down': nrm((DEPTH, N_EXPERTS, D_EXPERT, D), D_EXPERT ** -0.5),
    }


def reference(x, c, ctx, c_ctx, w_mod, b_mod, norm1_g, norm2_g, w_in, q_norm_g, k_norm_g,
              gla_wa2, gla_ba, gla_norm_g, hy_conv_w, hy_conv_b, hy_pos_w1, hy_pos_b1,
              hy_sin_freq, hy_pos_w2, hy_pos_b2, hy_pos_w3, hy_decay, hy_skip,
              w_br_hy, w_br_gla, w_br_att, w_out, router_w, router_b,
              moe_w_gate, moe_w_up, moe_w_down):
    xc = ctx
    sc = jax.nn.silu(c)
    scc = jax.nn.silu(c_ctx)
    for l in range(DEPTH):
        last = l == DEPTH - 1
        P = {
            'w_in': w_in[l], 'q_norm_g': q_norm_g[l], 'k_norm_g': k_norm_g[l],
            'gla_wa2': gla_wa2[l], 'gla_ba': gla_ba[l], 'gla_norm_g': gla_norm_g[l],
            'hy_conv_w': hy_conv_w[l], 'hy_conv_b': hy_conv_b[l],
            'hy_pos_w1': hy_pos_w1[l], 'hy_pos_b1': hy_pos_b1[l], 'hy_sin_freq': hy_sin_freq[l],
            'hy_pos_w2': hy_pos_w2[l], 'hy_pos_b2': hy_pos_b2[l], 'hy_pos_w3': hy_pos_w3[l],
            'hy_decay': hy_decay[l], 'hy_skip': hy_skip[l],
            'w_br_hy': w_br_hy[l], 'w_br_gla': w_br_gla[l], 'w_br_att': w_br_att[l], 'w_out': w_out[l],
        }
        shift1, scale1, gate1, shift2, scale2, gate2 = jnp.split(sc @ w_mod[l] + b_mod[l], 6, axis=-1)
        n_cmod = 2 if last else 6
        cmods = jnp.split(scc @ w_mod[l][:, :n_cmod * D_MODEL] + b_mod[l][:n_cmod * D_MODEL], n_cmod, axis=-1)

        h = modulate(rms_norm(x, norm1_g[l]), shift1, scale1)
        hc = modulate(rms_norm(xc, norm1_g[l]), cmods[0], cmods[1])
        y, yc = token_mixer(h, hc, P, last)
        x = x + gate1[:, None, :] * y
        h = modulate(rms_norm(x, norm2_g[l]), shift2, scale2)
        x = x + gate2[:, None, :] * grouped_moe(h, router_w, router_b, moe_w_gate[l], moe_w_up[l], moe_w_down[l])
        if not last:
            xc = xc + cmods[2] * yc
            hc = modulate(rms_norm(xc, norm2_g[l]), cmods[3], cmods[4])
            xc = xc + cmods[5] * grouped_moe(hc, router_w, router_b, moe_w_gate[l], moe_w_up[l], moe_w_down[l])
    return x
```

```python
import functools
import math

import numpy as np
import jax
import jax.numpy as jnp
from jax import lax
from jax.experimental import pallas as pl
from jax.experimental.pallas import tpu as pltpu

F32 = jnp.float32
BF16 = jnp.bfloat16
HIGHEST = lax.Precision.HIGHEST

D_MODEL = 1024
DEPTH = 2
GRID_W = 64
EPS = 1e-6

ATT_HEADS = 8
ATT_KV_HEADS = 2
ATT_GROUP = ATT_HEADS // ATT_KV_HEADS
HEAD_DIM = 128
ROPE_PAIRS_PER_AXIS = HEAD_DIM // 4
ROPE_THETA = 10000.0

GLA_HEADS = 4
GLA_DK = 64
GLA_DV = 128
GLA_RANK = 16
GLA_TAU = 16.0
GLA_CHUNK = 64

HY_WIDTH = 512
HY_ORDER = 2
HY_BANDS = 16
HY_EMB = 1 + 2 * HY_BANDS
HY_EMB_PAD = 40
HY_FFN = 64

N_EXPERTS = 16
N_GROUPS = 4
EXPERTS_PER_GROUP = N_EXPERTS // N_GROUPS
D_EXPERT = 512

ATT_Q_W = ATT_HEADS * HEAD_DIM
ATT_KV_W = ATT_KV_HEADS * HEAD_DIM
GLA_K_W = GLA_HEADS * GLA_DK
GLA_V_W = GLA_HEADS * GLA_DV
GLA_A_PAD = 128
GLA_COLS = 2 * GLA_K_W + 2 * GLA_V_W + GLA_A_PAD
ATT_COLS = ATT_Q_W + 2 * ATT_KV_W

LANES = 128
VMEM_LIMIT_BYTES = 56 * 1024 * 1024

DFT_GROUP = 8

NT_DIMS = (((1,), (1,)), ((), ()))
TN_DIMS = (((0,), (0,)), ((), ()))


def _params(*sem):
    return pltpu.CompilerParams(dimension_semantics=sem, vmem_limit_bytes=VMEM_LIMIT_BYTES)


def _sigmoid(x):
    return 1.0 / (1.0 + jnp.exp(-x))


def _norm_mod(x, g, shift, scale):
    ms = jnp.mean(x * x, axis=-1, keepdims=True)
    return (x * lax.rsqrt(ms + EPS) * g) * (1.0 + scale) + shift


def _mod_body(c_ref, w_ref, b_ref, o_ref):
    c = c_ref[...]
    s = c * _sigmoid(c)
    o_ref[0] = jnp.dot(s, w_ref[0], precision=HIGHEST, preferred_element_type=F32) + b_ref[0]


def _mods(c_all, w_mod, b_mod):
    tn = 512
    rows = c_all.shape[0]
    return pl.pallas_call(
        _mod_body,
        grid=(DEPTH, 6 * D_MODEL // tn),
        in_specs=[
            pl.BlockSpec((rows, D_MODEL), lambda l, j: (0, 0)),
            pl.BlockSpec((1, D_MODEL, tn), lambda l, j: (l, 0, j)),
            pl.BlockSpec((1, 1, tn), lambda l, j: (l, 0, j)),
        ],
        out_specs=pl.BlockSpec((1, rows, tn), lambda l, j: (l, 0, j)),
        out_shape=jax.ShapeDtypeStruct((DEPTH, rows, 6 * D_MODEL), F32),
        compiler_params=_params("parallel", "parallel"),
        name="mods",
    )(c_all, w_mod, b_mod.reshape(DEPTH, 1, 6 * D_MODEL))


def _proj_att_body(*refs, rope):
    if rope:
        x_ref, m_ref, g_ref, w_ref, qg_ref, kg_ref, cos_ref, sin_ref, q_ref, k_ref, v_ref = refs
    else:
        x_ref, m_ref, g_ref, w_ref, qg_ref, kg_ref, q_ref, k_ref, v_ref = refs
    m = m_ref[0]
    h = _norm_mod(x_ref[0], g_ref[...], m[0:1], m[1:2]).astype(BF16)
    p = jnp.dot(h, w_ref[...], preferred_element_type=F32)

    def head(t, gain):
        t = t * lax.rsqrt(jnp.mean(t * t, axis=-1, keepdims=True) + EPS) * gain
        if rope:
            t = t * cos_ref[...] + pltpu.roll(t, HEAD_DIM // 2, 1) * sin_ref[...]
        return t

    for i in range(ATT_HEADS):
        sl = slice(i * HEAD_DIM, (i + 1) * HEAD_DIM)
        q_ref[0, :, sl] = (head(p[:, sl], qg_ref[...]) * (HEAD_DIM ** -0.5)).astype(BF16)
    for i in range(ATT_KV_HEADS):
        sl = slice(i * HEAD_DIM, (i + 1) * HEAD_DIM)
        src = slice(ATT_Q_W + i * HEAD_DIM, ATT_Q_W + (i + 1) * HEAD_DIM)
        k_ref[0, :, sl] = head(p[:, src], kg_ref[...]).astype(BF16)
    v_ref[0] = p[:, ATT_Q_W + ATT_KV_W:].astype(BF16)


def _proj_att(x, mods, g, w, qg, kg, rope_tabs, tm):
    bm, r, _ = x.shape
    rope = rope_tabs is not None
    in_specs = [
        pl.BlockSpec((1, tm, D_MODEL), lambda b, i: (b, i, 0)),
        pl.BlockSpec((1, 6, D_MODEL), lambda b, i: (b, 0, 0)),
        pl.BlockSpec((1, D_MODEL), lambda b, i: (0, 0)),
        pl.BlockSpec((D_MODEL, ATT_COLS), lambda b, i: (0, 0)),
        pl.BlockSpec((1, HEAD_DIM), lambda b, i: (0, 0)),
        pl.BlockSpec((1, HEAD_DIM), lambda b, i: (0, 0)),
    ]
    args = [x, mods, g, w, qg, kg]
    if rope:
        in_specs += [pl.BlockSpec((tm, HEAD_DIM), lambda b, i: (i, 0))] * 2
        args += list(rope_tabs)
    return pl.pallas_call(
        functools.partial(_proj_att_body, rope=rope),
        grid=(bm, r // tm),
        in_specs=in_specs,
        out_specs=[
            pl.BlockSpec((1, tm, ATT_Q_W), lambda b, i: (b, i, 0)),
            pl.BlockSpec((1, tm, ATT_KV_W), lambda b, i: (b, i, 0)),
            pl.BlockSpec((1, tm, ATT_KV_W), lambda b, i: (b, i, 0)),
        ],
        out_shape=[
            jax.ShapeDtypeStruct((bm, r, ATT_Q_W), BF16),
            jax.ShapeDtypeStruct((bm, r, ATT_KV_W), BF16),
            jax.ShapeDtypeStruct((bm, r, ATT_KV_W), BF16),
        ],
        compiler_params=_params("parallel", "parallel"),
        name="proj_att",
    )(*args)


def _proj_plain_body(x_ref, m_ref, g_ref, w_ref, o_ref):
    m = m_ref[0]
    h = _norm_mod(x_ref[0], g_ref[...], m[0:1], m[1:2]).astype(BF16)
    o_ref[0] = jnp.dot(h, w_ref[...], preferred_element_type=F32)


def _proj_plain(x, mods, g, w, tm):
    bm, r, _ = x.shape
    n = w.shape[1]
    return pl.pallas_call(
        _proj_plain_body,
        grid=(bm, r // tm),
        in_specs=[
            pl.BlockSpec((1, tm, D_MODEL), lambda b, i: (b, i, 0)),
            pl.BlockSpec((1, 6, D_MODEL), lambda b, i: (b, 0, 0)),
            pl.BlockSpec((1, D_MODEL), lambda b, i: (0, 0)),
            pl.BlockSpec((D_MODEL, n), lambda b, i: (0, 0)),
        ],
        out_specs=pl.BlockSpec((1, tm, n), lambda b, i: (b, i, 0)),
        out_shape=jax.ShapeDtypeStruct((bm, r, n), F32),
        compiler_params=_params("parallel", "parallel"),
        name="proj_gla",
    )(x, mods, g, w)


def _proj_t_body(x_ref, m_ref, g_ref, wt_ref, o_ref):
    m = m_ref[0]
    h = _norm_mod(x_ref[0], g_ref[...], m[0:1], m[1:2]).astype(BF16)
    o_ref[0] = lax.dot_general(wt_ref[...], h, NT_DIMS, preferred_element_type=F32)


def _proj_t(x, mods, g, wt, tm):
    bm, r, _ = x.shape
    n = wt.shape[0]
    return pl.pallas_call(
        _proj_t_body,
        grid=(bm, r // tm),
        in_specs=[
            pl.BlockSpec((1, tm, D_MODEL), lambda b, i: (b, i, 0)),
            pl.BlockSpec((1, 6, D_MODEL), lambda b, i: (b, 0, 0)),
            pl.BlockSpec((1, D_MODEL), lambda b, i: (0, 0)),
            pl.BlockSpec((n, D_MODEL), lambda b, i: (0, 0)),
        ],
        out_specs=pl.BlockSpec((1, n, tm), lambda b, i: (b, 0, i)),
        out_shape=jax.ShapeDtypeStruct((bm, n, r), F32),
        compiler_params=_params("parallel", "parallel"),
        name="proj_hy",
    )(x, mods, g, wt)


def _attn_body(*refs, has_lat):
    if has_lat:
        q_ref, kl_ref, vl_ref, kc_ref, vc_ref, o_ref = refs
    else:
        q_ref, kc_ref, vc_ref, o_ref = refs
    q = q_ref[0]
    tq = q.shape[0]
    q4 = jnp.concatenate([q[:, g * HEAD_DIM:(g + 1) * HEAD_DIM] for g in range(ATT_GROUP)], axis=0)
    sc = lax.dot_general(q4, kc_ref[0], NT_DIMS, preferred_element_type=F32)
    m = jnp.max(sc, axis=-1, keepdims=True)
    if has_lat:
        sl = lax.dot_general(q4, kl_ref[0], NT_DIMS, preferred_element_type=F32)
        m = jnp.maximum(m, jnp.max(sl, axis=-1, keepdims=True))
    pc = jnp.exp(sc - m)
    den = jnp.sum(pc, axis=-1, keepdims=True)
    o = jnp.dot(pc.astype(BF16), vc_ref[0], preferred_element_type=F32)
    if has_lat:
        pl_ = jnp.exp(sl - m)
        den = den + jnp.sum(pl_, axis=-1, keepdims=True)
        o = o + jnp.dot(pl_.astype(BF16), vl_ref[0], preferred_element_type=F32)
    o = o / den
    o_ref[0] = jnp.concatenate([o[g * tq:(g + 1) * tq] for g in range(ATT_GROUP)], axis=1)


def _attn(q, k_lat, v_lat, k_ctx, v_ctx, tq):
    b, sq, _ = q.shape
    has_lat = k_lat is not None
    gw = ATT_GROUP * HEAD_DIM
    in_specs = [pl.BlockSpec((1, tq, gw), lambda bb, h, i: (bb, i, h))]
    args = [q]
    if has_lat:
        sk = k_lat.shape[1]
        in_specs += [pl.BlockSpec((1, sk, HEAD_DIM), lambda bb, h, i: (bb, 0, h))] * 2
        args += [k_lat, v_lat]
    sc = k_ctx.shape[1]
    in_specs += [pl.BlockSpec((1, sc, HEAD_DIM), lambda bb, h, i: (bb, 0, h))] * 2
    args += [k_ctx, v_ctx]
    return pl.pallas_call(
        functools.partial(_attn_body, has_lat=has_lat),
        grid=(b, ATT_KV_HEADS, sq // tq),
        in_specs=in_specs,
        out_specs=pl.BlockSpec((1, tq, gw), lambda bb, h, i: (bb, i, h)),
        out_shape=jax.ShapeDtypeStruct((b, sq, ATT_Q_W), F32),
        compiler_params=_params("parallel", "parallel", "parallel"),
        name="attn",
    )(*args)


def _gla_dir(q, k, v, a, wa2, ba, tri, st_ref, reverse):
    tt = q.shape[0]
    nc = tt // GLA_CHUNK
    z = jnp.dot(a, wa2, precision=HIGHEST, preferred_element_type=F32) + ba
    la = (jnp.minimum(z, 0.0) - jnp.log(1.0 + jnp.exp(-jnp.abs(z)))) * (1.0 / GLA_TAU)
    cum = jnp.dot(tri, la, precision=HIGHEST, preferred_element_type=F32)
    qd = (q * (GLA_DK ** -0.5) * jnp.exp(cum)).astype(BF16)
    ki = (k * jnp.exp(-cum)).astype(BF16)
    vb = v.astype(BF16)
    row = lax.broadcasted_iota(jnp.int32, (GLA_CHUNK, GLA_CHUNK), 0)
    col = lax.broadcasted_iota(jnp.int32, (GLA_CHUNK, GLA_CHUNK), 1)
    mask = (col >= row) if reverse else (col <= row)
    outs = [None] * nc
    for c in (range(nc - 1, -1, -1) if reverse else range(nc)):
        r0 = c * GLA_CHUNK
        rs = slice(r0, r0 + GLA_CHUNK)
        end = r0 if reverse else r0 + GLA_CHUNK - 1
        cl = cum[end:end + 1, :]
        kte = (k[rs] * jnp.exp(cl - cum[rs])).astype(BF16)
        st = st_ref[...]
        stb = st.astype(BF16)
        o_heads, upd = [], []
        for h in range(GLA_HEADS):
            cs = slice(h * GLA_DK, (h + 1) * GLA_DK)
            vs = slice(h * GLA_DV, (h + 1) * GLA_DV)
            qh, kh, vh = qd[rs, cs], ki[rs, cs], vb[rs, vs]
            att = lax.dot_general(qh, kh, NT_DIMS, preferred_element_type=F32)
            att = jnp.where(mask, att, 0.0).astype(BF16)
            o_heads.append(jnp.dot(att, vh, preferred_element_type=F32)
                           + lax.dot_general(qh, stb[:, cs], NT_DIMS, preferred_element_type=F32))
            upd.append(lax.dot_general(vh, kte[:, cs], TN_DIMS, preferred_element_type=F32))
        st_ref[...] = st * jnp.exp(cl) + jnp.concatenate(upd, axis=1)
        outs[c] = jnp.concatenate(o_heads, axis=1)
    return jnp.concatenate(outs, axis=0)


def _gla_body(qf, kf, vf, af, qb, kb, vb, ab, wa2, ba, tri, s0f, s0b,
              of, ob, sf_out, sb_out, sf_scr, sb_scr):
    @pl.when(pl.program_id(1) == 0)
    def _():
        sf_scr[...] = s0f[0]
        sb_scr[...] = s0b[0]

    a_f = af[0]
    a_b = ab[0]
    of[0] = _gla_dir(qf[0], kf[0], vf[0], a_f[:, :GLA_RANK], wa2[0], ba[0], tri[0], sf_scr, False)
    ob[0] = _gla_dir(qb[0], kb[0], vb[0], a_b[:, GLA_RANK:2 * GLA_RANK], wa2[1], ba[1], tri[1],
                     sb_scr, True)
    sf_out[0] = sf_scr[...]
    sb_out[0] = sb_scr[...]


def _gla(p, wa2, ba, tri, s0f, s0b, tt):
    b, s, _ = p.shape
    n = s // tt
    a_blk = (2 * GLA_K_W + 2 * GLA_V_W) // GLA_A_PAD

    def specs(rev):
        t = (lambda j: n - 1 - j) if rev else (lambda j: j)
        return [
            pl.BlockSpec((1, tt, GLA_K_W), lambda bb, j: (bb, t(j), 0)),
            pl.BlockSpec((1, tt, GLA_K_W), lambda bb, j: (bb, t(j), 1)),
            pl.BlockSpec((1, tt, GLA_V_W), lambda bb, j: (bb, t(j), 1)),
            pl.BlockSpec((1, tt, GLA_A_PAD), lambda bb, j: (bb, t(j), a_blk)),
        ]

    st_spec = pl.BlockSpec((1, GLA_DV, GLA_K_W), lambda bb, j: (bb, 0, 0))
    st_shape = jax.ShapeDtypeStruct((b, GLA_DV, GLA_K_W), F32)
    return pl.pallas_call(
        _gla_body,
        grid=(b, n),
        in_specs=specs(False) + specs(True) + [
            pl.BlockSpec((2, GLA_RANK, GLA_K_W), lambda bb, j: (0, 0, 0)),
            pl.BlockSpec((2, 1, GLA_K_W), lambda bb, j: (0, 0, 0)),
            pl.BlockSpec((2, tt, tt), lambda bb, j: (0, 0, 0)),
            st_spec, st_spec,
        ],
        out_specs=[
            pl.BlockSpec((1, tt, GLA_V_W), lambda bb, j: (bb, j, 0)),
            pl.BlockSpec((1, tt, GLA_V_W), lambda bb, j: (bb, n - 1 - j, 0)),
            st_spec, st_spec,
        ],
        out_shape=[
            jax.ShapeDtypeStruct((b, s, GLA_V_W), F32),
            jax.ShapeDtypeStruct((b, s, GLA_V_W), F32),
            st_shape, st_shape,
        ],
        scratch_shapes=[pltpu.VMEM((GLA_DV, GLA_K_W), F32), pltpu.VMEM((GLA_DV, GLA_K_W), F32)],
        compiler_params=_params("parallel", "arbitrary"),
        name="gla",
    )(p, p, p, p, p, p, p, p, wa2, ba, tri, s0f, s0b)


def _gla_tri(tt):
    t = np.arange(tt)
    same = (t[:, None] // GLA_CHUNK) == (t[None, :] // GLA_CHUNK)
    fwd = same & (t[None, :] <= t[:, None])
    bwd = same & (t[None, :] >= t[:, None])
    return jnp.asarray(np.stack([fwd, bwd]).astype(np.float32))


def _hy_filter_body(zt_ref, w1t_ref, b1_ref, fr_ref, w2t_ref, b2_ref, w3t_ref, dec_ref,
                    f_ref, b_ref):
    zt = zt_ref[...]
    fr = fr_ref[...]
    h1 = jnp.sin(fr * (jnp.dot(w1t_ref[...], zt, precision=HIGHEST, preferred_element_type=F32)
                       + b1_ref[...]))
    h2 = jnp.sin(fr * (jnp.dot(w2t_ref[...], h1, precision=HIGHEST, preferred_element_type=F32)
                       + b2_ref[...]))
    tn = zt[0:1, :]
    f = jnp.dot(w3t_ref[0, 0], h2, precision=HIGHEST, preferred_element_type=F32)
    f = f * jnp.exp(-tn * jnp.abs(dec_ref[0, 0]))
    b = jnp.dot(w3t_ref[0, 1], h2, precision=HIGHEST, preferred_element_type=F32)
    b = b * jnp.exp(-tn * jnp.abs(dec_ref[0, 1]))
    lane = lax.broadcasted_iota(jnp.int32, b.shape, 1)
    b = jnp.where(lane == 0, 0.0, b)
    den = (jnp.sum(jnp.abs(f), axis=-1, keepdims=True)
           + jnp.sum(jnp.abs(b), axis=-1, keepdims=True) + EPS)
    f_ref[0] = f / den
    b_ref[0] = b / den


def _hy_filters(n, w1, b1, fr, w2, b2, w3, dec):
    t = jnp.arange(n, dtype=F32)
    t_norm = t / n
    bands = jnp.linspace(1e-4, HY_BANDS - 1, HY_BANDS, dtype=F32)
    phase = (2 * math.pi / n) * t[:, None] * bands[None, :]
    z = jnp.concatenate([t_norm[:, None], jnp.cos(phase), -jnp.sin(phase)], axis=-1)
    zt = jnp.pad(z.T, ((0, HY_EMB_PAD - HY_EMB), (0, 0)))
    w1t = jnp.pad(w1.T, ((0, 0), (0, HY_EMB_PAD - HY_EMB)))
    w3t = w3.T.reshape(HY_ORDER, 2, HY_WIDTH, HY_FFN)
    ct = 128
    col = lambda v: v.reshape(HY_FFN, 1)
    full = lambda shape: pl.BlockSpec(shape, lambda o, c: (0,) * len(shape))
    out_spec = pl.BlockSpec((1, ct, n), lambda o, c: (o, c, 0))
    out_shape = jax.ShapeDtypeStruct((HY_ORDER, HY_WIDTH, n), F32)
    return pl.pallas_call(
        _hy_filter_body,
        grid=(HY_ORDER, HY_WIDTH // ct),
        in_specs=[
            full((HY_EMB_PAD, n)), full((HY_FFN, HY_EMB_PAD)), full((HY_FFN, 1)), full((HY_FFN, 1)),
            full((HY_FFN, HY_FFN)), full((HY_FFN, 1)),
            pl.BlockSpec((1, 2, ct, HY_FFN), lambda o, c: (o, 0, c, 0)),
            pl.BlockSpec((1, 2, ct, 1), lambda o, c: (o, 0, c, 0)),
        ],
        out_specs=[out_spec, out_spec],
        out_shape=[out_shape, out_shape],
        compiler_params=_params("parallel", "parallel"),
        name="hy_filter",
    )(zt, w1t, col(b1), col(fr), w2.T, col(b2), w3t, dec.reshape(HY_ORDER, 2, HY_WIDTH, 1))


def _dft_consts(n1):
    n = n1 * LANES
    half = n1 // 2
    g = DFT_GROUP
    eye = np.eye(g)
    k1 = np.arange(n1)[:, None]
    i1 = np.arange(half)[None, :]
    a1 = 2 * np.pi * k1 * i1 / n1
    g1k = np.concatenate([np.kron(eye, np.cos(a1)), np.kron(eye, -np.sin(a1))], axis=0)
    i2 = np.arange(LANES)[None, :]
    at = 2 * np.pi * k1 * i2 / n
    twr = np.tile(np.cos(at), (g, 1))
    twi = np.tile(-np.sin(at), (g, 1))
    a2 = 2 * np.pi * np.arange(LANES)[:, None] * np.arange(LANES)[None, :] / LANES
    f2 = np.concatenate([np.cos(a2), -np.sin(a2)], axis=1)
    f2i = np.concatenate([np.cos(a2), np.sin(a2)], axis=1)
    ai = a1.T
    gik = np.concatenate([np.kron(eye, np.cos(ai)), np.kron(eye, -np.sin(ai))], axis=1)
    bf = lambda m: jnp.asarray(m, dtype=BF16)
    return dict(g1k=bf(g1k), twr=jnp.asarray(twr, F32), twi=jnp.asarray(twi, F32),
                f2=bf(f2), f2i=bf(f2i), gik=bf(gik))


def _cplx_lanes(p):
    h, w = p.shape[0] // 2, p.shape[1] // 2
    return p[:h, :w] - p[h:, w:], p[:h, w:] + p[h:, :w]


def _fwd_dft(x, g1k, twr, twi, f2):
    a = jnp.dot(g1k, x.astype(BF16), preferred_element_type=F32)
    h = a.shape[0] // 2
    ar, ai = a[:h], a[h:]
    tr = ar * twr - ai * twi
    ti = ar * twi + ai * twr
    p = jnp.dot(jnp.concatenate([tr, ti], axis=0).astype(BF16), f2, preferred_element_type=F32)
    return _cplx_lanes(p)


def _inv_dft(yr, yi, twr, twi, f2i, gik):
    p = jnp.dot(jnp.concatenate([yr, yi], axis=0).astype(BF16), f2i, preferred_element_type=F32)
    br, bi = _cplx_lanes(p)
    tr = br * twr + bi * twi
    ti = bi * twr - br * twi
    return jnp.dot(gik, jnp.concatenate([tr, ti], axis=0).astype(BF16), preferred_element_type=F32)


def _hy_spec_body(f_ref, b_ref, g1k, twr, twi, f2, hr_ref, hi_ref, *, groups, n1):
    half = n1 // 2
    scale = 1.0 / (n1 * LANES)

    def body(g, carry):
        rows = pl.ds(pl.multiple_of(g * (DFT_GROUP * half), DFT_GROUP * half), DFT_GROUP * half)
        zfr, zfi = _fwd_dft(f_ref[rows, :], g1k[...], twr[...], twi[...], f2[...])
        zbr, zbi = _fwd_dft(b_ref[rows, :], g1k[...], twr[...], twi[...], f2[...])
        orow = pl.ds(pl.multiple_of(g * (DFT_GROUP * n1), DFT_GROUP * n1), DFT_GROUP * n1)
        hr_ref[orow, :] = (zfr + zbr) * scale
        hi_ref[orow, :] = (zfi - zbi) * scale
        return carry

    lax.fori_loop(0, groups, body, 0)


def _const_spec(arr, ngrid):
    nd = arr.ndim
    return pl.BlockSpec(arr.shape, lambda *_: (0,) * nd)


def _hy_spec(f, b, consts, n1):
    half = n1 // 2
    chans = f.shape[0] // half
    ct = 64
    groups = ct // DFT_GROUP
    cs = [consts[k] for k in ("g1k", "twr", "twi", "f2")]
    in_spec = pl.BlockSpec((ct * half, LANES), lambda c: (c, 0))
    out_spec = pl.BlockSpec((ct * n1, LANES), lambda c: (c, 0))
    out_shape = jax.ShapeDtypeStruct((chans * n1, LANES), F32)
    return pl.pallas_call(
        functools.partial(_hy_spec_body, groups=groups, n1=n1),
        grid=(chans // ct,),
        in_specs=[in_spec, in_spec] + [_const_spec(c, 1) for c in cs],
        out_specs=[out_spec, out_spec],
        out_shape=[out_shape, out_shape],
        compiler_params=_params("parallel"),
        name="hy_spec",
    )(f, b, *cs)


def _hy_main_body(u1_ref, u2_ref, uz_ref, p_ref, hr_ref, hi_ref, g1k, twr, twi, f2, f2i, gik,
                  o_ref, *, groups, n1):
    half = n1 // 2
    rows_g = DFT_GROUP * half
    shape = (rows_g, LANES)
    rowi = lax.broadcasted_iota(jnp.int32, shape, 0) % half
    lane = lax.broadcasted_iota(jnp.int32, shape, 1)
    lane_first = lane == 0
    lane_last = lane == LANES - 1
    t_first = (rowi == 0) & lane_first
    t_last = (rowi == half - 1) & lane_last

    def prev(x):
        r = pltpu.roll(x, 1, 1)
        return jnp.where(t_first, 0.0, jnp.where(lane_first, pltpu.roll(r, 1, 0), r))

    def nxt(x):
        l = pltpu.roll(x, LANES - 1, 1)
        return jnp.where(t_last, 0.0, jnp.where(lane_last, pltpu.roll(l, rows_g - 1, 0), l))

    def body(g, carry):
        cs = pl.ds(pl.multiple_of(g * DFT_GROUP, DFT_GROUP), DFT_GROUP)
        pr = p_ref[pl.ds(pl.multiple_of(g * rows_g, rows_g), rows_g), :]
        col = lambda j: pr[:, j:j + 1]

        def sconv(ref, j):
            x = ref[0, cs].reshape(shape)
            return prev(x) * col(j) + x * col(j + 1) + nxt(x) * col(j + 2) + col(j + 3)

        x1 = sconv(u1_ref, 0)
        x2 = sconv(u2_ref, 4)
        z = sconv(uz_ref, 8)
        hrow = pl.ds(pl.multiple_of(g * (DFT_GROUP * n1), DFT_GROUP * n1), DFT_GROUP * n1)

        def conv(v, o):
            zr, zi = _fwd_dft(v, g1k[...], twr[...], twi[...], f2[...])
            hr = hr_ref[o, hrow, :]
            hi = hi_ref[o, hrow, :]
            return _inv_dft(zr * hr - zi * hi, zr * hi + zi * hr, twr[...], twi[...], f2i[...], gik[...])

        z = x1 * (conv(z, 0) + col(12) * z)
        z = x2 * (conv(z, 1) + col(13) * z)
        o_ref[0, cs] = z.reshape(DFT_GROUP, half, LANES)
        return carry

    lax.fori_loop(0, groups, body, 0)


def _hy_main(ut, prm, hr, hi, consts, n1):
    b = ut.shape[0]
    half = n1 // 2
    ct = 64
    nct = HY_WIDTH // ct
    cs = [consts[k] for k in ("g1k", "twr", "twi", "f2", "f2i", "gik")]
    u_spec = lambda sec: pl.BlockSpec((1, ct, half, LANES), lambda c, bb: (bb, sec * nct + c, 0, 0))
    h_spec = pl.BlockSpec((HY_ORDER, ct * n1, LANES), lambda c, bb: (0, c, 0))
    return pl.pallas_call(
        functools.partial(_hy_main_body, groups=ct // DFT_GROUP, n1=n1),
        grid=(nct, b),
        in_specs=[u_spec(0), u_spec(1), u_spec(2),
                  pl.BlockSpec((ct * half, 16), lambda c, bb: (c, 0)),
                  h_spec, h_spec] + [_const_spec(c, 2) for c in cs],
        out_specs=pl.BlockSpec((1, ct, half, LANES), lambda c, bb: (bb, c, 0, 0)),
        out_shape=jax.ShapeDtypeStruct((b, HY_WIDTH, half, LANES), F32),
        compiler_params=_params("parallel", "parallel"),
        name="hy_main",
    )(ut, ut, ut, prm, hr, hi, *cs)


def _hyc_consts(n):
    nn = 2 * n
    a = 2 * np.pi * np.arange(n)[:, None] * np.arange(nn)[None, :] / nn
    fc = np.concatenate([np.cos(a), -np.sin(a)], axis=1)
    fi = np.concatenate([np.cos(a.T), -np.sin(a.T)], axis=0)
    return jnp.asarray(fc, BF16), jnp.asarray(fi, BF16)


def _hyc_spec_body(f_ref, b_ref, fc_ref, hr_ref, hi_ref):
    nn = hr_ref.shape[-1]
    zf = jnp.dot(f_ref[...].astype(BF16), fc_ref[...], preferred_element_type=F32)
    zb = jnp.dot(b_ref[...].astype(BF16), fc_ref[...], preferred_element_type=F32)
    hr_ref[...] = (zf[:, :nn] + zb[:, :nn]) * (1.0 / nn)
    hi_ref[...] = (zf[:, nn:] - zb[:, nn:]) * (1.0 / nn)


def _hyc_spec(f, b, fc):
    rows, n = f.shape
    out_shape = jax.ShapeDtypeStruct((rows, 2 * n), F32)
    return pl.pallas_call(
        _hyc_spec_body,
        out_shape=[out_shape, out_shape],
        compiler_params=pltpu.CompilerParams(vmem_limit_bytes=VMEM_LIMIT_BYTES),
        name="hyc_spec",
    )(f, b, fc)


def _hyc_main_body(u1_ref, u2_ref, uz_ref, p_ref, hr_ref, hi_ref, fc_ref, fi_ref, o_ref):
    shape = u1_ref.shape[1:]
    n = shape[1]
    nn = 2 * n
    lane = lax.broadcasted_iota(jnp.int32, shape, 1)
    pr = p_ref[...]
    col = lambda j: pr[:, j:j + 1]

    def sconv(ref, j):
        x = ref[0]
        p = jnp.where(lane == 0, 0.0, pltpu.roll(x, 1, 1))
        q = jnp.where(lane == n - 1, 0.0, pltpu.roll(x, n - 1, 1))
        return p * col(j) + x * col(j + 1) + q * col(j + 2) + col(j + 3)

    def conv(v, o):
        s = jnp.dot(v.astype(BF16), fc_ref[...], preferred_element_type=F32)
        zr, zi = s[:, :nn], s[:, nn:]
        hr, hi = hr_ref[o], hi_ref[o]
        y = jnp.concatenate([zr * hr - zi * hi, zr * hi + zi * hr], axis=1).astype(BF16)
        return jnp.dot(y, fi_ref[...], preferred_element_type=F32)

    x1 = sconv(u1_ref, 0)
    x2 = sconv(u2_ref, 4)
    z = sconv(uz_ref, 8)
    z = x1 * (conv(z, 0) + col(12) * z)
    o_ref[0] = x2 * (conv(z, 1) + col(13) * z)


def _hyc_main(ut, prm, hr, hi, fc, fi, nb, n):
    ct = 128
    nct = HY_WIDTH // ct
    u_spec = lambda sec: pl.BlockSpec((1, ct, n), lambda bb, c: (0, sec * nct + c, bb))
    h_spec = pl.BlockSpec((HY_ORDER, ct, 2 * n), lambda bb, c: (0, c, 0))
    return pl.pallas_call(
        _hyc_main_body,
        grid=(nb, nct),
        in_specs=[u_spec(0), u_spec(1), u_spec(2),
                  pl.BlockSpec((ct, 16), lambda bb, c: (c, 0)),
                  h_spec, h_spec, _const_spec(fc, 2), _const_spec(fi, 2)],
        out_specs=pl.BlockSpec((1, ct, n), lambda bb, c: (0, c, bb)),
        out_shape=jax.ShapeDtypeStruct((1, HY_WIDTH, nb * n), F32),
        compiler_params=_params("parallel", "parallel"),
        name="hyc_main",
    )(ut, ut, ut, prm, hr, hi, fc, fi)


def _merge_body(x_ref, m_ref, g_ref, wbrg, yh_ref, of_ref, ob_ref, og_ref, gn_ref, ya_ref,
                wh, wg, wa, wo, o_ref):
    x = x_ref[0]
    m = m_ref[0]
    h = _norm_mod(x, g_ref[...], m[0:1], m[1:2]).astype(BF16)
    gates = jnp.dot(h, wbrg[...], preferred_element_type=F32)
    yh = yh_ref[0].T
    o = of_ref[0] + ob_ref[0]
    parts = []
    for hh in range(GLA_HEADS):
        t = o[:, hh * GLA_DV:(hh + 1) * GLA_DV]
        parts.append(t * lax.rsqrt(jnp.mean(t * t, axis=-1, keepdims=True) + EPS) * gn_ref[...])
    og = og_ref[0]
    yg = jnp.concatenate(parts, axis=1) * (og * _sigmoid(og))
    d = D_MODEL
    mm = (_sigmoid(gates[:, :d]) * jnp.dot(yh.astype(BF16), wh[...], preferred_element_type=F32)
          + _sigmoid(gates[:, d:2 * d]) * jnp.dot(yg.astype(BF16), wg[...], preferred_element_type=F32)
          + _sigmoid(gates[:, 2 * d:]) * jnp.dot(ya_ref[0].astype(BF16), wa[...],
                                                  preferred_element_type=F32))
    out = jnp.dot(mm.astype(BF16), wo[...], preferred_element_type=F32)
    o_ref[0] = x + m[2:3] * out


def _merge(x, mods, g, wbrg, yht, of, ob, pg, gn, ya, wh, wg, wa, wo, tm):
    bm, r, _ = x.shape
    row = lambda w: pl.BlockSpec((1, tm, w), lambda b, i: (b, i, 0))
    full = lambda a: pl.BlockSpec(a.shape, lambda b, i: (0,) * a.ndim)
    og_blk = (2 * GLA_K_W + GLA_V_W) // GLA_V_W
    return pl.pallas_call(
        _merge_body,
        grid=(bm, r // tm),
        in_specs=[
            row(D_MODEL),
            pl.BlockSpec((1, 6, D_MODEL), lambda b, i: (b, 0, 0)),
            full(g), full(wbrg),
            pl.BlockSpec((1, HY_WIDTH, tm), lambda b, i: (b, 0, i)),
            row(GLA_V_W), row(GLA_V_W),
            pl.BlockSpec((1, tm, GLA_V_W), lambda b, i: (b, i, og_blk)),
            full(gn), row(ATT_Q_W), full(wh), full(wg), full(wa), full(wo),
        ],
        out_specs=row(D_MODEL),
        out_shape=jax.ShapeDtypeStruct(x.shape, F32),
        compiler_params=_params("parallel", "parallel"),
        name="merge",
    )(x, mods, g, wbrg, yht, of, ob, pg, gn, ya, wh, wg, wa, wo)


def _route(logits, rb):
    scores = _sigmoid(logits)
    sel = scores + rb
    lane = lax.broadcasted_iota(jnp.int32, sel.shape, 1).astype(F32)
    neg = -jnp.inf

    def top2(v):
        m1 = jnp.max(v, axis=-1, keepdims=True)
        i1 = jnp.min(jnp.where(v == m1, lane, float(N_EXPERTS)), axis=-1, keepdims=True)
        v2 = jnp.where(lane == i1, neg, v)
        m2 = jnp.max(v2, axis=-1, keepdims=True)
        i2 = jnp.min(jnp.where(v2 == m2, lane, float(N_EXPERTS)), axis=-1, keepdims=True)
        return m1, m2, i1, i2

    group_of = jnp.floor(lane * (1.0 / EXPERTS_PER_GROUP))
    best = None
    best_g = None
    for g in range(N_GROUPS):
        m1, m2, _, _ = top2(jnp.where(group_of == float(g), sel, neg))
        gs = m1 + m2
        if best is None:
            best, best_g = gs, jnp.zeros_like(gs)
        else:
            better = gs > best
            best_g = jnp.where(better, float(g), best_g)
            best = jnp.where(better, gs, best)
    _, _, i1, i2 = top2(jnp.where(group_of == best_g, sel, neg))
    w = jnp.where((lane == i1) | (lane == i2), scores, 0.0)
    return w / jnp.sum(w, axis=-1, keepdims=True)


def _moe_body(x_ref, m_ref, g_ref, rw_ref, rb_ref, wg_ref, wu_ref, wd_ref, o_ref,
              h_scr, gate_scr, acc_scr):
    e = pl.program_id(2)

    @pl.when(e == 0)
    def _():
        m = m_ref[0]
        h = _norm_mod(x_ref[0], g_ref[...], m[3:4], m[4:5])
        h_scr[...] = h.astype(BF16)
        logits = jnp.dot(h, rw_ref[...], precision=HIGHEST, preferred_element_type=F32)
        gate_scr[...] = _route(logits, rb_ref[...])
        acc_scr[...] = jnp.zeros_like(acc_scr)

    h = h_scr[...]
    a = jnp.dot(h, wg_ref[0], preferred_element_type=F32)
    b = jnp.dot(h, wu_ref[0], preferred_element_type=F32)
    hid = (a * _sigmoid(a) * b).astype(BF16)
    y = jnp.dot(hid, wd_ref[0], preferred_element_type=F32)
    gates = gate_scr[...]
    lane = lax.broadcasted_iota(jnp.int32, gates.shape, 1)
    ge = jnp.sum(jnp.where(lane == e, gates, 0.0), axis=-1, keepdims=True)
    acc_scr[...] += ge * y

    @pl.when(e == N_EXPERTS - 1)
    def _():
        o_ref[0] = x_ref[0] + m_ref[0][5:6] * acc_scr[...]


def _moe(x, mods, g, rw, rb, wg, wu, wd, tm):
    bm, r, _ = x.shape
    row = pl.BlockSpec((1, tm, D_MODEL), lambda b, i, e: (b, i, 0))
    return pl.pallas_call(
        _moe_body,
        grid=(bm, r // tm, N_EXPERTS),
        in_specs=[
            row,
            pl.BlockSpec((1, 6, D_MODEL), lambda b, i, e: (b, 0, 0)),
            pl.BlockSpec((1, D_MODEL), lambda b, i, e: (0, 0)),
            pl.BlockSpec((D_MODEL, N_EXPERTS), lambda b, i, e: (0, 0)),
            pl.BlockSpec((1, N_EXPERTS), lambda b, i, e: (0, 0)),
            pl.BlockSpec((1, D_MODEL, D_EXPERT), lambda b, i, e: (e, 0, 0)),
            pl.BlockSpec((1, D_MODEL, D_EXPERT), lambda b, i, e: (e, 0, 0)),
            pl.BlockSpec((1, D_EXPERT, D_MODEL), lambda b, i, e: (e, 0, 0)),
        ],
        out_specs=row,
        out_shape=jax.ShapeDtypeStruct(x.shape, F32),
        scratch_shapes=[
            pltpu.VMEM((tm, D_MODEL), BF16),
            pltpu.VMEM((tm, N_EXPERTS), F32),
            pltpu.VMEM((tm, D_MODEL), F32),
        ],
        compiler_params=_params("parallel", "parallel", "arbitrary"),
        name="moe",
    )(x, mods, g, rw, rb, wg, wu, wd)


def _rope_tables(n_tokens):
    rows = n_tokens // GRID_W
    row = jnp.broadcast_to(jnp.arange(rows)[:, None], (rows, GRID_W)).reshape(-1).astype(F32)
    col = jnp.broadcast_to(jnp.arange(GRID_W)[None, :], (rows, GRID_W)).reshape(-1).astype(F32)
    inv_freq = ROPE_THETA ** (-jnp.arange(ROPE_PAIRS_PER_AXIS, dtype=F32) / ROPE_PAIRS_PER_AXIS)
    ang = jnp.concatenate([row[:, None] * inv_freq, col[:, None] * inv_freq], axis=-1)
    cos, sin = jnp.cos(ang), jnp.sin(ang)
    return jnp.concatenate([cos, cos], axis=-1), jnp.concatenate([-sin, sin], axis=-1)


def _split_w_in(w):
    sizes = (ATT_KV_W, ATT_KV_W, GLA_K_W, GLA_V_W, 2 * GLA_RANK,
             ATT_Q_W, GLA_K_W, GLA_V_W, 3 * HY_WIDTH, 3 * D_MODEL)
    cuts = [int(v) for v in np.cumsum(sizes)[:-1]]
    a_k, a_v, g_k, g_v, g_a, a_q, g_q, g_og, hy_u, br_g = jnp.split(w, cuts, axis=1)
    perm = np.concatenate([np.arange(0, HEAD_DIM, 2), np.arange(1, HEAD_DIM, 2)])
    perm_q = np.concatenate([h * HEAD_DIM + perm for h in range(ATT_HEADS)])
    perm_k = np.concatenate([h * HEAD_DIM + perm for h in range(ATT_KV_HEADS)])
    w_att = jnp.concatenate([a_q[:, perm_q], a_k[:, perm_k], a_v], axis=1).astype(BF16)
    pad = jnp.zeros((D_MODEL, GLA_A_PAD - 2 * GLA_RANK), w.dtype)
    w_gla = jnp.concatenate([g_q, g_k, g_v, g_og, g_a, pad], axis=1).astype(BF16)
    return w_att, w_gla, hy_u.T.astype(BF16), br_g.astype(BF16), perm


def kernel(x, c, ctx, c_ctx, w_mod, b_mod, norm1_g, norm2_g, w_in, q_norm_g, k_norm_g, gla_wa2, gla_ba, gla_norm_g, hy_conv_w, hy_conv_b, hy_pos_w1, hy_pos_b1, hy_sin_freq, hy_pos_w2, hy_pos_b2, hy_pos_w3, hy_decay, hy_skip, w_br_hy, w_br_gla, w_br_att, w_out, router_w, router_b, moe_w_gate, moe_w_up, moe_w_down):
    nb, seq, d = x.shape
    nctx = ctx.shape[1]
    n1_lat = 2 * seq // LANES

    c_all = jnp.concatenate([c, c_ctx[None, :], jnp.zeros((16 - nb - 1, d), F32)], axis=0)
    mods_all = _mods(c_all, w_mod, b_mod)

    rope_tabs = _rope_tables(seq)
    consts = _dft_consts(n1_lat)
    fc, fi = _hyc_consts(nctx)
    tri_lat = _gla_tri(256)
    zero_state = jnp.zeros((nb, GLA_DV, GLA_K_W), F32)
    rb = router_b.reshape(1, N_EXPERTS)

    xc = ctx.reshape(1, nb * nctx, d)
    for l in range(DEPTH):
        last = l == DEPTH - 1
        mods_lat = mods_all[l, :nb].reshape(nb, 6, d)
        mods_ctx = mods_all[l, nb:nb + 1].reshape(1, 6, d)
        g1 = norm1_g[l].reshape(1, d)
        g2 = norm2_g[l].reshape(1, d)
        w_att, w_gla, w_hyt, w_brg, perm = _split_w_in(w_in[l])
        qg = q_norm_g[l][perm].reshape(1, HEAD_DIM)
        kg = k_norm_g[l][perm].reshape(1, HEAD_DIM)
        wa2 = gla_wa2[l]
        ba = gla_ba[l].reshape(2, 1, GLA_K_W)
        gn = gla_norm_g[l].reshape(1, GLA_DV)
        wh, wg, wa, wo = (w_br_hy[l].astype(BF16), w_br_gla[l].astype(BF16),
                          w_br_att[l].astype(BF16), w_out[l].astype(BF16))
        cw, cb = hy_conv_w[l], hy_conv_b[l]
        sec = lambda s: [cw[0, s], cw[1, s], cw[2, s], cb[s]]
        w_ = HY_WIDTH
        prm = jnp.stack(sec(slice(0, w_)) + sec(slice(w_, 2 * w_)) + sec(slice(2 * w_, 3 * w_))
                        + [hy_skip[l, 0], hy_skip[l, 1], jnp.zeros((w_,), F32), jnp.zeros((w_,), F32)],
                        axis=1)
        filt_args = (hy_pos_w1[l], hy_pos_b1[l], hy_sin_freq[l], hy_pos_w2[l], hy_pos_b2[l],
                     hy_pos_w3[l], hy_decay[l])

        q, k, v = _proj_att(x, mods_lat, g1, w_att, qg, kg, rope_tabs, 512)
        qc, kc, vc = _proj_att(xc, mods_ctx, g1, w_att, qg, kg, None, 256)
        kc = kc.reshape(nb, nctx, ATT_KV_W)
        vc = vc.reshape(nb, nctx, ATT_KV_W)
        pg = _proj_plain(x, mods_lat, g1, w_gla, 512)
        pgc = _proj_plain(xc, mods_ctx, g1, w_gla, 256)
        ut = _proj_t(x, mods_lat, g1, w_hyt, 512)

        y_att = _attn(q, k, v, kc, vc, 128)

        cof, cob, s_f, s_b = _gla(pgc.reshape(nb, nctx, GLA_COLS), wa2, ba, _gla_tri(nctx),
                                  zero_state, zero_state, nctx)
        of, ob, _, _ = _gla(pg, wa2, ba, tri_lat, s_f, s_b, 256)

        ff, fb = _hy_filters(seq, *filt_args)
        hr, hi = _hy_spec(ff.reshape(-1, LANES), fb.reshape(-1, LANES), consts, n1_lat)
        hr = hr.reshape(HY_ORDER, HY_WIDTH * n1_lat, LANES)
        hi = hi.reshape(HY_ORDER, HY_WIDTH * n1_lat, LANES)
        y_hyt = _hy_main(ut.reshape(nb, 3 * HY_WIDTH, n1_lat // 2, LANES),
                         jnp.repeat(prm, n1_lat // 2, axis=0), hr, hi, consts, n1_lat)
        y_hyt = y_hyt.reshape(nb, HY_WIDTH, seq)

        x = _merge(x, mods_lat, g1, w_brg, y_hyt, of, ob, pg, gn, y_att, wh, wg, wa, wo, 256)
        moe_w = (moe_w_gate[l].astype(BF16), moe_w_up[l].astype(BF16), moe_w_down[l].astype(BF16))
        x = _moe(x, mods_lat, g2, router_w, rb, *moe_w, 1024)

        if not last:
            yc_att = _attn(qc.reshape(nb, nctx, ATT_Q_W), None, None, kc, vc, 128)
            uct = _proj_t(xc, mods_ctx, g1, w_hyt, 256)
            cff, cfb = _hy_filters(nctx, *filt_args)
            chr_, chi = _hyc_spec(cff.reshape(-1, nctx), cfb.reshape(-1, nctx), fc)
            chr_ = chr_.reshape(HY_ORDER, HY_WIDTH, 2 * nctx)
            chi = chi.reshape(HY_ORDER, HY_WIDTH, 2 * nctx)
            yc_hyt = _hyc_main(uct, prm, chr_, chi, fc, fi, nb, nctx)
            xc = _merge(xc, mods_ctx, g1, w_brg, yc_hyt,
                        cof.reshape(1, nb * nctx, GLA_V_W), cob.reshape(1, nb * nctx, GLA_V_W),
                        pgc, gn, yc_att.reshape(1, nb * nctx, ATT_Q_W), wh, wg, wa, wo, 256)
            xc = _moe(xc, mods_ctx, g2, router_w, rb, *moe_w, 256)
    return x
```

```python
import functools
import math

import numpy as np
import jax
import jax.numpy as jnp
from jax import lax
from jax.experimental import pallas as pl
from jax.experimental.pallas import tpu as pltpu

F32 = jnp.float32
BF16 = jnp.bfloat16
HIGHEST = lax.Precision.HIGHEST

D_MODEL = 1024
DEPTH = 2
GRID_W = 64
EPS = 1e-6

ATT_HEADS = 8
ATT_KV_HEADS = 2
ATT_GROUP = ATT_HEADS // ATT_KV_HEADS
HEAD_DIM = 128
ROPE_PAIRS_PER_AXIS = HEAD_DIM // 4
ROPE_THETA = 10000.0
Q_SCALE = HEAD_DIM ** -0.5 * math.log2(math.e)

GLA_HEADS = 4
GLA_DK = 64
GLA_DV = 128
GLA_RANK = 16
GLA_TAU = 16.0
GLA_CHUNK = 64

HY_WIDTH = 512
HY_ORDER = 2
HY_BANDS = 16
HY_EMB = 1 + 2 * HY_BANDS
HY_EMB_PAD = 40
HY_FFN = 64

N_EXPERTS = 16
N_GROUPS = 4
EXPERTS_PER_GROUP = N_EXPERTS // N_GROUPS
D_EXPERT = 512

ATT_Q_W = ATT_HEADS * HEAD_DIM
ATT_KV_W = ATT_KV_HEADS * HEAD_DIM
GLA_K_W = GLA_HEADS * GLA_DK
GLA_V_W = GLA_HEADS * GLA_DV
GLA_A_PAD = 128
GLA_COLS = 2 * GLA_K_W + 2 * GLA_V_W + GLA_A_PAD
ATT_COLS = ATT_Q_W + 2 * ATT_KV_W

LANES = 128
VMEM_LIMIT_BYTES = 56 * 1024 * 1024

DFT_GROUP = 8

NT_DIMS = (((1,), (1,)), ((), ()))
TN_DIMS = (((0,), (0,)), ((), ()))


def _params(*sem):
    return pltpu.CompilerParams(dimension_semantics=sem, vmem_limit_bytes=VMEM_LIMIT_BYTES)


def _sigmoid(x):
    return 1.0 / (1.0 + jnp.exp(-x))


def _norm_mod(x, g, shift, scale):
    ms = jnp.mean(x * x, axis=-1, keepdims=True)
    return (x * lax.rsqrt(ms + EPS) * g) * (1.0 + scale) + shift


def _mod_body(c_ref, w_ref, b_ref, o_ref):
    c = c_ref[...]
    s = c * _sigmoid(c)
    o_ref[0] = jnp.dot(s, w_ref[0], precision=HIGHEST, preferred_element_type=F32) + b_ref[0]


def _mods(c_all, w_mod, b_mod):
    tn = 512
    rows = c_all.shape[0]
    return pl.pallas_call(
        _mod_body,
        grid=(DEPTH, 6 * D_MODEL // tn),
        in_specs=[
            pl.BlockSpec((rows, D_MODEL), lambda l, j: (0, 0)),
            pl.BlockSpec((1, D_MODEL, tn), lambda l, j: (l, 0, j)),
            pl.BlockSpec((1, 1, tn), lambda l, j: (l, 0, j)),
        ],
        out_specs=pl.BlockSpec((1, rows, tn), lambda l, j: (l, 0, j)),
        out_shape=jax.ShapeDtypeStruct((DEPTH, rows, 6 * D_MODEL), F32),
        compiler_params=_params("parallel", "parallel"),
        name="mods",
    )(c_all, w_mod, b_mod.reshape(DEPTH, 1, 6 * D_MODEL))


def _proj_att_body(*refs, rope):
    if rope:
        x_ref, m_ref, g_ref, w_ref, qg_ref, kg_ref, cos_ref, sin_ref, q_ref, k_ref, v_ref = refs
    else:
        x_ref, m_ref, g_ref, w_ref, qg_ref, kg_ref, q_ref, k_ref, v_ref = refs
    m = m_ref[0]
    h = _norm_mod(x_ref[0], g_ref[...], m[0:1], m[1:2]).astype(BF16)
    p = jnp.dot(h, w_ref[...], preferred_element_type=F32)

    def head(t, gain):
        t = t * lax.rsqrt(jnp.mean(t * t, axis=-1, keepdims=True) + EPS) * gain
        if rope:
            t = t * cos_ref[...] + pltpu.roll(t, HEAD_DIM // 2, 1) * sin_ref[...]
        return t

    for i in range(ATT_HEADS):
        sl = slice(i * HEAD_DIM, (i + 1) * HEAD_DIM)
        q_ref[0, :, sl] = (head(p[:, sl], qg_ref[...]) * Q_SCALE).astype(BF16)
    for i in range(ATT_KV_HEADS):
        sl = slice(i * HEAD_DIM, (i + 1) * HEAD_DIM)
        src = slice(ATT_Q_W + i * HEAD_DIM, ATT_Q_W + (i + 1) * HEAD_DIM)
        k_ref[0, :, sl] = head(p[:, src], kg_ref[...]).astype(BF16)
    v_ref[0] = p[:, ATT_Q_W + ATT_KV_W:].astype(BF16)


def _proj_att(x, mods, g, w, qg, kg, rope_tabs, tm):
    bm, r, _ = x.shape
    rope = rope_tabs is not None
    in_specs = [
        pl.BlockSpec((1, tm, D_MODEL), lambda b, i: (b, i, 0)),
        pl.BlockSpec((1, 6, D_MODEL), lambda b, i: (b, 0, 0)),
        pl.BlockSpec((1, D_MODEL), lambda b, i: (0, 0)),
        pl.BlockSpec((D_MODEL, ATT_COLS), lambda b, i: (0, 0)),
        pl.BlockSpec((1, HEAD_DIM), lambda b, i: (0, 0)),
        pl.BlockSpec((1, HEAD_DIM), lambda b, i: (0, 0)),
    ]
    args = [x, mods, g, w, qg, kg]
    if rope:
        in_specs += [pl.BlockSpec((tm, HEAD_DIM), lambda b, i: (i, 0))] * 2
        args += list(rope_tabs)
    return pl.pallas_call(
        functools.partial(_proj_att_body, rope=rope),
        grid=(bm, r // tm),
        in_specs=in_specs,
        out_specs=[
            pl.BlockSpec((1, tm, ATT_Q_W), lambda b, i: (b, i, 0)),
            pl.BlockSpec((1, tm, ATT_KV_W), lambda b, i: (b, i, 0)),
            pl.BlockSpec((1, tm, ATT_KV_W), lambda b, i: (b, i, 0)),
        ],
        out_shape=[
            jax.ShapeDtypeStruct((bm, r, ATT_Q_W), BF16),
            jax.ShapeDtypeStruct((bm, r, ATT_KV_W), BF16),
            jax.ShapeDtypeStruct((bm, r, ATT_KV_W), BF16),
        ],
        compiler_params=_params("parallel", "parallel"),
        name="proj_att",
    )(*args)


def _proj_plain_body(x_ref, m_ref, g_ref, w_ref, o_ref):
    m = m_ref[0]
    h = _norm_mod(x_ref[0], g_ref[...], m[0:1], m[1:2]).astype(BF16)
    o_ref[0] = jnp.dot(h, w_ref[...], preferred_element_type=F32)


def _proj_plain(x, mods, g, w, tm):
    bm, r, _ = x.shape
    n = w.shape[1]
    return pl.pallas_call(
        _proj_plain_body,
        grid=(bm, r // tm),
        in_specs=[
            pl.BlockSpec((1, tm, D_MODEL), lambda b, i: (b, i, 0)),
            pl.BlockSpec((1, 6, D_MODEL), lambda b, i: (b, 0, 0)),
            pl.BlockSpec((1, D_MODEL), lambda b, i: (0, 0)),
            pl.BlockSpec((D_MODEL, n), lambda b, i: (0, 0)),
        ],
        out_specs=pl.BlockSpec((1, tm, n), lambda b, i: (b, i, 0)),
        out_shape=jax.ShapeDtypeStruct((bm, r, n), F32),
        compiler_params=_params("parallel", "parallel"),
        name="proj_gla",
    )(x, mods, g, w)


def _proj_t_body(x_ref, m_ref, g_ref, wt_ref, o_ref):
    m = m_ref[0]
    h = _norm_mod(x_ref[0], g_ref[...], m[0:1], m[1:2]).astype(BF16)
    o_ref[0] = lax.dot_general(wt_ref[...], h, NT_DIMS, preferred_element_type=F32)


def _proj_t(x, mods, g, wt, tm):
    bm, r, _ = x.shape
    n = wt.shape[0]
    return pl.pallas_call(
        _proj_t_body,
        grid=(bm, r // tm),
        in_specs=[
            pl.BlockSpec((1, tm, D_MODEL), lambda b, i: (b, i, 0)),
            pl.BlockSpec((1, 6, D_MODEL), lambda b, i: (b, 0, 0)),
            pl.BlockSpec((1, D_MODEL), lambda b, i: (0, 0)),
            pl.BlockSpec((n, D_MODEL), lambda b, i: (0, 0)),
        ],
        out_specs=pl.BlockSpec((1, n, tm), lambda b, i: (b, 0, i)),
        out_shape=jax.ShapeDtypeStruct((bm, n, r), F32),
        compiler_params=_params("parallel", "parallel"),
        name="proj_hy",
    )(x, mods, g, wt)


def _attn_body(*refs, has_lat, tk):
    if has_lat:
        q_ref, kl_ref, vl_ref, kc_ref, vc_ref, o_ref = refs
    else:
        q_ref, kc_ref, vc_ref, o_ref = refs
    q = q_ref[0]
    tq = q.shape[0]
    q4 = jnp.concatenate([q[:, g * HEAD_DIM:(g + 1) * HEAD_DIM] for g in range(ATT_GROUP)], axis=0)

    def step(k_c, v_c, state):
        s = lax.dot_general(q4, k_c, NT_DIMS, preferred_element_type=F32)
        mc = jnp.max(s, axis=-1, keepdims=True)
        if state is None:
            p = jnp.exp2(s - mc)
            return mc, jnp.sum(p, axis=-1, keepdims=True), jnp.dot(p.astype(BF16), v_c,
                                                                    preferred_element_type=F32)
        m, den, acc = state
        m_new = jnp.maximum(m, mc)
        alpha = jnp.exp2(m - m_new)
        p = jnp.exp2(s - m_new)
        den = alpha * den + jnp.sum(p, axis=-1, keepdims=True)
        acc = alpha * acc + jnp.dot(p.astype(BF16), v_c, preferred_element_type=F32)
        return m_new, den, acc

    state = step(kc_ref[0], vc_ref[0], None)
    if has_lat:
        for j in range(kl_ref.shape[1] // tk):
            state = step(kl_ref[0, j * tk:(j + 1) * tk, :], vl_ref[0, j * tk:(j + 1) * tk, :], state)
    _, den, acc = state
    o = acc / den
    o_ref[0] = jnp.concatenate([o[g * tq:(g + 1) * tq] for g in range(ATT_GROUP)], axis=1)


def _attn(q, k_lat, v_lat, k_ctx, v_ctx, tq, tk=2048):
    b, sq, _ = q.shape
    has_lat = k_lat is not None
    gw = ATT_GROUP * HEAD_DIM
    in_specs = [pl.BlockSpec((1, tq, gw), lambda bb, h, i: (bb, i, h))]
    args = [q]
    if has_lat:
        sk = k_lat.shape[1]
        in_specs += [pl.BlockSpec((1, sk, HEAD_DIM), lambda bb, h, i: (bb, 0, h))] * 2
        args += [k_lat, v_lat]
    sc = k_ctx.shape[1]
    in_specs += [pl.BlockSpec((1, sc, HEAD_DIM), lambda bb, h, i: (bb, 0, h))] * 2
    args += [k_ctx, v_ctx]
    return pl.pallas_call(
        functools.partial(_attn_body, has_lat=has_lat, tk=tk),
        grid=(b, ATT_KV_HEADS, sq // tq),
        in_specs=in_specs,
        out_specs=pl.BlockSpec((1, tq, gw), lambda bb, h, i: (bb, i, h)),
        out_shape=jax.ShapeDtypeStruct((b, sq, ATT_Q_W), F32),
        compiler_params=_params("parallel", "parallel", "parallel"),
        name="attn",
    )(*args)


def _gla_dir(q, k, v, a, wa2, ba, tri, st_ref, reverse):
    tt = q.shape[0]
    nc = tt // GLA_CHUNK
    z = jnp.dot(a, wa2, precision=HIGHEST, preferred_element_type=F32) + ba
    la = (jnp.minimum(z, 0.0) - jnp.log(1.0 + jnp.exp(-jnp.abs(z)))) * (1.0 / GLA_TAU)
    cum = jnp.dot(tri, la, precision=HIGHEST, preferred_element_type=F32)
    qd = (q * (GLA_DK ** -0.5) * jnp.exp(cum)).astype(BF16)
    ki = (k * jnp.exp(-cum)).astype(BF16)
    vb = v.astype(BF16)
    row = lax.broadcasted_iota(jnp.int32, (GLA_CHUNK, GLA_CHUNK), 0)
    col = lax.broadcasted_iota(jnp.int32, (GLA_CHUNK, GLA_CHUNK), 1)
    mask = (col >= row) if reverse else (col <= row)
    outs = [None] * nc
    for c in (range(nc - 1, -1, -1) if reverse else range(nc)):
        r0 = c * GLA_CHUNK
        rs = slice(r0, r0 + GLA_CHUNK)
        end = r0 if reverse else r0 + GLA_CHUNK - 1
        cl = cum[end:end + 1, :]
        kte = (k[rs] * jnp.exp(cl - cum[rs])).astype(BF16)
        st = st_ref[...]
        stb = st.astype(BF16)
        o_heads, upd = [], []
        for h in range(GLA_HEADS):
            cs = slice(h * GLA_DK, (h + 1) * GLA_DK)
            vs = slice(h * GLA_DV, (h + 1) * GLA_DV)
            qh, kh, vh = qd[rs, cs], ki[rs, cs], vb[rs, vs]
            att = lax.dot_general(qh, kh, NT_DIMS, preferred_element_type=F32)
            att = jnp.where(mask, att, 0.0).astype(BF16)
            o_heads.append(jnp.dot(att, vh, preferred_element_type=F32)
                           + lax.dot_general(qh, stb[:, cs], NT_DIMS, preferred_element_type=F32))
            upd.append(lax.dot_general(vh, kte[:, cs], TN_DIMS, preferred_element_type=F32))
        st_ref[...] = st * jnp.exp(cl) + jnp.concatenate(upd, axis=1)
        outs[c] = jnp.concatenate(o_heads, axis=1)
    return jnp.concatenate(outs, axis=0)


def _gla_body(qf, kf, vf, af, qb, kb, vb, ab, wa2, ba, tri, s0f, s0b,
              of, ob, sf_out, sb_out, sf_scr, sb_scr):
    @pl.when(pl.program_id(1) == 0)
    def _():
        sf_scr[...] = s0f[0]
        sb_scr[...] = s0b[0]

    a_f = af[0]
    a_b = ab[0]
    of[0] = _gla_dir(qf[0], kf[0], vf[0], a_f[:, :GLA_RANK], wa2[0], ba[0], tri[0], sf_scr, False)
    ob[0] = _gla_dir(qb[0], kb[0], vb[0], a_b[:, GLA_RANK:2 * GLA_RANK], wa2[1], ba[1], tri[1],
                     sb_scr, True)
    sf_out[0] = sf_scr[...]
    sb_out[0] = sb_scr[...]


def _gla(p, wa2, ba, tri, s0f, s0b, tt):
    b, s, _ = p.shape
    n = s // tt
    a_blk = (2 * GLA_K_W + 2 * GLA_V_W) // GLA_A_PAD

    def specs(rev):
        t = (lambda j: n - 1 - j) if rev else (lambda j: j)
        return [
            pl.BlockSpec((1, tt, GLA_K_W), lambda bb, j: (bb, t(j), 0)),
            pl.BlockSpec((1, tt, GLA_K_W), lambda bb, j: (bb, t(j), 1)),
            pl.BlockSpec((1, tt, GLA_V_W), lambda bb, j: (bb, t(j), 1)),
            pl.BlockSpec((1, tt, GLA_A_PAD), lambda bb, j: (bb, t(j), a_blk)),
        ]

    st_spec = pl.BlockSpec((1, GLA_DV, GLA_K_W), lambda bb, j: (bb, 0, 0))
    st_shape = jax.ShapeDtypeStruct((b, GLA_DV, GLA_K_W), F32)
    return pl.pallas_call(
        _gla_body,
        grid=(b, n),
        in_specs=specs(False) + specs(True) + [
            pl.BlockSpec((2, GLA_RANK, GLA_K_W), lambda bb, j: (0, 0, 0)),
            pl.BlockSpec((2, 1, GLA_K_W), lambda bb, j: (0, 0, 0)),
            pl.BlockSpec((2, tt, tt), lambda bb, j: (0, 0, 0)),
            st_spec, st_spec,
        ],
        out_specs=[
            pl.BlockSpec((1, tt, GLA_V_W), lambda bb, j: (bb, j, 0)),
            pl.BlockSpec((1, tt, GLA_V_W), lambda bb, j: (bb, n - 1 - j, 0)),
            st_spec, st_spec,
        ],
        out_shape=[
            jax.ShapeDtypeStruct((b, s, GLA_V_W), F32),
            jax.ShapeDtypeStruct((b, s, GLA_V_W), F32),
            st_shape, st_shape,
        ],
        scratch_shapes=[pltpu.VMEM((GLA_DV, GLA_K_W), F32), pltpu.VMEM((GLA_DV, GLA_K_W), F32)],
        compiler_params=_params("parallel", "arbitrary"),
        name="gla",
    )(p, p, p, p, p, p, p, p, wa2, ba, tri, s0f, s0b)


def _gla_tri(tt):
    t = np.arange(tt)
    same = (t[:, None] // GLA_CHUNK) == (t[None, :] // GLA_CHUNK)
    fwd = same & (t[None, :] <= t[:, None])
    bwd = same & (t[None, :] >= t[:, None])
    return jnp.asarray(np.stack([fwd, bwd]).astype(np.float32))


def _hy_filter_body(zt_ref, w1t_ref, b1_ref, fr_ref, w2t_ref, b2_ref, w3t_ref, dec_ref,
                    f_ref, b_ref):
    zt = zt_ref[...]
    fr = fr_ref[...]
    h1 = jnp.sin(fr * (jnp.dot(w1t_ref[...], zt, precision=HIGHEST, preferred_element_type=F32)
                       + b1_ref[...]))
    h2 = jnp.sin(fr * (jnp.dot(w2t_ref[...], h1, precision=HIGHEST, preferred_element_type=F32)
                       + b2_ref[...]))
    tn = zt[0:1, :]
    f = jnp.dot(w3t_ref[0, 0], h2, precision=HIGHEST, preferred_element_type=F32)
    f = f * jnp.exp(-tn * jnp.abs(dec_ref[0, 0]))
    b = jnp.dot(w3t_ref[0, 1], h2, precision=HIGHEST, preferred_element_type=F32)
    b = b * jnp.exp(-tn * jnp.abs(dec_ref[0, 1]))
    lane = lax.broadcasted_iota(jnp.int32, b.shape, 1)
    b = jnp.where(lane == 0, 0.0, b)
    den = (jnp.sum(jnp.abs(f), axis=-1, keepdims=True)
           + jnp.sum(jnp.abs(b), axis=-1, keepdims=True) + EPS)
    f_ref[0] = f / den
    b_ref[0] = b / den


def _hy_filters(n, w1, b1, fr, w2, b2, w3, dec):
    t = jnp.arange(n, dtype=F32)
    t_norm = t / n
    bands = jnp.linspace(1e-4, HY_BANDS - 1, HY_BANDS, dtype=F32)
    phase = (2 * math.pi / n) * t[:, None] * bands[None, :]
    z = jnp.concatenate([t_norm[:, None], jnp.cos(phase), -jnp.sin(phase)], axis=-1)
    zt = jnp.pad(z.T, ((0, HY_EMB_PAD - HY_EMB), (0, 0)))
    w1t = jnp.pad(w1.T, ((0, 0), (0, HY_EMB_PAD - HY_EMB)))
    w3t = w3.T.reshape(HY_ORDER, 2, HY_WIDTH, HY_FFN)
    ct = 128
    col = lambda v: v.reshape(HY_FFN, 1)
    full = lambda shape: pl.BlockSpec(shape, lambda o, c: (0,) * len(shape))
    out_spec = pl.BlockSpec((1, ct, n), lambda o, c: (o, c, 0))
    out_shape = jax.ShapeDtypeStruct((HY_ORDER, HY_WIDTH, n), F32)
    return pl.pallas_call(
        _hy_filter_body,
        grid=(HY_ORDER, HY_WIDTH // ct),
        in_specs=[
            full((HY_EMB_PAD, n)), full((HY_FFN, HY_EMB_PAD)), full((HY_FFN, 1)), full((HY_FFN, 1)),
            full((HY_FFN, HY_FFN)), full((HY_FFN, 1)),
            pl.BlockSpec((1, 2, ct, HY_FFN), lambda o, c: (o, 0, c, 0)),
            pl.BlockSpec((1, 2, ct, 1), lambda o, c: (o, 0, c, 0)),
        ],
        out_specs=[out_spec, out_spec],
        out_shape=[out_shape, out_shape],
        compiler_params=_params("parallel", "parallel"),
        name="hy_filter",
    )(zt, w1t, col(b1), col(fr), w2.T, col(b2), w3t, dec.reshape(HY_ORDER, 2, HY_WIDTH, 1))


def _dft_consts(n1):
    n = n1 * LANES
    half = n1 // 2
    g = DFT_GROUP
    eye = np.eye(g)
    k1 = np.arange(n1)[:, None]
    i1 = np.arange(half)[None, :]
    a1 = 2 * np.pi * k1 * i1 / n1
    g1k = np.concatenate([np.kron(eye, np.cos(a1)), np.kron(eye, -np.sin(a1))], axis=0)
    i2 = np.arange(LANES)[None, :]
    at = 2 * np.pi * k1 * i2 / n
    twr = np.tile(np.cos(at), (g, 1))
    twi = np.tile(-np.sin(at), (g, 1))
    a2 = 2 * np.pi * np.arange(LANES)[:, None] * np.arange(LANES)[None, :] / LANES
    cplx = lambda fr, fi: np.block([[fr, fi], [-fi, fr]])
    f2 = cplx(np.cos(a2), -np.sin(a2))
    f2i = cplx(np.cos(a2), np.sin(a2))
    ai = a1.T
    gik = np.concatenate([np.kron(eye, np.cos(ai)), np.kron(eye, -np.sin(ai))], axis=1)
    bf = lambda m: jnp.asarray(m, dtype=BF16)
    return dict(g1k=bf(g1k), twr=jnp.asarray(twr, F32), twi=jnp.asarray(twi, F32),
                f2=bf(f2), f2i=bf(f2i), gik=bf(gik))


def _fwd_dft(x, g1k, twr, twi, f2):
    a = jnp.dot(g1k, x.astype(BF16), preferred_element_type=F32)
    h = a.shape[0] // 2
    ar, ai = a[:h], a[h:]
    tr = ar * twr - ai * twi
    ti = ar * twi + ai * twr
    p = jnp.dot(jnp.concatenate([tr, ti], axis=1).astype(BF16), f2, preferred_element_type=F32)
    return p[:, :LANES], p[:, LANES:]


def _inv_dft(yr, yi, twr, twi, f2i, gik):
    p = jnp.dot(jnp.concatenate([yr, yi], axis=1).astype(BF16), f2i, preferred_element_type=F32)
    br, bi = p[:, :LANES], p[:, LANES:]
    tr = br * twr + bi * twi
    ti = bi * twr - br * twi
    return jnp.dot(gik, jnp.concatenate([tr, ti], axis=0).astype(BF16), preferred_element_type=F32)


def _hy_spec_body(f_ref, b_ref, g1k, twr, twi, f2, hr_ref, hi_ref, *, groups, n1):
    half = n1 // 2
    scale = 1.0 / (n1 * LANES)

    def body(g, carry):
        rows = pl.ds(pl.multiple_of(g * (DFT_GROUP * half), DFT_GROUP * half), DFT_GROUP * half)
        zfr, zfi = _fwd_dft(f_ref[rows, :], g1k[...], twr[...], twi[...], f2[...])
        zbr, zbi = _fwd_dft(b_ref[rows, :], g1k[...], twr[...], twi[...], f2[...])
        orow = pl.ds(pl.multiple_of(g * (DFT_GROUP * n1), DFT_GROUP * n1), DFT_GROUP * n1)
        hr_ref[orow, :] = (zfr + zbr) * scale
        hi_ref[orow, :] = (zfi - zbi) * scale
        return carry

    lax.fori_loop(0, groups, body, 0)


def _const_spec(arr, ngrid):
    nd = arr.ndim
    return pl.BlockSpec(arr.shape, lambda *_: (0,) * nd)


def _hy_spec(f, b, consts, n1):
    half = n1 // 2
    chans = f.shape[0] // half
    ct = 64
    groups = ct // DFT_GROUP
    cs = [consts[k] for k in ("g1k", "twr", "twi", "f2")]
    in_spec = pl.BlockSpec((ct * half, LANES), lambda c: (c, 0))
    out_spec = pl.BlockSpec((ct * n1, LANES), lambda c: (c, 0))
    out_shape = jax.ShapeDtypeStruct((chans * n1, LANES), F32)
    return pl.pallas_call(
        functools.partial(_hy_spec_body, groups=groups, n1=n1),
        grid=(chans // ct,),
        in_specs=[in_spec, in_spec] + [_const_spec(c, 1) for c in cs],
        out_specs=[out_spec, out_spec],
        out_shape=[out_shape, out_shape],
        compiler_params=_params("parallel"),
        name="hy_spec",
    )(f, b, *cs)


def _hy_main_body(u1_ref, u2_ref, uz_ref, p_ref, hr_ref, hi_ref, g1k, twr, twi, f2, f2i, gik,
                  o_ref, *, groups, n1):
    half = n1 // 2
    rows_g = DFT_GROUP * half
    shape = (rows_g, LANES)
    rowi = lax.broadcasted_iota(jnp.int32, shape, 0) % half
    lane = lax.broadcasted_iota(jnp.int32, shape, 1)
    lane_first = lane == 0
    lane_last = lane == LANES - 1
    t_first = (rowi == 0) & lane_first
    t_last = (rowi == half - 1) & lane_last

    def prev(x):
        r = pltpu.roll(x, 1, 1)
        return jnp.where(t_first, 0.0, jnp.where(lane_first, pltpu.roll(r, 1, 0), r))

    def nxt(x):
        l = pltpu.roll(x, LANES - 1, 1)
        return jnp.where(t_last, 0.0, jnp.where(lane_last, pltpu.roll(l, rows_g - 1, 0), l))

    def body(g, carry):
        cs = pl.ds(pl.multiple_of(g * DFT_GROUP, DFT_GROUP), DFT_GROUP)
        pr = p_ref[pl.ds(pl.multiple_of(g * rows_g, rows_g), rows_g), :]
        col = lambda j: pr[:, j:j + 1]

        def sconv(ref, j):
            x = ref[0, cs].reshape(shape)
            return prev(x) * col(j) + x * col(j + 1) + nxt(x) * col(j + 2) + col(j + 3)

        x1 = sconv(u1_ref, 0)
        x2 = sconv(u2_ref, 4)
        z = sconv(uz_ref, 8)
        hrow = pl.ds(pl.multiple_of(g * (DFT_GROUP * n1), DFT_GROUP * n1), DFT_GROUP * n1)

        def conv(v, o):
            zr, zi = _fwd_dft(v, g1k[...], twr[...], twi[...], f2[...])
            hr = hr_ref[o, hrow, :]
            hi = hi_ref[o, hrow, :]
            return _inv_dft(zr * hr - zi * hi, zr * hi + zi * hr, twr[...], twi[...], f2i[...], gik[...])

        z = x1 * (conv(z, 0) + col(12) * z)
        z = x2 * (conv(z, 1) + col(13) * z)
        o_ref[0, cs] = z.reshape(DFT_GROUP, half, LANES)
        return carry

    lax.fori_loop(0, groups, body, 0, unroll=2)


def _hy_main(ut, prm, hr, hi, consts, n1):
    b = ut.shape[0]
    half = n1 // 2
    ct = 64
    nct = HY_WIDTH // ct
    cs = [consts[k] for k in ("g1k", "twr", "twi", "f2", "f2i", "gik")]
    u_spec = lambda sec: pl.BlockSpec((1, ct, half, LANES), lambda c, bb: (bb, sec * nct + c, 0, 0))
    h_spec = pl.BlockSpec((HY_ORDER, ct * n1, LANES), lambda c, bb: (0, c, 0))
    return pl.pallas_call(
        functools.partial(_hy_main_body, groups=ct // DFT_GROUP, n1=n1),
        grid=(nct, b),
        in_specs=[u_spec(0), u_spec(1), u_spec(2),
                  pl.BlockSpec((ct * half, 16), lambda c, bb: (c, 0)),
                  h_spec, h_spec] + [_const_spec(c, 2) for c in cs],
        out_specs=pl.BlockSpec((1, ct, half, LANES), lambda c, bb: (bb, c, 0, 0)),
        out_shape=jax.ShapeDtypeStruct((b, HY_WIDTH, half, LANES), F32),
        compiler_params=_params("parallel", "parallel"),
        name="hy_main",
    )(ut, ut, ut, prm, hr, hi, *cs)


def _hyc_consts(n):
    nn = 2 * n
    a = 2 * np.pi * np.arange(n)[:, None] * np.arange(nn)[None, :] / nn
    fc = np.concatenate([np.cos(a), -np.sin(a)], axis=1)
    fi = np.concatenate([np.cos(a.T), -np.sin(a.T)], axis=0)
    return jnp.asarray(fc, BF16), jnp.asarray(fi, BF16)


def _hyc_spec_body(f_ref, b_ref, fc_ref, hr_ref, hi_ref):
    nn = hr_ref.shape[-1]
    zf = jnp.dot(f_ref[...].astype(BF16), fc_ref[...], preferred_element_type=F32)
    zb = jnp.dot(b_ref[...].astype(BF16), fc_ref[...], preferred_element_type=F32)
    hr_ref[...] = (zf[:, :nn] + zb[:, :nn]) * (1.0 / nn)
    hi_ref[...] = (zf[:, nn:] - zb[:, nn:]) * (1.0 / nn)


def _hyc_spec(f, b, fc):
    rows, n = f.shape
    out_shape = jax.ShapeDtypeStruct((rows, 2 * n), F32)
    return pl.pallas_call(
        _hyc_spec_body,
        out_shape=[out_shape, out_shape],
        compiler_params=pltpu.CompilerParams(vmem_limit_bytes=VMEM_LIMIT_BYTES),
        name="hyc_spec",
    )(f, b, fc)


def _hyc_main_body(u1_ref, u2_ref, uz_ref, p_ref, hr_ref, hi_ref, fc_ref, fi_ref, o_ref):
    shape = u1_ref.shape[1:]
    n = shape[1]
    nn = 2 * n
    lane = lax.broadcasted_iota(jnp.int32, shape, 1)
    pr = p_ref[...]
    col = lambda j: pr[:, j:j + 1]

    def sconv(ref, j):
        x = ref[0]
        p = jnp.where(lane == 0, 0.0, pltpu.roll(x, 1, 1))
        q = jnp.where(lane == n - 1, 0.0, pltpu.roll(x, n - 1, 1))
        return p * col(j) + x * col(j + 1) + q * col(j + 2) + col(j + 3)

    def conv(v, o):
        s = jnp.dot(v.astype(BF16), fc_ref[...], preferred_element_type=F32)
        zr, zi = s[:, :nn], s[:, nn:]
        hr, hi = hr_ref[o], hi_ref[o]
        y = jnp.concatenate([zr * hr - zi * hi, zr * hi + zi * hr], axis=1).astype(BF16)
        return jnp.dot(y, fi_ref[...], preferred_element_type=F32)

    x1 = sconv(u1_ref, 0)
    x2 = sconv(u2_ref, 4)
    z = sconv(uz_ref, 8)
    z = x1 * (conv(z, 0) + col(12) * z)
    o_ref[0] = x2 * (conv(z, 1) + col(13) * z)


def _hyc_main(ut, prm, hr, hi, fc, fi, nb, n):
    ct = 128
    nct = HY_WIDTH // ct
    u_spec = lambda sec: pl.BlockSpec((1, ct, n), lambda bb, c: (0, sec * nct + c, bb))
    h_spec = pl.BlockSpec((HY_ORDER, ct, 2 * n), lambda bb, c: (0, c, 0))
    return pl.pallas_call(
        _hyc_main_body,
        grid=(nb, nct),
        in_specs=[u_spec(0), u_spec(1), u_spec(2),
                  pl.BlockSpec((ct, 16), lambda bb, c: (c, 0)),
                  h_spec, h_spec, _const_spec(fc, 2), _const_spec(fi, 2)],
        out_specs=pl.BlockSpec((1, ct, n), lambda bb, c: (0, c, bb)),
        out_shape=jax.ShapeDtypeStruct((1, HY_WIDTH, nb * n), F32),
        compiler_params=_params("parallel", "parallel"),
        name="hyc_main",
    )(ut, ut, ut, prm, hr, hi, fc, fi)


def _merge_body(x_ref, m_ref, g_ref, wbrg, yh_ref, of_ref, ob_ref, og_ref, gn_ref, ya_ref,
                wh, wg, wa, wo, o_ref):
    x = x_ref[0]
    m = m_ref[0]
    h = _norm_mod(x, g_ref[...], m[0:1], m[1:2]).astype(BF16)
    gates = jnp.dot(h, wbrg[...], preferred_element_type=F32)
    yh = yh_ref[0].T
    o = of_ref[0] + ob_ref[0]
    parts = []
    for hh in range(GLA_HEADS):
        t = o[:, hh * GLA_DV:(hh + 1) * GLA_DV]
        parts.append(t * lax.rsqrt(jnp.mean(t * t, axis=-1, keepdims=True) + EPS) * gn_ref[...])
    og = og_ref[0]
    yg = jnp.concatenate(parts, axis=1) * (og * _sigmoid(og))
    d = D_MODEL
    mm = (_sigmoid(gates[:, :d]) * jnp.dot(yh.astype(BF16), wh[...], preferred_element_type=F32)
          + _sigmoid(gates[:, d:2 * d]) * jnp.dot(yg.astype(BF16), wg[...], preferred_element_type=F32)
          + _sigmoid(gates[:, 2 * d:]) * jnp.dot(ya_ref[0].astype(BF16), wa[...],
                                                  preferred_element_type=F32))
    out = jnp.dot(mm.astype(BF16), wo[...], preferred_element_type=F32)
    o_ref[0] = x + m[2:3] * out


def _merge(x, mods, g, wbrg, yht, of, ob, pg, gn, ya, wh, wg, wa, wo, tm):
    bm, r, _ = x.shape
    row = lambda w: pl.BlockSpec((1, tm, w), lambda b, i: (b, i, 0))
    full = lambda a: pl.BlockSpec(a.shape, lambda b, i: (0,) * a.ndim)
    og_blk = (2 * GLA_K_W + GLA_V_W) // GLA_V_W
    return pl.pallas_call(
        _merge_body,
        grid=(bm, r // tm),
        in_specs=[
            row(D_MODEL),
            pl.BlockSpec((1, 6, D_MODEL), lambda b, i: (b, 0, 0)),
            full(g), full(wbrg),
            pl.BlockSpec((1, HY_WIDTH, tm), lambda b, i: (b, 0, i)),
            row(GLA_V_W), row(GLA_V_W),
            pl.BlockSpec((1, tm, GLA_V_W), lambda b, i: (b, i, og_blk)),
            full(gn), row(ATT_Q_W), full(wh), full(wg), full(wa), full(wo),
        ],
        out_specs=row(D_MODEL),
        out_shape=jax.ShapeDtypeStruct(x.shape, F32),
        compiler_params=_params("parallel", "parallel"),
        name="merge",
    )(x, mods, g, wbrg, yht, of, ob, pg, gn, ya, wh, wg, wa, wo)


def _route(logits, rb):
    scores = _sigmoid(logits)
    sel = scores + rb
    lane = lax.broadcasted_iota(jnp.int32, sel.shape, 1).astype(F32)
    neg = -jnp.inf

    def top2(v):
        m1 = jnp.max(v, axis=-1, keepdims=True)
        i1 = jnp.min(jnp.where(v == m1, lane, float(N_EXPERTS)), axis=-1, keepdims=True)
        v2 = jnp.where(lane == i1, neg, v)
        m2 = jnp.max(v2, axis=-1, keepdims=True)
        i2 = jnp.min(jnp.where(v2 == m2, lane, float(N_EXPERTS)), axis=-1, keepdims=True)
        return m1, m2, i1, i2

    group_of = jnp.floor(lane * (1.0 / EXPERTS_PER_GROUP))
    best = None
    best_g = None
    for g in range(N_GROUPS):
        m1, m2, _, _ = top2(jnp.where(group_of == float(g), sel, neg))
        gs = m1 + m2
        if best is None:
            best, best_g = gs, jnp.zeros_like(gs)
        else:
            better = gs > best
            best_g = jnp.where(better, float(g), best_g)
            best = jnp.where(better, gs, best)
    _, _, i1, i2 = top2(jnp.where(group_of == best_g, sel, neg))
    w = jnp.where((lane == i1) | (lane == i2), scores, 0.0)
    return w / jnp.sum(w, axis=-1, keepdims=True)


def _moe_body(x_ref, m_ref, g_ref, rw_ref, rb_ref, wg_ref, wu_ref, wd_ref, o_ref,
              h_scr, gate_scr, acc_scr):
    e = pl.program_id(2)

    @pl.when(e == 0)
    def _():
        m = m_ref[0]
        h = _norm_mod(x_ref[0], g_ref[...], m[3:4], m[4:5])
        h_scr[...] = h.astype(BF16)
        logits = jnp.dot(h, rw_ref[...], precision=HIGHEST, preferred_element_type=F32)
        gate_scr[...] = _route(logits, rb_ref[...])
        acc_scr[...] = jnp.zeros_like(acc_scr)

    h = h_scr[...]
    a = jnp.dot(h, wg_ref[0], preferred_element_type=F32)
    b = jnp.dot(h, wu_ref[0], preferred_element_type=F32)
    hid = (a * _sigmoid(a) * b).astype(BF16)
    y = jnp.dot(hid, wd_ref[0], preferred_element_type=F32)
    gates = gate_scr[...]
    lane = lax.broadcasted_iota(jnp.int32, gates.shape, 1)
    ge = jnp.sum(jnp.where(lane == e, gates, 0.0), axis=-1, keepdims=True)
    acc_scr[...] += ge * y

    @pl.when(e == N_EXPERTS - 1)
    def _():
        o_ref[0] = x_ref[0] + m_ref[0][5:6] * acc_scr[...]


def _moe(x, mods, g, rw, rb, wg, wu, wd, tm):
    bm, r, _ = x.shape
    row = pl.BlockSpec((1, tm, D_MODEL), lambda b, i, e: (b, i, 0))
    return pl.pallas_call(
        _moe_body,
        grid=(bm, r // tm, N_EXPERTS),
        in_specs=[
            row,
            pl.BlockSpec((1, 6, D_MODEL), lambda b, i, e: (b, 0, 0)),
            pl.BlockSpec((1, D_MODEL), lambda b, i, e: (0, 0)),
            pl.BlockSpec((D_MODEL, N_EXPERTS), lambda b, i, e: (0, 0)),
            pl.BlockSpec((1, N_EXPERTS), lambda b, i, e: (0, 0)),
            pl.BlockSpec((1, D_MODEL, D_EXPERT), lambda b, i, e: (e, 0, 0)),
            pl.BlockSpec((1, D_MODEL, D_EXPERT), lambda b, i, e: (e, 0, 0)),
            pl.BlockSpec((1, D_EXPERT, D_MODEL), lambda b, i, e: (e, 0, 0)),
        ],
        out_specs=row,
        out_shape=jax.ShapeDtypeStruct(x.shape, F32),
        scratch_shapes=[
            pltpu.VMEM((tm, D_MODEL), BF16),
            pltpu.VMEM((tm, N_EXPERTS), F32),
            pltpu.VMEM((tm, D_MODEL), F32),
        ],
        compiler_params=_params("parallel", "parallel", "arbitrary"),
        name="moe",
    )(x, mods, g, rw, rb, wg, wu, wd)


def _rope_tables(n_tokens):
    rows = n_tokens // GRID_W
    row = jnp.broadcast_to(jnp.arange(rows)[:, None], (rows, GRID_W)).reshape(-1).astype(F32)
    col = jnp.broadcast_to(jnp.arange(GRID_W)[None, :], (rows, GRID_W)).reshape(-1).astype(F32)
    inv_freq = ROPE_THETA ** (-jnp.arange(ROPE_PAIRS_PER_AXIS, dtype=F32) / ROPE_PAIRS_PER_AXIS)
    ang = jnp.concatenate([row[:, None] * inv_freq, col[:, None] * inv_freq], axis=-1)
    cos, sin = jnp.cos(ang), jnp.sin(ang)
    return jnp.concatenate([cos, cos], axis=-1), jnp.concatenate([-sin, sin], axis=-1)


def _split_w_in(w):
    sizes = (ATT_KV_W, ATT_KV_W, GLA_K_W, GLA_V_W, 2 * GLA_RANK,
             ATT_Q_W, GLA_K_W, GLA_V_W, 3 * HY_WIDTH, 3 * D_MODEL)
    cuts = [int(v) for v in np.cumsum(sizes)[:-1]]
    a_k, a_v, g_k, g_v, g_a, a_q, g_q, g_og, hy_u, br_g = jnp.split(w, cuts, axis=1)
    perm = np.concatenate([np.arange(0, HEAD_DIM, 2), np.arange(1, HEAD_DIM, 2)])
    perm_q = np.concatenate([h * HEAD_DIM + perm for h in range(ATT_HEADS)])
    perm_k = np.concatenate([h * HEAD_DIM + perm for h in range(ATT_KV_HEADS)])
    w_att = jnp.concatenate([a_q[:, perm_q], a_k[:, perm_k], a_v], axis=1).astype(BF16)
    pad = jnp.zeros((D_MODEL, GLA_A_PAD - 2 * GLA_RANK), w.dtype)
    w_gla = jnp.concatenate([g_q, g_k, g_v, g_og, g_a, pad], axis=1).astype(BF16)
    return w_att, w_gla, hy_u.T.astype(BF16), br_g.astype(BF16), perm


def kernel(x, c, ctx, c_ctx, w_mod, b_mod, norm1_g, norm2_g, w_in, q_norm_g, k_norm_g, gla_wa2, gla_ba, gla_norm_g, hy_conv_w, hy_conv_b, hy_pos_w1, hy_pos_b1, hy_sin_freq, hy_pos_w2, hy_pos_b2, hy_pos_w3, hy_decay, hy_skip, w_br_hy, w_br_gla, w_br_att, w_out, router_w, router_b, moe_w_gate, moe_w_up, moe_w_down):
    nb, seq, d = x.shape
    nctx = ctx.shape[1]
    n1_lat = 2 * seq // LANES

    c_all = jnp.concatenate([c, c_ctx[None, :], jnp.zeros((16 - nb - 1, d), F32)], axis=0)
    mods_all = _mods(c_all, w_mod, b_mod)

    rope_tabs = _rope_tables(seq)
    consts = _dft_consts(n1_lat)
    fc, fi = _hyc_consts(nctx)
    tri_lat = _gla_tri(256)
    zero_state = jnp.zeros((nb, GLA_DV, GLA_K_W), F32)
    rb = router_b.reshape(1, N_EXPERTS)

    xc = ctx.reshape(1, nb * nctx, d)
    for l in range(DEPTH):
        last = l == DEPTH - 1
        mods_lat = mods_all[l, :nb].reshape(nb, 6, d)
        mods_ctx = mods_all[l, nb:nb + 1].reshape(1, 6, d)
        g1 = norm1_g[l].reshape(1, d)
        g2 = norm2_g[l].reshape(1, d)
        w_att, w_gla, w_hyt, w_brg, perm = _split_w_in(w_in[l])
        qg = q_norm_g[l][perm].reshape(1, HEAD_DIM)
        kg = k_norm_g[l][perm].reshape(1, HEAD_DIM)
        wa2 = gla_wa2[l]
        ba = gla_ba[l].reshape(2, 1, GLA_K_W)
        gn = gla_norm_g[l].reshape(1, GLA_DV)
        wh, wg, wa, wo = (w_br_hy[l].astype(BF16), w_br_gla[l].astype(BF16),
                          w_br_att[l].astype(BF16), w_out[l].astype(BF16))
        cw, cb = hy_conv_w[l], hy_conv_b[l]
        sec = lambda s: [cw[0, s], cw[1, s], cw[2, s], cb[s]]
        w_ = HY_WIDTH
        prm = jnp.stack(sec(slice(0, w_)) + sec(slice(w_, 2 * w_)) + sec(slice(2 * w_, 3 * w_))
                        + [hy_skip[l, 0], hy_skip[l, 1], jnp.zeros((w_,), F32), jnp.zeros((w_,), F32)],
                        axis=1)
        filt_args = (hy_pos_w1[l], hy_pos_b1[l], hy_sin_freq[l], hy_pos_w2[l], hy_pos_b2[l],
                     hy_pos_w3[l], hy_decay[l])

        q, k, v = _proj_att(x, mods_lat, g1, w_att, qg, kg, rope_tabs, 512)
        qc, kc, vc = _proj_att(xc, mods_ctx, g1, w_att, qg, kg, None, 256)
        kc = kc.reshape(nb, nctx, ATT_KV_W)
        vc = vc.reshape(nb, nctx, ATT_KV_W)
        pg = _proj_plain(x, mods_lat, g1, w_gla, 512)
        pgc = _proj_plain(xc, mods_ctx, g1, w_gla, 256)
        ut = _proj_t(x, mods_lat, g1, w_hyt, 512)

        y_att = _attn(q, k, v, kc, vc, 256)

        cof, cob, s_f, s_b = _gla(pgc.reshape(nb, nctx, GLA_COLS), wa2, ba, _gla_tri(nctx),
                                  zero_state, zero_state, nctx)
        of, ob, _, _ = _gla(pg, wa2, ba, tri_lat, s_f, s_b, 256)

        ff, fb = _hy_filters(seq, *filt_args)
        hr, hi = _hy_spec(ff.reshape(-1, LANES), fb.reshape(-1, LANES), consts, n1_lat)
        hr = hr.reshape(HY_ORDER, HY_WIDTH * n1_lat, LANES)
        hi = hi.reshape(HY_ORDER, HY_WIDTH * n1_lat, LANES)
        y_hyt = _hy_main(ut.reshape(nb, 3 * HY_WIDTH, n1_lat // 2, LANES),
                         jnp.repeat(prm, n1_lat // 2, axis=0), hr, hi, consts, n1_lat)
        y_hyt = y_hyt.reshape(nb, HY_WIDTH, seq)

        x = _merge(x, mods_lat, g1, w_brg, y_hyt, of, ob, pg, gn, y_att, wh, wg, wa, wo, 256)
        moe_w = (moe_w_gate[l].astype(BF16), moe_w_up[l].astype(BF16), moe_w_down[l].astype(BF16))
        x = _moe(x, mods_lat, g2, router_w, rb, *moe_w, 1024)

        if not last:
            yc_att = _attn(qc.reshape(nb, nctx, ATT_Q_W), None, None, kc, vc, 128)
            uct = _proj_t(xc, mods_ctx, g1, w_hyt, 256)
            cff, cfb = _hy_filters(nctx, *filt_args)
            chr_, chi = _hyc_spec(cff.reshape(-1, nctx), cfb.reshape(-1, nctx), fc)
            chr_ = chr_.reshape(HY_ORDER, HY_WIDTH, 2 * nctx)
            chi = chi.reshape(HY_ORDER, HY_WIDTH, 2 * nctx)
            yc_hyt = _hyc_main(uct, prm, chr_, chi, fc, fi, nb, nctx)
            xc = _merge(xc, mods_ctx, g1, w_brg, yc_hyt,
                        cof.reshape(1, nb * nctx, GLA_V_W), cob.reshape(1, nb * nctx, GLA_V_W),
                        pgc, gn, yc_att.reshape(1, nb * nctx, ATT_Q_W), wh, wg, wa, wo, 256)
            xc = _moe(xc, mods_ctx, g2, router_w, rb, *moe_w, 256)
    return x
```

```python
import functools
import math

import numpy as np
import jax
import jax.numpy as jnp
from jax import lax
from jax.experimental import pallas as pl
from jax.experimental.pallas import tpu as pltpu

F32 = jnp.float32
BF16 = jnp.bfloat16
HIGHEST = lax.Precision.HIGHEST

D_MODEL = 1024
DEPTH = 2
GRID_W = 64
EPS = 1e-6

ATT_HEADS = 8
ATT_KV_HEADS = 2
ATT_GROUP = ATT_HEADS // ATT_KV_HEADS
HEAD_DIM = 128
ROPE_PAIRS_PER_AXIS = HEAD_DIM // 4
ROPE_THETA = 10000.0
Q_SCALE = HEAD_DIM ** -0.5 * math.log2(math.e)

GLA_HEADS = 4
GLA_DK = 64
GLA_DV = 128
GLA_RANK = 16
GLA_TAU = 16.0
GLA_CHUNK = 64

HY_WIDTH = 512
HY_ORDER = 2
HY_BANDS = 16
HY_EMB = 1 + 2 * HY_BANDS
HY_EMB_PAD = 40
HY_FFN = 64

N_EXPERTS = 16
N_GROUPS = 4
EXPERTS_PER_GROUP = N_EXPERTS // N_GROUPS
D_EXPERT = 512

ATT_Q_W = ATT_HEADS * HEAD_DIM
ATT_KV_W = ATT_KV_HEADS * HEAD_DIM
GLA_K_W = GLA_HEADS * GLA_DK
GLA_V_W = GLA_HEADS * GLA_DV
GLA_A_PAD = 128
GLA_COLS = 2 * GLA_K_W + 2 * GLA_V_W + GLA_A_PAD
ATT_COLS = ATT_Q_W + 2 * ATT_KV_W

LANES = 128
MOE_SUB = LANES
VMEM_LIMIT_BYTES = 60 * 1024 * 1024

DFT_GROUP = 8

NT_DIMS = (((1,), (1,)), ((), ()))
TN_DIMS = (((0,), (0,)), ((), ()))


def _params(*sem):
    return pltpu.CompilerParams(dimension_semantics=sem, vmem_limit_bytes=VMEM_LIMIT_BYTES)


def _sigmoid(x):
    return 1.0 / (1.0 + jnp.exp(-x))


def _norm_mod(x, g, shift, scale):
    ms = jnp.mean(x * x, axis=-1, keepdims=True)
    return (x * lax.rsqrt(ms + EPS) * g) * (1.0 + scale) + shift


def _mod_body(c_ref, w_ref, b_ref, o_ref):
    c = c_ref[...]
    s = c * _sigmoid(c)
    o_ref[0] = jnp.dot(s, w_ref[0], precision=HIGHEST, preferred_element_type=F32) + b_ref[0]


def _mods(c_all, w_mod, b_mod):
    tn = 512
    rows = c_all.shape[0]
    return pl.pallas_call(
        _mod_body,
        grid=(DEPTH, 6 * D_MODEL // tn),
        in_specs=[
            pl.BlockSpec((rows, D_MODEL), lambda l, j: (0, 0)),
            pl.BlockSpec((1, D_MODEL, tn), lambda l, j: (l, 0, j)),
            pl.BlockSpec((1, 1, tn), lambda l, j: (l, 0, j)),
        ],
        out_specs=pl.BlockSpec((1, rows, tn), lambda l, j: (l, 0, j)),
        out_shape=jax.ShapeDtypeStruct((DEPTH, rows, 6 * D_MODEL), F32),
        compiler_params=_params("parallel", "parallel"),
        name="mods",
    )(c_all, w_mod, b_mod.reshape(DEPTH, 1, 6 * D_MODEL))


def _proj_att_body(*refs, rope):
    if rope:
        x_ref, m_ref, g_ref, w_ref, qg_ref, kg_ref, cos_ref, sin_ref, q_ref, k_ref, v_ref = refs
    else:
        x_ref, m_ref, g_ref, w_ref, qg_ref, kg_ref, q_ref, k_ref, v_ref = refs
    m = m_ref[0]
    h = _norm_mod(x_ref[0], g_ref[...], m[0:1], m[1:2]).astype(BF16)
    p = jnp.dot(h, w_ref[...], preferred_element_type=F32)

    def head(t, gain):
        t = t * lax.rsqrt(jnp.mean(t * t, axis=-1, keepdims=True) + EPS) * gain
        if rope:
            t = t * cos_ref[...] + pltpu.roll(t, HEAD_DIM // 2, 1) * sin_ref[...]
        return t

    for i in range(ATT_HEADS):
        sl = slice(i * HEAD_DIM, (i + 1) * HEAD_DIM)
        q_ref[0, :, sl] = (head(p[:, sl], qg_ref[...]) * Q_SCALE).astype(BF16)
    for i in range(ATT_KV_HEADS):
        sl = slice(i * HEAD_DIM, (i + 1) * HEAD_DIM)
        src = slice(ATT_Q_W + i * HEAD_DIM, ATT_Q_W + (i + 1) * HEAD_DIM)
        k_ref[0, :, sl] = head(p[:, src], kg_ref[...]).astype(BF16)
    v_ref[0] = p[:, ATT_Q_W + ATT_KV_W:].astype(BF16)


def _proj_att(x, mods, g, w, qg, kg, rope_tabs, tm):
    bm, r, _ = x.shape
    rope = rope_tabs is not None
    in_specs = [
        pl.BlockSpec((1, tm, D_MODEL), lambda b, i: (b, i, 0)),
        pl.BlockSpec((1, 6, D_MODEL), lambda b, i: (b, 0, 0)),
        pl.BlockSpec((1, D_MODEL), lambda b, i: (0, 0)),
        pl.BlockSpec((D_MODEL, ATT_COLS), lambda b, i: (0, 0)),
        pl.BlockSpec((1, HEAD_DIM), lambda b, i: (0, 0)),
        pl.BlockSpec((1, HEAD_DIM), lambda b, i: (0, 0)),
    ]
    args = [x, mods, g, w, qg, kg]
    if rope:
        in_specs += [pl.BlockSpec((tm, HEAD_DIM), lambda b, i: (i, 0))] * 2
        args += list(rope_tabs)
    return pl.pallas_call(
        functools.partial(_proj_att_body, rope=rope),
        grid=(bm, r // tm),
        in_specs=in_specs,
        out_specs=[
            pl.BlockSpec((1, tm, ATT_Q_W), lambda b, i: (b, i, 0)),
            pl.BlockSpec((1, tm, ATT_KV_W), lambda b, i: (b, i, 0)),
            pl.BlockSpec((1, tm, ATT_KV_W), lambda b, i: (b, i, 0)),
        ],
        out_shape=[
            jax.ShapeDtypeStruct((bm, r, ATT_Q_W), BF16),
            jax.ShapeDtypeStruct((bm, r, ATT_KV_W), BF16),
            jax.ShapeDtypeStruct((bm, r, ATT_KV_W), BF16),
        ],
        compiler_params=_params("parallel", "parallel"),
        name="proj_att",
    )(*args)


def _proj_plain_body(x_ref, m_ref, g_ref, w_ref, o_ref):
    m = m_ref[0]
    h = _norm_mod(x_ref[0], g_ref[...], m[0:1], m[1:2]).astype(BF16)
    o_ref[0] = jnp.dot(h, w_ref[...], preferred_element_type=F32)


def _proj_plain(x, mods, g, w, tm):
    bm, r, _ = x.shape
    n = w.shape[1]
    return pl.pallas_call(
        _proj_plain_body,
        grid=(bm, r // tm),
        in_specs=[
            pl.BlockSpec((1, tm, D_MODEL), lambda b, i: (b, i, 0)),
            pl.BlockSpec((1, 6, D_MODEL), lambda b, i: (b, 0, 0)),
            pl.BlockSpec((1, D_MODEL), lambda b, i: (0, 0)),
            pl.BlockSpec((D_MODEL, n), lambda b, i: (0, 0)),
        ],
        out_specs=pl.BlockSpec((1, tm, n), lambda b, i: (b, i, 0)),
        out_shape=jax.ShapeDtypeStruct((bm, r, n), F32),
        compiler_params=_params("parallel", "parallel"),
        name="proj_gla",
    )(x, mods, g, w)


def _proj_t_body(x_ref, m_ref, g_ref, wt_ref, o_ref):
    m = m_ref[0]
    h = _norm_mod(x_ref[0], g_ref[...], m[0:1], m[1:2]).astype(BF16)
    o_ref[0] = lax.dot_general(wt_ref[...], h, NT_DIMS, preferred_element_type=F32)


def _proj_t(x, mods, g, wt, tm):
    bm, r, _ = x.shape
    n = wt.shape[0]
    return pl.pallas_call(
        _proj_t_body,
        grid=(bm, r // tm),
        in_specs=[
            pl.BlockSpec((1, tm, D_MODEL), lambda b, i: (b, i, 0)),
            pl.BlockSpec((1, 6, D_MODEL), lambda b, i: (b, 0, 0)),
            pl.BlockSpec((1, D_MODEL), lambda b, i: (0, 0)),
            pl.BlockSpec((n, D_MODEL), lambda b, i: (0, 0)),
        ],
        out_specs=pl.BlockSpec((1, n, tm), lambda b, i: (b, 0, i)),
        out_shape=jax.ShapeDtypeStruct((bm, n, r), F32),
        compiler_params=_params("parallel", "parallel"),
        name="proj_hy",
    )(x, mods, g, wt)


def _attn_body(*refs, has_lat, tk):
    if has_lat:
        q_ref, kl_ref, vl_ref, kc_ref, vc_ref, o_ref = refs
    else:
        q_ref, kc_ref, vc_ref, o_ref = refs
    q = q_ref[0]
    tq = q.shape[0]
    q4 = jnp.concatenate([q[:, g * HEAD_DIM:(g + 1) * HEAD_DIM] for g in range(ATT_GROUP)], axis=0)

    def step(k_c, v_c, state):
        s = lax.dot_general(q4, k_c, NT_DIMS, preferred_element_type=F32)
        mc = jnp.max(s, axis=-1, keepdims=True)
        if state is None:
            p = jnp.exp2(s - mc)
            return mc, jnp.sum(p, axis=-1, keepdims=True), jnp.dot(p.astype(BF16), v_c,
                                                                    preferred_element_type=F32)
        m, den, acc = state
        m_new = jnp.maximum(m, mc)
        alpha = jnp.exp2(m - m_new)
        p = jnp.exp2(s - m_new)
        den = alpha * den + jnp.sum(p, axis=-1, keepdims=True)
        acc = alpha * acc + jnp.dot(p.astype(BF16), v_c, preferred_element_type=F32)
        return m_new, den, acc

    state = step(kc_ref[0], vc_ref[0], None)
    if has_lat:
        for j in range(kl_ref.shape[1] // tk):
            state = step(kl_ref[0, j * tk:(j + 1) * tk, :], vl_ref[0, j * tk:(j + 1) * tk, :], state)
    _, den, acc = state
    o = acc / den
    o_ref[0] = jnp.concatenate([o[g * tq:(g + 1) * tq] for g in range(ATT_GROUP)], axis=1)


def _attn(q, k_lat, v_lat, k_ctx, v_ctx, tq, tk=2048):
    b, sq, _ = q.shape
    has_lat = k_lat is not None
    gw = ATT_GROUP * HEAD_DIM
    in_specs = [pl.BlockSpec((1, tq, gw), lambda bb, h, i: (bb, i, h))]
    args = [q]
    if has_lat:
        sk = k_lat.shape[1]
        in_specs += [pl.BlockSpec((1, sk, HEAD_DIM), lambda bb, h, i: (bb, 0, h))] * 2
        args += [k_lat, v_lat]
    sc = k_ctx.shape[1]
    in_specs += [pl.BlockSpec((1, sc, HEAD_DIM), lambda bb, h, i: (bb, 0, h))] * 2
    args += [k_ctx, v_ctx]
    return pl.pallas_call(
        functools.partial(_attn_body, has_lat=has_lat, tk=tk),
        grid=(b, ATT_KV_HEADS, sq // tq),
        in_specs=in_specs,
        out_specs=pl.BlockSpec((1, tq, gw), lambda bb, h, i: (bb, i, h)),
        out_shape=jax.ShapeDtypeStruct((b, sq, ATT_Q_W), F32),
        compiler_params=_params("parallel", "parallel", "parallel"),
        name="attn",
    )(*args)


def _gla_dir(q, k, v, a, wa2, ba, tri, st_ref, reverse):
    tt = q.shape[0]
    nc = tt // GLA_CHUNK
    z = jnp.dot(a, wa2, precision=HIGHEST, preferred_element_type=F32) + ba
    la = (jnp.minimum(z, 0.0) - jnp.log(1.0 + jnp.exp(-jnp.abs(z)))) * (1.0 / GLA_TAU)
    cum = jnp.dot(tri, la, precision=HIGHEST, preferred_element_type=F32)
    qd = (q * (GLA_DK ** -0.5) * jnp.exp(cum)).astype(BF16)
    ki = (k * jnp.exp(-cum)).astype(BF16)
    vb = v.astype(BF16)
    row = lax.broadcasted_iota(jnp.int32, (GLA_CHUNK, GLA_CHUNK), 0)
    col = lax.broadcasted_iota(jnp.int32, (GLA_CHUNK, GLA_CHUNK), 1)
    mask = (col >= row) if reverse else (col <= row)
    outs = [None] * nc
    for c in (range(nc - 1, -1, -1) if reverse else range(nc)):
        r0 = c * GLA_CHUNK
        rs = slice(r0, r0 + GLA_CHUNK)
        end = r0 if reverse else r0 + GLA_CHUNK - 1
        cl = cum[end:end + 1, :]
        kte = (k[rs] * jnp.exp(cl - cum[rs])).astype(BF16)
        st = st_ref[...]
        stb = st.astype(BF16)
        o_heads, upd = [], []
        for h in range(GLA_HEADS):
            cs = slice(h * GLA_DK, (h + 1) * GLA_DK)
            vs = slice(h * GLA_DV, (h + 1) * GLA_DV)
            qh, kh, vh = qd[rs, cs], ki[rs, cs], vb[rs, vs]
            att = lax.dot_general(qh, kh, NT_DIMS, preferred_element_type=F32)
            att = jnp.where(mask, att, 0.0).astype(BF16)
            o_heads.append(jnp.dot(att, vh, preferred_element_type=F32)
                           + lax.dot_general(qh, stb[:, cs], NT_DIMS, preferred_element_type=F32))
            upd.append(lax.dot_general(vh, kte[:, cs], TN_DIMS, preferred_element_type=F32))
        st_ref[...] = st * jnp.exp(cl) + jnp.concatenate(upd, axis=1)
        outs[c] = jnp.concatenate(o_heads, axis=1)
    return jnp.concatenate(outs, axis=0)


def _gla_body(qf, kf, vf, af, qb, kb, vb, ab, wa2, ba, tri, s0f, s0b,
              of, ob, sf_out, sb_out, sf_scr, sb_scr):
    @pl.when(pl.program_id(1) == 0)
    def _():
        sf_scr[...] = s0f[0]
        sb_scr[...] = s0b[0]

    a_f = af[0]
    a_b = ab[0]
    of[0] = _gla_dir(qf[0], kf[0], vf[0], a_f[:, :GLA_RANK], wa2[0], ba[0], tri[0], sf_scr, False)
    ob[0] = _gla_dir(qb[0], kb[0], vb[0], a_b[:, GLA_RANK:2 * GLA_RANK], wa2[1], ba[1], tri[1],
                     sb_scr, True)
    sf_out[0] = sf_scr[...]
    sb_out[0] = sb_scr[...]


def _gla(p, wa2, ba, tri, s0f, s0b, tt):
    b, s, _ = p.shape
    n = s // tt
    a_blk = (2 * GLA_K_W + 2 * GLA_V_W) // GLA_A_PAD

    def specs(rev):
        t = (lambda j: n - 1 - j) if rev else (lambda j: j)
        return [
            pl.BlockSpec((1, tt, GLA_K_W), lambda bb, j: (bb, t(j), 0)),
            pl.BlockSpec((1, tt, GLA_K_W), lambda bb, j: (bb, t(j), 1)),
            pl.BlockSpec((1, tt, GLA_V_W), lambda bb, j: (bb, t(j), 1)),
            pl.BlockSpec((1, tt, GLA_A_PAD), lambda bb, j: (bb, t(j), a_blk)),
        ]

    st_spec = pl.BlockSpec((1, GLA_DV, GLA_K_W), lambda bb, j: (bb, 0, 0))
    st_shape = jax.ShapeDtypeStruct((b, GLA_DV, GLA_K_W), F32)
    return pl.pallas_call(
        _gla_body,
        grid=(b, n),
        in_specs=specs(False) + specs(True) + [
            pl.BlockSpec((2, GLA_RANK, GLA_K_W), lambda bb, j: (0, 0, 0)),
            pl.BlockSpec((2, 1, GLA_K_W), lambda bb, j: (0, 0, 0)),
            pl.BlockSpec((2, tt, tt), lambda bb, j: (0, 0, 0)),
            st_spec, st_spec,
        ],
        out_specs=[
            pl.BlockSpec((1, tt, GLA_V_W), lambda bb, j: (bb, j, 0)),
            pl.BlockSpec((1, tt, GLA_V_W), lambda bb, j: (bb, n - 1 - j, 0)),
            st_spec, st_spec,
        ],
        out_shape=[
            jax.ShapeDtypeStruct((b, s, GLA_V_W), F32),
            jax.ShapeDtypeStruct((b, s, GLA_V_W), F32),
            st_shape, st_shape,
        ],
        scratch_shapes=[pltpu.VMEM((GLA_DV, GLA_K_W), F32), pltpu.VMEM((GLA_DV, GLA_K_W), F32)],
        compiler_params=_params("parallel", "arbitrary"),
        name="gla",
    )(p, p, p, p, p, p, p, p, wa2, ba, tri, s0f, s0b)


def _gla_tri(tt):
    t = np.arange(tt)
    same = (t[:, None] // GLA_CHUNK) == (t[None, :] // GLA_CHUNK)
    fwd = same & (t[None, :] <= t[:, None])
    bwd = same & (t[None, :] >= t[:, None])
    return jnp.asarray(np.stack([fwd, bwd]).astype(np.float32))


def _hy_filter_body(zt_ref, w1t_ref, b1_ref, fr_ref, w2t_ref, b2_ref, w3t_ref, dec_ref,
                    f_ref, b_ref):
    zt = zt_ref[...]
    fr = fr_ref[...]
    h1 = jnp.sin(fr * (jnp.dot(w1t_ref[...], zt, precision=HIGHEST, preferred_element_type=F32)
                       + b1_ref[...]))
    h2 = jnp.sin(fr * (jnp.dot(w2t_ref[...], h1, precision=HIGHEST, preferred_element_type=F32)
                       + b2_ref[...]))
    tn = zt[0:1, :]
    f = jnp.dot(w3t_ref[0, 0], h2, precision=HIGHEST, preferred_element_type=F32)
    f = f * jnp.exp(-tn * jnp.abs(dec_ref[0, 0]))
    b = jnp.dot(w3t_ref[0, 1], h2, precision=HIGHEST, preferred_element_type=F32)
    b = b * jnp.exp(-tn * jnp.abs(dec_ref[0, 1]))
    lane = lax.broadcasted_iota(jnp.int32, b.shape, 1)
    b = jnp.where(lane == 0, 0.0, b)
    den = (jnp.sum(jnp.abs(f), axis=-1, keepdims=True)
           + jnp.sum(jnp.abs(b), axis=-1, keepdims=True) + EPS)
    f_ref[0] = f / den
    b_ref[0] = b / den


def _hy_filters(n, w1, b1, fr, w2, b2, w3, dec):
    t = jnp.arange(n, dtype=F32)
    t_norm = t / n
    bands = jnp.linspace(1e-4, HY_BANDS - 1, HY_BANDS, dtype=F32)
    phase = (2 * math.pi / n) * t[:, None] * bands[None, :]
    z = jnp.concatenate([t_norm[:, None], jnp.cos(phase), -jnp.sin(phase)], axis=-1)
    zt = jnp.pad(z.T, ((0, HY_EMB_PAD - HY_EMB), (0, 0)))
    w1t = jnp.pad(w1.T, ((0, 0), (0, HY_EMB_PAD - HY_EMB)))
    w3t = w3.T.reshape(HY_ORDER, 2, HY_WIDTH, HY_FFN)
    ct = 128
    col = lambda v: v.reshape(HY_FFN, 1)
    full = lambda shape: pl.BlockSpec(shape, lambda o, c: (0,) * len(shape))
    out_spec = pl.BlockSpec((1, ct, n), lambda o, c: (o, c, 0))
    out_shape = jax.ShapeDtypeStruct((HY_ORDER, HY_WIDTH, n), F32)
    return pl.pallas_call(
        _hy_filter_body,
        grid=(HY_ORDER, HY_WIDTH // ct),
        in_specs=[
            full((HY_EMB_PAD, n)), full((HY_FFN, HY_EMB_PAD)), full((HY_FFN, 1)), full((HY_FFN, 1)),
            full((HY_FFN, HY_FFN)), full((HY_FFN, 1)),
            pl.BlockSpec((1, 2, ct, HY_FFN), lambda o, c: (o, 0, c, 0)),
            pl.BlockSpec((1, 2, ct, 1), lambda o, c: (o, 0, c, 0)),
        ],
        out_specs=[out_spec, out_spec],
        out_shape=[out_shape, out_shape],
        compiler_params=_params("parallel", "parallel"),
        name="hy_filter",
    )(zt, w1t, col(b1), col(fr), w2.T, col(b2), w3t, dec.reshape(HY_ORDER, 2, HY_WIDTH, 1))


def _dft_consts(n1):
    n = n1 * LANES
    half = n1 // 2
    g = DFT_GROUP
    eye = np.eye(g)
    k1 = np.arange(n1)[:, None]
    i1 = np.arange(half)[None, :]
    a1 = 2 * np.pi * k1 * i1 / n1
    g1k = np.concatenate([np.kron(eye, np.cos(a1)), np.kron(eye, -np.sin(a1))], axis=0)
    i2 = np.arange(LANES)[None, :]
    at = 2 * np.pi * k1 * i2 / n
    twr = np.tile(np.cos(at), (g, 1))
    twi = np.tile(-np.sin(at), (g, 1))
    a2 = 2 * np.pi * np.arange(LANES)[:, None] * np.arange(LANES)[None, :] / LANES
    cplx = lambda fr, fi: np.block([[fr, fi], [-fi, fr]])
    f2 = cplx(np.cos(a2), -np.sin(a2))
    f2i = cplx(np.cos(a2), np.sin(a2))
    ai = a1.T
    gik = np.concatenate([np.kron(eye, np.cos(ai)), np.kron(eye, -np.sin(ai))], axis=1)
    bf = lambda m: jnp.asarray(m, dtype=BF16)
    return dict(g1k=bf(g1k), twr=jnp.asarray(twr, F32), twi=jnp.asarray(twi, F32),
                f2=bf(f2), f2i=bf(f2i), gik=bf(gik))


def _fwd_dft(x, g1k, twr, twi, f2):
    a = jnp.dot(g1k, x.astype(BF16), preferred_element_type=F32)
    h = a.shape[0] // 2
    ar, ai = a[:h], a[h:]
    tr = ar * twr - ai * twi
    ti = ar * twi + ai * twr
    p = jnp.dot(jnp.concatenate([tr, ti], axis=1).astype(BF16), f2, preferred_element_type=F32)
    return p[:, :LANES], p[:, LANES:]


def _inv_dft(yr, yi, twr, twi, f2i, gik):
    p = jnp.dot(jnp.concatenate([yr, yi], axis=1).astype(BF16), f2i, preferred_element_type=F32)
    br, bi = p[:, :LANES], p[:, LANES:]
    tr = br * twr + bi * twi
    ti = bi * twr - br * twi
    return jnp.dot(gik, jnp.concatenate([tr, ti], axis=0).astype(BF16), preferred_element_type=F32)


def _hy_spec_body(f_ref, b_ref, g1k, twr, twi, f2, hr_ref, hi_ref, *, groups, n1):
    half = n1 // 2
    scale = 1.0 / (n1 * LANES)

    def body(g, carry):
        rows = pl.ds(pl.multiple_of(g * (DFT_GROUP * half), DFT_GROUP * half), DFT_GROUP * half)
        zfr, zfi = _fwd_dft(f_ref[rows, :], g1k[...], twr[...], twi[...], f2[...])
        zbr, zbi = _fwd_dft(b_ref[rows, :], g1k[...], twr[...], twi[...], f2[...])
        orow = pl.ds(pl.multiple_of(g * (DFT_GROUP * n1), DFT_GROUP * n1), DFT_GROUP * n1)
        hr_ref[orow, :] = (zfr + zbr) * scale
        hi_ref[orow, :] = (zfi - zbi) * scale
        return carry

    lax.fori_loop(0, groups, body, 0)


def _const_spec(arr, ngrid):
    nd = arr.ndim
    return pl.BlockSpec(arr.shape, lambda *_: (0,) * nd)


def _hy_spec(f, b, consts, n1):
    half = n1 // 2
    chans = f.shape[0] // half
    ct = 64
    groups = ct // DFT_GROUP
    cs = [consts[k] for k in ("g1k", "twr", "twi", "f2")]
    in_spec = pl.BlockSpec((ct * half, LANES), lambda c: (c, 0))
    out_spec = pl.BlockSpec((ct * n1, LANES), lambda c: (c, 0))
    out_shape = jax.ShapeDtypeStruct((chans * n1, LANES), F32)
    return pl.pallas_call(
        functools.partial(_hy_spec_body, groups=groups, n1=n1),
        grid=(chans // ct,),
        in_specs=[in_spec, in_spec] + [_const_spec(c, 1) for c in cs],
        out_specs=[out_spec, out_spec],
        out_shape=[out_shape, out_shape],
        compiler_params=_params("parallel"),
        name="hy_spec",
    )(f, b, *cs)


def _hy_main_body(u1_ref, u2_ref, uz_ref, p_ref, hr_ref, hi_ref, g1k, twr, twi, f2, f2i, gik,
                  o_ref, *, groups, n1):
    half = n1 // 2
    rows_g = DFT_GROUP * half
    shape = (rows_g, LANES)
    rowi = lax.broadcasted_iota(jnp.int32, shape, 0) % half
    lane = lax.broadcasted_iota(jnp.int32, shape, 1)
    lane_first = lane == 0
    lane_last = lane == LANES - 1
    t_first = (rowi == 0) & lane_first
    t_last = (rowi == half - 1) & lane_last

    def prev(x):
        r = pltpu.roll(x, 1, 1)
        return jnp.where(t_first, 0.0, jnp.where(lane_first, pltpu.roll(r, 1, 0), r))

    def nxt(x):
        l = pltpu.roll(x, LANES - 1, 1)
        return jnp.where(t_last, 0.0, jnp.where(lane_last, pltpu.roll(l, rows_g - 1, 0), l))

    def body(g, carry):
        cs = pl.ds(pl.multiple_of(g * DFT_GROUP, DFT_GROUP), DFT_GROUP)
        pr = p_ref[pl.ds(pl.multiple_of(g * rows_g, rows_g), rows_g), :]
        col = lambda j: pr[:, j:j + 1]

        def sconv(ref, j):
            x = ref[0, cs].reshape(shape)
            return prev(x) * col(j) + x * col(j + 1) + nxt(x) * col(j + 2) + col(j + 3)

        x1 = sconv(u1_ref, 0)
        x2 = sconv(u2_ref, 4)
        z = sconv(uz_ref, 8)
        hrow = pl.ds(pl.multiple_of(g * (DFT_GROUP * n1), DFT_GROUP * n1), DFT_GROUP * n1)

        def conv(v, o):
            zr, zi = _fwd_dft(v, g1k[...], twr[...], twi[...], f2[...])
            hr = hr_ref[o, hrow, :]
            hi = hi_ref[o, hrow, :]
            return _inv_dft(zr * hr - zi * hi, zr * hi + zi * hr, twr[...], twi[...], f2i[...], gik[...])

        z = x1 * (conv(z, 0) + col(12) * z)
        z = x2 * (conv(z, 1) + col(13) * z)
        o_ref[0, cs] = z.reshape(DFT_GROUP, half, LANES)
        return carry

    lax.fori_loop(0, groups, body, 0, unroll=2)


def _hy_main(ut, prm, hr, hi, consts, n1):
    b = ut.shape[0]
    half = n1 // 2
    ct = 64
    nct = HY_WIDTH // ct
    cs = [consts[k] for k in ("g1k", "twr", "twi", "f2", "f2i", "gik")]
    u_spec = lambda sec: pl.BlockSpec((1, ct, half, LANES), lambda c, bb: (bb, sec * nct + c, 0, 0))
    h_spec = pl.BlockSpec((HY_ORDER, ct * n1, LANES), lambda c, bb: (0, c, 0))
    return pl.pallas_call(
        functools.partial(_hy_main_body, groups=ct // DFT_GROUP, n1=n1),
        grid=(nct, b),
        in_specs=[u_spec(0), u_spec(1), u_spec(2),
                  pl.BlockSpec((ct * half, 16), lambda c, bb: (c, 0)),
                  h_spec, h_spec] + [_const_spec(c, 2) for c in cs],
        out_specs=pl.BlockSpec((1, ct, half, LANES), lambda c, bb: (bb, c, 0, 0)),
        out_shape=jax.ShapeDtypeStruct((b, HY_WIDTH, half, LANES), F32),
        compiler_params=_params("parallel", "parallel"),
        name="hy_main",
    )(ut, ut, ut, prm, hr, hi, *cs)


def _hyc_consts(n):
    nn = 2 * n
    a = 2 * np.pi * np.arange(n)[:, None] * np.arange(nn)[None, :] / nn
    fc = np.concatenate([np.cos(a), -np.sin(a)], axis=1)
    fi = np.concatenate([np.cos(a.T), -np.sin(a.T)], axis=0)
    return jnp.asarray(fc, BF16), jnp.asarray(fi, BF16)


def _hyc_spec_body(f_ref, b_ref, fc_ref, hr_ref, hi_ref):
    nn = hr_ref.shape[-1]
    zf = jnp.dot(f_ref[...].astype(BF16), fc_ref[...], preferred_element_type=F32)
    zb = jnp.dot(b_ref[...].astype(BF16), fc_ref[...], preferred_element_type=F32)
    hr_ref[...] = (zf[:, :nn] + zb[:, :nn]) * (1.0 / nn)
    hi_ref[...] = (zf[:, nn:] - zb[:, nn:]) * (1.0 / nn)


def _hyc_spec(f, b, fc):
    rows, n = f.shape
    out_shape = jax.ShapeDtypeStruct((rows, 2 * n), F32)
    return pl.pallas_call(
        _hyc_spec_body,
        out_shape=[out_shape, out_shape],
        compiler_params=pltpu.CompilerParams(vmem_limit_bytes=VMEM_LIMIT_BYTES),
        name="hyc_spec",
    )(f, b, fc)


def _hyc_main_body(u1_ref, u2_ref, uz_ref, p_ref, hr_ref, hi_ref, fc_ref, fi_ref, o_ref):
    shape = u1_ref.shape[1:]
    n = shape[1]
    nn = 2 * n
    lane = lax.broadcasted_iota(jnp.int32, shape, 1)
    pr = p_ref[...]
    col = lambda j: pr[:, j:j + 1]

    def sconv(ref, j):
        x = ref[0]
        p = jnp.where(lane == 0, 0.0, pltpu.roll(x, 1, 1))
        q = jnp.where(lane == n - 1, 0.0, pltpu.roll(x, n - 1, 1))
        return p * col(j) + x * col(j + 1) + q * col(j + 2) + col(j + 3)

    def conv(v, o):
        s = jnp.dot(v.astype(BF16), fc_ref[...], preferred_element_type=F32)
        zr, zi = s[:, :nn], s[:, nn:]
        hr, hi = hr_ref[o], hi_ref[o]
        y = jnp.concatenate([zr * hr - zi * hi, zr * hi + zi * hr], axis=1).astype(BF16)
        return jnp.dot(y, fi_ref[...], preferred_element_type=F32)

    x1 = sconv(u1_ref, 0)
    x2 = sconv(u2_ref, 4)
    z = sconv(uz_ref, 8)
    z = x1 * (conv(z, 0) + col(12) * z)
    o_ref[0] = x2 * (conv(z, 1) + col(13) * z)


def _hyc_main(ut, prm, hr, hi, fc, fi, nb, n):
    ct = 128
    nct = HY_WIDTH // ct
    u_spec = lambda sec: pl.BlockSpec((1, ct, n), lambda bb, c: (0, sec * nct + c, bb))
    h_spec = pl.BlockSpec((HY_ORDER, ct, 2 * n), lambda bb, c: (0, c, 0))
    return pl.pallas_call(
        _hyc_main_body,
        grid=(nb, nct),
        in_specs=[u_spec(0), u_spec(1), u_spec(2),
                  pl.BlockSpec((ct, 16), lambda bb, c: (c, 0)),
                  h_spec, h_spec, _const_spec(fc, 2), _const_spec(fi, 2)],
        out_specs=pl.BlockSpec((1, ct, n), lambda bb, c: (0, c, bb)),
        out_shape=jax.ShapeDtypeStruct((1, HY_WIDTH, nb * n), F32),
        compiler_params=_params("parallel", "parallel"),
        name="hyc_main",
    )(ut, ut, ut, prm, hr, hi, fc, fi)


def _merge_body(x_ref, m_ref, g_ref, wbrg, yh_ref, of_ref, ob_ref, og_ref, gn_ref, ya_ref,
                wh, wg, wa, wo, o_ref):
    x = x_ref[0]
    m = m_ref[0]
    h = _norm_mod(x, g_ref[...], m[0:1], m[1:2]).astype(BF16)
    gates = jnp.dot(h, wbrg[...], preferred_element_type=F32)
    yh = yh_ref[0].T
    o = of_ref[0] + ob_ref[0]
    parts = []
    for hh in range(GLA_HEADS):
        t = o[:, hh * GLA_DV:(hh + 1) * GLA_DV]
        parts.append(t * lax.rsqrt(jnp.mean(t * t, axis=-1, keepdims=True) + EPS) * gn_ref[...])
    og = og_ref[0]
    yg = jnp.concatenate(parts, axis=1) * (og * _sigmoid(og))
    d = D_MODEL
    mm = (_sigmoid(gates[:, :d]) * jnp.dot(yh.astype(BF16), wh[...], preferred_element_type=F32)
          + _sigmoid(gates[:, d:2 * d]) * jnp.dot(yg.astype(BF16), wg[...], preferred_element_type=F32)
          + _sigmoid(gates[:, 2 * d:]) * jnp.dot(ya_ref[0].astype(BF16), wa[...],
                                                  preferred_element_type=F32))
    out = jnp.dot(mm.astype(BF16), wo[...], preferred_element_type=F32)
    o_ref[0] = x + m[2:3] * out


def _merge(x, mods, g, wbrg, yht, of, ob, pg, gn, ya, wh, wg, wa, wo, tm):
    bm, r, _ = x.shape
    row = lambda w: pl.BlockSpec((1, tm, w), lambda b, i: (b, i, 0))
    full = lambda a: pl.BlockSpec(a.shape, lambda b, i: (0,) * a.ndim)
    og_blk = (2 * GLA_K_W + GLA_V_W) // GLA_V_W
    return pl.pallas_call(
        _merge_body,
        grid=(bm, r // tm),
        in_specs=[
            row(D_MODEL),
            pl.BlockSpec((1, 6, D_MODEL), lambda b, i: (b, 0, 0)),
            full(g), full(wbrg),
            pl.BlockSpec((1, HY_WIDTH, tm), lambda b, i: (b, 0, i)),
            row(GLA_V_W), row(GLA_V_W),
            pl.BlockSpec((1, tm, GLA_V_W), lambda b, i: (b, i, og_blk)),
            full(gn), row(ATT_Q_W), full(wh), full(wg), full(wa), full(wo),
        ],
        out_specs=row(D_MODEL),
        out_shape=jax.ShapeDtypeStruct(x.shape, F32),
        compiler_params=_params("parallel", "parallel"),
        name="merge",
    )(x, mods, g, wbrg, yht, of, ob, pg, gn, ya, wh, wg, wa, wo)


def _route(logits, rb):
    scores = _sigmoid(logits)
    sel = scores + rb
    lane = lax.broadcasted_iota(jnp.int32, sel.shape, 1).astype(F32)
    neg = -jnp.inf

    def top2(v):
        m1 = jnp.max(v, axis=-1, keepdims=True)
        i1 = jnp.min(jnp.where(v == m1, lane, float(N_EXPERTS)), axis=-1, keepdims=True)
        v2 = jnp.where(lane == i1, neg, v)
        m2 = jnp.max(v2, axis=-1, keepdims=True)
        i2 = jnp.min(jnp.where(v2 == m2, lane, float(N_EXPERTS)), axis=-1, keepdims=True)
        return m1, m2, i1, i2

    group_of = jnp.floor(lane * (1.0 / EXPERTS_PER_GROUP))
    best = None
    best_g = None
    for g in range(N_GROUPS):
        m1, m2, _, _ = top2(jnp.where(group_of == float(g), sel, neg))
        gs = m1 + m2
        if best is None:
            best, best_g = gs, jnp.zeros_like(gs)
        else:
            better = gs > best
            best_g = jnp.where(better, float(g), best_g)
            best = jnp.where(better, gs, best)
    _, _, i1, i2 = top2(jnp.where(group_of == best_g, sel, neg))
    w = jnp.where((lane == i1) | (lane == i2), scores, 0.0)
    return w / jnp.sum(w, axis=-1, keepdims=True), best_g


def _moe_body(x_ref, m_ref, g_ref, rw_ref, rb_ref, wg_ref, wu_ref, wd_ref, o_ref,
              h_scr, ghi_scr, glo_scr, grp_scr, pos_scr, grpt_scr, post_scr, acc_scr):
    g = pl.program_id(2)
    gf = g.astype(F32)
    tm = x_ref.shape[1]
    sub = MOE_SUB
    lane = lax.broadcasted_iota(jnp.int32, (tm, LANES), 1).astype(F32)

    @pl.when(g == 0)
    def _():
        m = m_ref[0]
        h = _norm_mod(x_ref[0], g_ref[...], m[3:4], m[4:5])
        h_scr[...] = h.astype(BF16)
        logits = jnp.dot(h, rw_ref[...], precision=HIGHEST, preferred_element_type=F32)
        gates, grp = _route(logits, rb_ref[...])
        ghi = gates.astype(BF16)
        ghi_scr[...] = ghi
        glo_scr[...] = (gates - ghi.astype(F32)).astype(BF16)
        grp_b = jnp.broadcast_to(grp, (tm, LANES))
        onehot = (lane == grp_b).astype(BF16)
        ranks = []
        for rb_ in range(tm // sub):
            r = lax.broadcasted_iota(jnp.int32, (sub, tm), 0) + rb_ * sub
            c = lax.broadcasted_iota(jnp.int32, (sub, tm), 1)
            ranks.append(jnp.dot((c < r).astype(BF16), onehot, preferred_element_type=F32))
        rank = jnp.concatenate(ranks, axis=0)
        pos = jnp.sum(jnp.where(lane == grp_b, rank, 0.0), axis=-1, keepdims=True)
        pos_b = jnp.broadcast_to(pos, (tm, LANES))
        grp_scr[...] = grp_b
        pos_scr[...] = pos_b
        grpt_scr[...] = grp_b.T
        post_scr[...] = pos_b.T
        acc_scr[...] = jnp.zeros_like(acc_scr)

    in_group = grp_scr[...] == gf
    n_g = jnp.sum(jnp.where(in_group[:, 0:1], 1.0, 0.0)).astype(jnp.int32)
    n_sub = (n_g + (sub - 1)) // sub
    d_iota = lax.broadcasted_iota(jnp.int32, (sub, tm), 0).astype(F32)
    lane16 = lax.broadcasted_iota(jnp.int32, (sub, N_EXPERTS), 1)

    def body(k, carry):
        kf = (k * sub).astype(F32)
        p = ((grpt_scr[...] == gf) & (post_scr[...] - kf == d_iota)).astype(BF16)
        hd = jnp.dot(p, h_scr[...], preferred_element_type=F32).astype(BF16)
        gd = (jnp.dot(p, ghi_scr[...], preferred_element_type=F32)
              + jnp.dot(p, glo_scr[...], preferred_element_type=F32))
        y = jnp.zeros((sub, D_MODEL), F32)
        for j in range(EXPERTS_PER_GROUP):
            a = jnp.dot(hd, wg_ref[j], preferred_element_type=F32)
            b = jnp.dot(hd, wu_ref[j], preferred_element_type=F32)
            hid = (a * _sigmoid(a) * b).astype(BF16)
            ge = jnp.sum(jnp.where(lane16 == g * EXPERTS_PER_GROUP + j, gd, 0.0),
                         axis=-1, keepdims=True)
            y = y + ge * jnp.dot(hid, wd_ref[j], preferred_element_type=F32)
        pt = (in_group & (pos_scr[...] - kf == lane)).astype(BF16)
        acc_scr[...] += jnp.dot(pt, y.astype(BF16), preferred_element_type=F32)
        return carry

    lax.fori_loop(0, n_sub, body, 0)

    @pl.when(g == N_GROUPS - 1)
    def _():
        o_ref[0] = x_ref[0] + m_ref[0][5:6] * acc_scr[...]


def _moe(x, mods, g, rw, rb, wg, wu, wd, tm):
    bm, r, _ = x.shape
    row = pl.BlockSpec((1, tm, D_MODEL), lambda b, i, e: (b, i, 0))
    return pl.pallas_call(
        _moe_body,
        grid=(bm, r // tm, N_GROUPS),
        in_specs=[
            row,
            pl.BlockSpec((1, 6, D_MODEL), lambda b, i, e: (b, 0, 0)),
            pl.BlockSpec((1, D_MODEL), lambda b, i, e: (0, 0)),
            pl.BlockSpec((D_MODEL, N_EXPERTS), lambda b, i, e: (0, 0)),
            pl.BlockSpec((1, N_EXPERTS), lambda b, i, e: (0, 0)),
            pl.BlockSpec((EXPERTS_PER_GROUP, D_MODEL, D_EXPERT), lambda b, i, e: (e, 0, 0)),
            pl.BlockSpec((EXPERTS_PER_GROUP, D_MODEL, D_EXPERT), lambda b, i, e: (e, 0, 0)),
            pl.BlockSpec((EXPERTS_PER_GROUP, D_EXPERT, D_MODEL), lambda b, i, e: (e, 0, 0)),
        ],
        out_specs=row,
        out_shape=jax.ShapeDtypeStruct(x.shape, F32),
        scratch_shapes=[
            pltpu.VMEM((tm, D_MODEL), BF16),
            pltpu.VMEM((tm, N_EXPERTS), BF16),
            pltpu.VMEM((tm, N_EXPERTS), BF16),
            pltpu.VMEM((tm, LANES), F32),
            pltpu.VMEM((tm, LANES), F32),
            pltpu.VMEM((LANES, tm), F32),
            pltpu.VMEM((LANES, tm), F32),
            pltpu.VMEM((tm, D_MODEL), F32),
        ],
        compiler_params=_params("parallel", "parallel", "arbitrary"),
        name="moe",
    )(x, mods, g, rw, rb, wg, wu, wd)


def _rope_tables(n_tokens):
    rows = n_tokens // GRID_W
    row = jnp.broadcast_to(jnp.arange(rows)[:, None], (rows, GRID_W)).reshape(-1).astype(F32)
    col = jnp.broadcast_to(jnp.arange(GRID_W)[None, :], (rows, GRID_W)).reshape(-1).astype(F32)
    inv_freq = ROPE_THETA ** (-jnp.arange(ROPE_PAIRS_PER_AXIS, dtype=F32) / ROPE_PAIRS_PER_AXIS)
    ang = jnp.concatenate([row[:, None] * inv_freq, col[:, None] * inv_freq], axis=-1)
    cos, sin = jnp.cos(ang), jnp.sin(ang)
    return jnp.concatenate([cos, cos], axis=-1), jnp.concatenate([-sin, sin], axis=-1)


def _split_w_in(w):
    sizes = (ATT_KV_W, ATT_KV_W, GLA_K_W, GLA_V_W, 2 * GLA_RANK,
             ATT_Q_W, GLA_K_W, GLA_V_W, 3 * HY_WIDTH, 3 * D_MODEL)
    cuts = [int(v) for v in np.cumsum(sizes)[:-1]]
    a_k, a_v, g_k, g_v, g_a, a_q, g_q, g_og, hy_u, br_g = jnp.split(w, cuts, axis=1)
    perm = np.concatenate([np.arange(0, HEAD_DIM, 2), np.arange(1, HEAD_DIM, 2)])
    perm_q = np.concatenate([h * HEAD_DIM + perm for h in range(ATT_HEADS)])
    perm_k = np.concatenate([h * HEAD_DIM + perm for h in range(ATT_KV_HEADS)])
    w_att = jnp.concatenate([a_q[:, perm_q], a_k[:, perm_k], a_v], axis=1).astype(BF16)
    pad = jnp.zeros((D_MODEL, GLA_A_PAD - 2 * GLA_RANK), w.dtype)
    w_gla = jnp.concatenate([g_q, g_k, g_v, g_og, g_a, pad], axis=1).astype(BF16)
    return w_att, w_gla, hy_u.T.astype(BF16), br_g.astype(BF16), perm


def kernel(x, c, ctx, c_ctx, w_mod, b_mod, norm1_g, norm2_g, w_in, q_norm_g, k_norm_g, gla_wa2, gla_ba, gla_norm_g, hy_conv_w, hy_conv_b, hy_pos_w1, hy_pos_b1, hy_sin_freq, hy_pos_w2, hy_pos_b2, hy_pos_w3, hy_decay, hy_skip, w_br_hy, w_br_gla, w_br_att, w_out, router_w, router_b, moe_w_gate, moe_w_up, moe_w_down):
    nb, seq, d = x.shape
    nctx = ctx.shape[1]
    n1_lat = 2 * seq // LANES

    c_all = jnp.concatenate([c, c_ctx[None, :], jnp.zeros((16 - nb - 1, d), F32)], axis=0)
    mods_all = _mods(c_all, w_mod, b_mod)

    rope_tabs = _rope_tables(seq)
    consts = _dft_consts(n1_lat)
    fc, fi = _hyc_consts(nctx)
    tri_lat = _gla_tri(256)
    zero_state = jnp.zeros((nb, GLA_DV, GLA_K_W), F32)
    rb = router_b.reshape(1, N_EXPERTS)

    xc = ctx.reshape(1, nb * nctx, d)
    for l in range(DEPTH):
        last = l == DEPTH - 1
        mods_lat = mods_all[l, :nb].reshape(nb, 6, d)
        mods_ctx = mods_all[l, nb:nb + 1].reshape(1, 6, d)
        g1 = norm1_g[l].reshape(1, d)
        g2 = norm2_g[l].reshape(1, d)
        w_att, w_gla, w_hyt, w_brg, perm = _split_w_in(w_in[l])
        qg = q_norm_g[l][perm].reshape(1, HEAD_DIM)
        kg = k_norm_g[l][perm].reshape(1, HEAD_DIM)
        wa2 = gla_wa2[l]
        ba = gla_ba[l].reshape(2, 1, GLA_K_W)
        gn = gla_norm_g[l].reshape(1, GLA_DV)
        wh, wg, wa, wo = (w_br_hy[l].astype(BF16), w_br_gla[l].astype(BF16),
                          w_br_att[l].astype(BF16), w_out[l].astype(BF16))
        cw, cb = hy_conv_w[l], hy_conv_b[l]
        sec = lambda s: [cw[0, s], cw[1, s], cw[2, s], cb[s]]
        w_ = HY_WIDTH
        prm = jnp.stack(sec(slice(0, w_)) + sec(slice(w_, 2 * w_)) + sec(slice(2 * w_, 3 * w_))
                        + [hy_skip[l, 0], hy_skip[l, 1], jnp.zeros((w_,), F32), jnp.zeros((w_,), F32)],
                        axis=1)
        filt_args = (hy_pos_w1[l], hy_pos_b1[l], hy_sin_freq[l], hy_pos_w2[l], hy_pos_b2[l],
                     hy_pos_w3[l], hy_decay[l])

        q, k, v = _proj_att(x, mods_lat, g1, w_att, qg, kg, rope_tabs, 512)
        qc, kc, vc = _proj_att(xc, mods_ctx, g1, w_att, qg, kg, None, 256)
        kc = kc.reshape(nb, nctx, ATT_KV_W)
        vc = vc.reshape(nb, nctx, ATT_KV_W)
        pg = _proj_plain(x, mods_lat, g1, w_gla, 512)
        pgc = _proj_plain(xc, mods_ctx, g1, w_gla, 256)
        ut = _proj_t(x, mods_lat, g1, w_hyt, 512)

        y_att = _attn(q, k, v, kc, vc, 256)

        cof, cob, s_f, s_b = _gla(pgc.reshape(nb, nctx, GLA_COLS), wa2, ba, _gla_tri(nctx),
                                  zero_state, zero_state, nctx)
        of, ob, _, _ = _gla(pg, wa2, ba, tri_lat, s_f, s_b, 256)

        ff, fb = _hy_filters(seq, *filt_args)
        hr, hi = _hy_spec(ff.reshape(-1, LANES), fb.reshape(-1, LANES), consts, n1_lat)
        hr = hr.reshape(HY_ORDER, HY_WIDTH * n1_lat, LANES)
        hi = hi.reshape(HY_ORDER, HY_WIDTH * n1_lat, LANES)
        y_hyt = _hy_main(ut.reshape(nb, 3 * HY_WIDTH, n1_lat // 2, LANES),
                         jnp.repeat(prm, n1_lat // 2, axis=0), hr, hi, consts, n1_lat)
        y_hyt = y_hyt.reshape(nb, HY_WIDTH, seq)

        x = _merge(x, mods_lat, g1, w_brg, y_hyt, of, ob, pg, gn, y_att, wh, wg, wa, wo, 256)
        moe_w = (moe_w_gate[l].astype(BF16), moe_w_up[l].astype(BF16), moe_w_down[l].astype(BF16))
        x = _moe(x, mods_lat, g2, router_w, rb, *moe_w, 1024)

        if not last:
            yc_att = _attn(qc.reshape(nb, nctx, ATT_Q_W), None, None, kc, vc, 128)
            uct = _proj_t(xc, mods_ctx, g1, w_hyt, 256)
            cff, cfb = _hy_filters(nctx, *filt_args)
            chr_, chi = _hyc_spec(cff.reshape(-1, nctx), cfb.reshape(-1, nctx), fc)
            chr_ = chr_.reshape(HY_ORDER, HY_WIDTH, 2 * nctx)
            chi = chi.reshape(HY_ORDER, HY_WIDTH, 2 * nctx)
            yc_hyt = _hyc_main(uct, prm, chr_, chi, fc, fi, nb, nctx)
            xc = _merge(xc, mods_ctx, g1, w_brg, yc_hyt,
                        cof.reshape(1, nb * nctx, GLA_V_W), cob.reshape(1, nb * nctx, GLA_V_W),
                        pgc, gn, yc_att.reshape(1, nb * nctx, ATT_Q_W), wh, wg, wa, wo, 256)
            xc = _moe(xc, mods_ctx, g2, router_w, rb, *moe_w, min(1024, nb * nctx))
    return x
```

```python
import functools
import math

import numpy as np
import jax
import jax.numpy as jnp
from jax import lax
from jax.experimental import pallas as pl
from jax.experimental.pallas import tpu as pltpu

F32 = jnp.float32
BF16 = jnp.bfloat16
HIGHEST = lax.Precision.HIGHEST

D_MODEL = 1024
DEPTH = 2
GRID_W = 64
EPS = 1e-6

ATT_HEADS = 8
ATT_KV_HEADS = 2
ATT_GROUP = ATT_HEADS // ATT_KV_HEADS
HEAD_DIM = 128
ROPE_PAIRS_PER_AXIS = HEAD_DIM // 4
ROPE_THETA = 10000.0
Q_SCALE = HEAD_DIM ** -0.5 * math.log2(math.e)

GLA_HEADS = 4
GLA_DK = 64
GLA_DV = 128
GLA_RANK = 16
GLA_TAU = 16.0
GLA_CHUNK = 64

HY_WIDTH = 512
HY_ORDER = 2
HY_BANDS = 16
HY_EMB = 1 + 2 * HY_BANDS
HY_EMB_PAD = 40
HY_FFN = 64

N_EXPERTS = 16
N_GROUPS = 4
EXPERTS_PER_GROUP = N_EXPERTS // N_GROUPS
D_EXPERT = 512

ATT_Q_W = ATT_HEADS * HEAD_DIM
ATT_KV_W = ATT_KV_HEADS * HEAD_DIM
GLA_K_W = GLA_HEADS * GLA_DK
GLA_V_W = GLA_HEADS * GLA_DV
GLA_A_PAD = 128
GLA_COLS = 2 * GLA_K_W + 2 * GLA_V_W + GLA_A_PAD
ATT_COLS = ATT_Q_W + 2 * ATT_KV_W

LANES = 128
SUBLANES = 8
MOE_SUB = LANES
VMEM_LIMIT_BYTES = 60 * 1024 * 1024

DFT_GROUP = SUBLANES
HY_INTERLEAVE = 4

NT_DIMS = (((1,), (1,)), ((), ()))
TN_DIMS = (((0,), (0,)), ((), ()))


def _params(*sem):
    return pltpu.CompilerParams(dimension_semantics=sem, vmem_limit_bytes=VMEM_LIMIT_BYTES)


def _sigmoid(x):
    return 1.0 / (1.0 + jnp.exp(-x))


def _norm_mod(x, g, shift, scale):
    ms = jnp.mean(x * x, axis=-1, keepdims=True)
    return (x * lax.rsqrt(ms + EPS) * g) * (1.0 + scale) + shift


def _mod_body(c_ref, w_ref, b_ref, o_ref):
    c = c_ref[...]
    s = c * _sigmoid(c)
    o_ref[0] = jnp.dot(s, w_ref[0], precision=HIGHEST, preferred_element_type=F32) + b_ref[0]


def _mods(c_all, w_mod, b_mod):
    tn = 512
    rows = c_all.shape[0]
    return pl.pallas_call(
        _mod_body,
        grid=(DEPTH, 6 * D_MODEL // tn),
        in_specs=[
            pl.BlockSpec((rows, D_MODEL), lambda l, j: (0, 0)),
            pl.BlockSpec((1, D_MODEL, tn), lambda l, j: (l, 0, j)),
            pl.BlockSpec((1, 1, tn), lambda l, j: (l, 0, j)),
        ],
        out_specs=pl.BlockSpec((1, rows, tn), lambda l, j: (l, 0, j)),
        out_shape=jax.ShapeDtypeStruct((DEPTH, rows, 6 * D_MODEL), F32),
        compiler_params=_params("parallel", "parallel"),
        name="mods",
    )(c_all, w_mod, b_mod.reshape(DEPTH, 1, 6 * D_MODEL))


def _proj_att_body(*refs, rope):
    if rope:
        x_ref, m_ref, g_ref, w_ref, qg_ref, kg_ref, cos_ref, sin_ref, q_ref, k_ref, v_ref = refs
    else:
        x_ref, m_ref, g_ref, w_ref, qg_ref, kg_ref, q_ref, k_ref, v_ref = refs
    m = m_ref[0]
    h = _norm_mod(x_ref[0], g_ref[...], m[0:1], m[1:2]).astype(BF16)
    p = jnp.dot(h, w_ref[...], preferred_element_type=F32)

    def head(t, gain):
        t = t * lax.rsqrt(jnp.mean(t * t, axis=-1, keepdims=True) + EPS) * gain
        if rope:
            t = t * cos_ref[...] + pltpu.roll(t, HEAD_DIM // 2, 1) * sin_ref[...]
        return t

    for i in range(ATT_HEADS):
        sl = slice(i * HEAD_DIM, (i + 1) * HEAD_DIM)
        q_ref[0, :, sl] = (head(p[:, sl], qg_ref[...]) * Q_SCALE).astype(BF16)
    for i in range(ATT_KV_HEADS):
        sl = slice(i * HEAD_DIM, (i + 1) * HEAD_DIM)
        src = slice(ATT_Q_W + i * HEAD_DIM, ATT_Q_W + (i + 1) * HEAD_DIM)
        k_ref[0, :, sl] = head(p[:, src], kg_ref[...]).astype(BF16)
    v_ref[0] = p[:, ATT_Q_W + ATT_KV_W:].astype(BF16)


def _proj_att(x, mods, g, w, qg, kg, rope_tabs, tm):
    bm, r, _ = x.shape
    rope = rope_tabs is not None
    in_specs = [
        pl.BlockSpec((1, tm, D_MODEL), lambda b, i: (b, i, 0)),
        pl.BlockSpec((1, 6, D_MODEL), lambda b, i: (b, 0, 0)),
        pl.BlockSpec((1, D_MODEL), lambda b, i: (0, 0)),
        pl.BlockSpec((D_MODEL, ATT_COLS), lambda b, i: (0, 0)),
        pl.BlockSpec((1, HEAD_DIM), lambda b, i: (0, 0)),
        pl.BlockSpec((1, HEAD_DIM), lambda b, i: (0, 0)),
    ]
    args = [x, mods, g, w, qg, kg]
    if rope:
        in_specs += [pl.BlockSpec((tm, HEAD_DIM), lambda b, i: (i, 0))] * 2
        args += list(rope_tabs)
    return pl.pallas_call(
        functools.partial(_proj_att_body, rope=rope),
        grid=(bm, r // tm),
        in_specs=in_specs,
        out_specs=[
            pl.BlockSpec((1, tm, ATT_Q_W), lambda b, i: (b, i, 0)),
            pl.BlockSpec((1, tm, ATT_KV_W), lambda b, i: (b, i, 0)),
            pl.BlockSpec((1, tm, ATT_KV_W), lambda b, i: (b, i, 0)),
        ],
        out_shape=[
            jax.ShapeDtypeStruct((bm, r, ATT_Q_W), BF16),
            jax.ShapeDtypeStruct((bm, r, ATT_KV_W), BF16),
            jax.ShapeDtypeStruct((bm, r, ATT_KV_W), BF16),
        ],
        compiler_params=_params("parallel", "parallel"),
        name="proj_att",
    )(*args)


def _proj_plain_body(x_ref, m_ref, g_ref, w_ref, o_ref):
    m = m_ref[0]
    h = _norm_mod(x_ref[0], g_ref[...], m[0:1], m[1:2]).astype(BF16)
    o_ref[0] = jnp.dot(h, w_ref[...], preferred_element_type=F32)


def _proj_plain(x, mods, g, w, tm):
    bm, r, _ = x.shape
    n = w.shape[1]
    return pl.pallas_call(
        _proj_plain_body,
        grid=(bm, r // tm),
        in_specs=[
            pl.BlockSpec((1, tm, D_MODEL), lambda b, i: (b, i, 0)),
            pl.BlockSpec((1, 6, D_MODEL), lambda b, i: (b, 0, 0)),
            pl.BlockSpec((1, D_MODEL), lambda b, i: (0, 0)),
            pl.BlockSpec((D_MODEL, n), lambda b, i: (0, 0)),
        ],
        out_specs=pl.BlockSpec((1, tm, n), lambda b, i: (b, i, 0)),
        out_shape=jax.ShapeDtypeStruct((bm, r, n), F32),
        compiler_params=_params("parallel", "parallel"),
        name="proj_gla",
    )(x, mods, g, w)


def _proj_t_body(x_ref, m_ref, g_ref, wt_ref, o_ref):
    m = m_ref[0]
    h = _norm_mod(x_ref[0], g_ref[...], m[0:1], m[1:2]).astype(BF16)
    o_ref[0] = lax.dot_general(wt_ref[...], h, NT_DIMS, preferred_element_type=F32)


def _proj_t(x, mods, g, wt, tm):
    bm, r, _ = x.shape
    n = wt.shape[0]
    return pl.pallas_call(
        _proj_t_body,
        grid=(bm, r // tm),
        in_specs=[
            pl.BlockSpec((1, tm, D_MODEL), lambda b, i: (b, i, 0)),
            pl.BlockSpec((1, 6, D_MODEL), lambda b, i: (b, 0, 0)),
            pl.BlockSpec((1, D_MODEL), lambda b, i: (0, 0)),
            pl.BlockSpec((n, D_MODEL), lambda b, i: (0, 0)),
        ],
        out_specs=pl.BlockSpec((1, n, tm), lambda b, i: (b, 0, i)),
        out_shape=jax.ShapeDtypeStruct((bm, n, r), F32),
        compiler_params=_params("parallel", "parallel"),
        name="proj_hy",
    )(x, mods, g, wt)


def _attn_body(*refs, has_lat, tk):
    if has_lat:
        q_ref, kl_ref, vl_ref, kc_ref, vc_ref, o_ref = refs
    else:
        q_ref, kc_ref, vc_ref, o_ref = refs
    q = q_ref[0]
    tq = q.shape[0]
    q4 = jnp.concatenate([q[:, g * HEAD_DIM:(g + 1) * HEAD_DIM] for g in range(ATT_GROUP)], axis=0)

    def scores(k_c):
        return lax.dot_general(q4, k_c, NT_DIMS, preferred_element_type=F32)

    def update(s, v_c, state):
        mc = jnp.max(s, axis=-1, keepdims=True)
        if state is None:
            p = jnp.exp2(s - mc)
            return mc, jnp.sum(p, axis=-1, keepdims=True), jnp.dot(p.astype(BF16), v_c,
                                                                    preferred_element_type=F32)
        m, den, acc = state
        m_new = jnp.maximum(m, mc)
        alpha = jnp.exp2(m - m_new)
        p = jnp.exp2(s - m_new)
        den = alpha * den + jnp.sum(p, axis=-1, keepdims=True)
        acc = alpha * acc + jnp.dot(p.astype(BF16), v_c, preferred_element_type=F32)
        return m_new, den, acc

    chunks = [(kc_ref[0], vc_ref[0])]
    if has_lat:
        chunks += [(kl_ref[0, j * tk:(j + 1) * tk, :], vl_ref[0, j * tk:(j + 1) * tk, :])
                   for j in range(kl_ref.shape[1] // tk)]
    state = None
    s_next = scores(chunks[0][0])
    for j, (_, v_c) in enumerate(chunks):
        s_cur = s_next
        if j + 1 < len(chunks):
            s_next = scores(chunks[j + 1][0])
        state = update(s_cur, v_c, state)
    _, den, acc = state
    o = acc / den
    o_ref[0] = jnp.concatenate([o[g * tq:(g + 1) * tq] for g in range(ATT_GROUP)], axis=1)


def _attn(q, k_lat, v_lat, k_ctx, v_ctx, tq, tk=1024):
    b, sq, _ = q.shape
    has_lat = k_lat is not None
    gw = ATT_GROUP * HEAD_DIM
    in_specs = [pl.BlockSpec((1, tq, gw), lambda bb, h, i: (bb, i, h))]
    args = [q]
    if has_lat:
        sk = k_lat.shape[1]
        in_specs += [pl.BlockSpec((1, sk, HEAD_DIM), lambda bb, h, i: (bb, 0, h))] * 2
        args += [k_lat, v_lat]
    sc = k_ctx.shape[1]
    in_specs += [pl.BlockSpec((1, sc, HEAD_DIM), lambda bb, h, i: (bb, 0, h))] * 2
    args += [k_ctx, v_ctx]
    return pl.pallas_call(
        functools.partial(_attn_body, has_lat=has_lat, tk=tk),
        grid=(b, ATT_KV_HEADS, sq // tq),
        in_specs=in_specs,
        out_specs=pl.BlockSpec((1, tq, gw), lambda bb, h, i: (bb, i, h)),
        out_shape=jax.ShapeDtypeStruct((b, sq, ATT_Q_W), F32),
        compiler_params=_params("parallel", "parallel", "parallel"),
        name="attn",
    )(*args)


def _gla_pair(qs, ks, vs, a_s, wa2, ba, tri, st_refs):
    tt = qs[0].shape[0]
    nc = tt // GLA_CHUNK
    dirs = range(2)
    heads = range(GLA_HEADS)
    zs = [jnp.dot(a_s[d], wa2[d], precision=HIGHEST, preferred_element_type=F32) + ba[d] for d in dirs]
    las = [(jnp.minimum(z, 0.0) - jnp.log(1.0 + jnp.exp(-jnp.abs(z)))) * (1.0 / GLA_TAU) for z in zs]
    cums = [jnp.dot(tri[d], las[d], precision=HIGHEST, preferred_element_type=F32) for d in dirs]
    qd = [(qs[d] * (GLA_DK ** -0.5) * jnp.exp(cums[d])).astype(BF16) for d in dirs]
    ki = [(ks[d] * jnp.exp(-cums[d])).astype(BF16) for d in dirs]
    vb = [vs[d].astype(BF16) for d in dirs]
    row = lax.broadcasted_iota(jnp.int32, (GLA_CHUNK, GLA_CHUNK), 0)
    col = lax.broadcasted_iota(jnp.int32, (GLA_CHUNK, GLA_CHUNK), 1)
    masks = [col <= row, col >= row]
    hk = lambda h: slice(h * GLA_DK, (h + 1) * GLA_DK)
    hv = lambda h: slice(h * GLA_DV, (h + 1) * GLA_DV)
    outs = [[None] * nc, [None] * nc]
    for step in range(nc):
        chunk = [step, nc - 1 - step]
        rs = [slice(c * GLA_CHUNK, (c + 1) * GLA_CHUNK) for c in chunk]
        ends = [rs[0].stop - 1, rs[1].start]
        cl = [cums[d][ends[d]:ends[d] + 1, :] for d in dirs]
        kte = [(ks[d][rs[d]] * jnp.exp(cl[d] - cums[d][rs[d]])).astype(BF16) for d in dirs]
        st = [st_refs[d][...] for d in dirs]
        stb = [s.astype(BF16) for s in st]
        att = [[lax.dot_general(qd[d][rs[d], hk(h)], ki[d][rs[d], hk(h)], NT_DIMS,
                                preferred_element_type=F32) for h in heads] for d in dirs]
        upd = [[lax.dot_general(vb[d][rs[d], hv(h)], kte[d][:, hk(h)], TN_DIMS,
                                preferred_element_type=F32) for h in heads] for d in dirs]
        inter = [[lax.dot_general(qd[d][rs[d], hk(h)], stb[d][:, hk(h)], NT_DIMS,
                                  preferred_element_type=F32) for h in heads] for d in dirs]
        intra = [[jnp.dot(jnp.where(masks[d], att[d][h], 0.0).astype(BF16), vb[d][rs[d], hv(h)],
                          preferred_element_type=F32) for h in heads] for d in dirs]
        for d in dirs:
            st_refs[d][...] = st[d] * jnp.exp(cl[d]) + jnp.concatenate(upd[d], axis=1)
            outs[d][chunk[d]] = jnp.concatenate([intra[d][h] + inter[d][h] for h in heads], axis=1)
    return [jnp.concatenate(o, axis=0) for o in outs]


def _gla_body(qf, kf, vf, af, qb, kb, vb, ab, wa2, ba, tri, s0f, s0b,
              of, ob, sf_out, sb_out, sf_scr, sb_scr):
    @pl.when(pl.program_id(1) == 0)
    def _():
        sf_scr[...] = s0f[0]
        sb_scr[...] = s0b[0]

    o_f, o_b = _gla_pair([qf[0], qb[0]], [kf[0], kb[0]], [vf[0], vb[0]],
                         [af[0][:, :GLA_RANK], ab[0][:, GLA_RANK:2 * GLA_RANK]],
                         wa2, ba, tri, [sf_scr, sb_scr])
    of[0] = o_f
    ob[0] = o_b
    sf_out[0] = sf_scr[...]
    sb_out[0] = sb_scr[...]


def _gla(p, wa2, ba, tri, s0f, s0b, tt):
    b, s, _ = p.shape
    n = s // tt
    a_blk = (2 * GLA_K_W + 2 * GLA_V_W) // GLA_A_PAD

    def specs(rev):
        t = (lambda j: n - 1 - j) if rev else (lambda j: j)
        return [
            pl.BlockSpec((1, tt, GLA_K_W), lambda bb, j: (bb, t(j), 0)),
            pl.BlockSpec((1, tt, GLA_K_W), lambda bb, j: (bb, t(j), 1)),
            pl.BlockSpec((1, tt, GLA_V_W), lambda bb, j: (bb, t(j), 1)),
            pl.BlockSpec((1, tt, GLA_A_PAD), lambda bb, j: (bb, t(j), a_blk)),
        ]

    st_spec = pl.BlockSpec((1, GLA_DV, GLA_K_W), lambda bb, j: (bb, 0, 0))
    st_shape = jax.ShapeDtypeStruct((b, GLA_DV, GLA_K_W), F32)
    return pl.pallas_call(
        _gla_body,
        grid=(b, n),
        in_specs=specs(False) + specs(True) + [
            pl.BlockSpec((2, GLA_RANK, GLA_K_W), lambda bb, j: (0, 0, 0)),
            pl.BlockSpec((2, 1, GLA_K_W), lambda bb, j: (0, 0, 0)),
            pl.BlockSpec((2, tt, tt), lambda bb, j: (0, 0, 0)),
            st_spec, st_spec,
        ],
        out_specs=[
            pl.BlockSpec((1, tt, GLA_V_W), lambda bb, j: (bb, j, 0)),
            pl.BlockSpec((1, tt, GLA_V_W), lambda bb, j: (bb, n - 1 - j, 0)),
            st_spec, st_spec,
        ],
        out_shape=[
            jax.ShapeDtypeStruct((b, s, GLA_V_W), F32),
            jax.ShapeDtypeStruct((b, s, GLA_V_W), F32),
            st_shape, st_shape,
        ],
        scratch_shapes=[pltpu.VMEM((GLA_DV, GLA_K_W), F32), pltpu.VMEM((GLA_DV, GLA_K_W), F32)],
        compiler_params=_params("parallel", "arbitrary"),
        name="gla",
    )(p, p, p, p, p, p, p, p, wa2, ba, tri, s0f, s0b)


def _gla_tri(tt):
    t = np.arange(tt)
    same = (t[:, None] // GLA_CHUNK) == (t[None, :] // GLA_CHUNK)
    fwd = same & (t[None, :] <= t[:, None])
    bwd = same & (t[None, :] >= t[:, None])
    return jnp.asarray(np.stack([fwd, bwd]).astype(np.float32))


def _hy_filter_body(zt_ref, w1t_ref, b1_ref, fr_ref, w2t_ref, b2_ref, w3t_ref, dec_ref,
                    f_ref, b_ref):
    zt = zt_ref[...]
    fr = fr_ref[...]
    h1 = jnp.sin(fr * (jnp.dot(w1t_ref[...], zt, precision=HIGHEST, preferred_element_type=F32)
                       + b1_ref[...]))
    h2 = jnp.sin(fr * (jnp.dot(w2t_ref[...], h1, precision=HIGHEST, preferred_element_type=F32)
                       + b2_ref[...]))
    tn = zt[0:1, :]
    f = jnp.dot(w3t_ref[0, 0], h2, precision=HIGHEST, preferred_element_type=F32)
    f = f * jnp.exp(-tn * jnp.abs(dec_ref[0, 0]))
    b = jnp.dot(w3t_ref[0, 1], h2, precision=HIGHEST, preferred_element_type=F32)
    b = b * jnp.exp(-tn * jnp.abs(dec_ref[0, 1]))
    lane = lax.broadcasted_iota(jnp.int32, b.shape, 1)
    b = jnp.where(lane == 0, 0.0, b)
    den = (jnp.sum(jnp.abs(f), axis=-1, keepdims=True)
           + jnp.sum(jnp.abs(b), axis=-1, keepdims=True) + EPS)
    f_ref[0] = f / den
    b_ref[0] = b / den


def _hy_filters(n, w1, b1, fr, w2, b2, w3, dec):
    t = jnp.arange(n, dtype=F32)
    t_norm = t / n
    bands = jnp.linspace(1e-4, HY_BANDS - 1, HY_BANDS, dtype=F32)
    phase = (2 * math.pi / n) * t[:, None] * bands[None, :]
    z = jnp.concatenate([t_norm[:, None], jnp.cos(phase), -jnp.sin(phase)], axis=-1)
    zt = jnp.pad(z.T, ((0, HY_EMB_PAD - HY_EMB), (0, 0)))
    w1t = jnp.pad(w1.T, ((0, 0), (0, HY_EMB_PAD - HY_EMB)))
    w3t = w3.T.reshape(HY_ORDER, 2, HY_WIDTH, HY_FFN)
    ct = 128
    col = lambda v: v.reshape(HY_FFN, 1)
    full = lambda shape: pl.BlockSpec(shape, lambda o, c: (0,) * len(shape))
    out_spec = pl.BlockSpec((1, ct, n), lambda o, c: (o, c, 0))
    out_shape = jax.ShapeDtypeStruct((HY_ORDER, HY_WIDTH, n), F32)
    return pl.pallas_call(
        _hy_filter_body,
        grid=(HY_ORDER, HY_WIDTH // ct),
        in_specs=[
            full((HY_EMB_PAD, n)), full((HY_FFN, HY_EMB_PAD)), full((HY_FFN, 1)), full((HY_FFN, 1)),
            full((HY_FFN, HY_FFN)), full((HY_FFN, 1)),
            pl.BlockSpec((1, 2, ct, HY_FFN), lambda o, c: (o, 0, c, 0)),
            pl.BlockSpec((1, 2, ct, 1), lambda o, c: (o, 0, c, 0)),
        ],
        out_specs=[out_spec, out_spec],
        out_shape=[out_shape, out_shape],
        compiler_params=_params("parallel", "parallel"),
        name="hy_filter",
    )(zt, w1t, col(b1), col(fr), w2.T, col(b2), w3t, dec.reshape(HY_ORDER, 2, HY_WIDTH, 1))


def _dft_consts(n1):
    n = n1 * LANES
    half = n1 // 2
    g = DFT_GROUP
    rows = g * n1
    i1 = np.arange(half)
    g1k = np.zeros((2 * rows, g * half))
    for r in range(rows):
        c, k1 = divmod(r, n1)
        tau, j = divmod(r, SUBLANES)
        ang = 2 * np.pi * k1 * i1 / n1
        g1k[2 * SUBLANES * tau + j, i1 * g + c] = np.cos(ang)
        g1k[2 * SUBLANES * tau + SUBLANES + j, i1 * g + c] = -np.sin(ang)
    gik = g1k.T
    k1 = np.arange(n1)[:, None]
    i2 = np.arange(LANES)[None, :]
    at = 2 * np.pi * k1 * i2 / n
    twr = np.tile(np.cos(at), (g, 1))
    twi = np.tile(-np.sin(at), (g, 1))
    a2 = 2 * np.pi * np.arange(LANES)[:, None] * np.arange(LANES)[None, :] / LANES
    cplx = lambda fr, fi: np.block([[fr, fi], [-fi, fr]])
    f2 = cplx(np.cos(a2), -np.sin(a2))
    f2i = cplx(np.cos(a2), np.sin(a2))
    bf = lambda m: jnp.asarray(m, dtype=BF16)
    return dict(g1k=bf(g1k), twr=jnp.asarray(twr, F32), twi=jnp.asarray(twi, F32),
                f2=bf(f2), f2i=bf(f2i), gik=bf(gik))


def _split_tiles(a):
    nt = a.shape[0] // (2 * SUBLANES)
    re = jnp.concatenate([a[2 * SUBLANES * t:2 * SUBLANES * t + SUBLANES] for t in range(nt)], axis=0)
    im = jnp.concatenate([a[2 * SUBLANES * t + SUBLANES:2 * SUBLANES * (t + 1)] for t in range(nt)],
                         axis=0)
    return re, im


def _merge_tiles(re, im):
    nt = re.shape[0] // SUBLANES
    parts = []
    for t in range(nt):
        parts += [re[SUBLANES * t:SUBLANES * (t + 1)], im[SUBLANES * t:SUBLANES * (t + 1)]]
    return jnp.concatenate(parts, axis=0)


def _fwd_dft(xs, g1k, twr, twi, f2):
    half = xs[0].shape[1] // LANES
    rows = [jnp.concatenate([x[:, i * LANES:(i + 1) * LANES] for i in range(half)], axis=0)
            for x in xs]
    a = [jnp.dot(g1k, r.astype(BF16), preferred_element_type=F32) for r in rows]
    t = []
    for v in a:
        ar, ai = _split_tiles(v)
        t.append(jnp.concatenate([ar * twr - ai * twi, ar * twi + ai * twr], axis=1).astype(BF16))
    p = [jnp.dot(v, f2, preferred_element_type=F32) for v in t]
    return [(v[:, :LANES], v[:, LANES:]) for v in p]


def _inv_dft(ys, twr, twi, f2i, gik):
    p = [jnp.dot(jnp.concatenate([yr, yi], axis=1).astype(BF16), f2i, preferred_element_type=F32)
         for yr, yi in ys]
    t = []
    for v in p:
        br, bi = v[:, :LANES], v[:, LANES:]
        t.append(_merge_tiles(br * twr + bi * twi, bi * twr - br * twi).astype(BF16))
    y = [jnp.dot(gik, v, preferred_element_type=F32) for v in t]
    half = y[0].shape[0] // DFT_GROUP
    return [jnp.concatenate([v[i * DFT_GROUP:(i + 1) * DFT_GROUP] for i in range(half)], axis=1)
            for v in y]


def _hy_spec_body(f_ref, b_ref, g1k, twr, twi, f2, hr_ref, hi_ref, *, groups, n1):
    scale = 1.0 / (n1 * LANES)

    def body(g, carry):
        rows = pl.ds(pl.multiple_of(g * DFT_GROUP, DFT_GROUP), DFT_GROUP)
        (zfr, zfi), (zbr, zbi) = _fwd_dft([f_ref[rows, :], b_ref[rows, :]],
                                          g1k[...], twr[...], twi[...], f2[...])
        orow = pl.ds(pl.multiple_of(g * (DFT_GROUP * n1), DFT_GROUP * n1), DFT_GROUP * n1)
        hr_ref[orow, :] = (zfr + zbr) * scale
        hi_ref[orow, :] = (zfi - zbi) * scale
        return carry

    lax.fori_loop(0, groups, body, 0)


def _const_spec(arr, ngrid):
    nd = arr.ndim
    return pl.BlockSpec(arr.shape, lambda *_: (0,) * nd)


def _hy_spec(f, b, consts, n1):
    chans, n_half = f.shape
    ct = 64
    groups = ct // DFT_GROUP
    cs = [consts[k] for k in ("g1k", "twr", "twi", "f2")]
    in_spec = pl.BlockSpec((ct, n_half), lambda c: (c, 0))
    out_spec = pl.BlockSpec((ct * n1, LANES), lambda c: (c, 0))
    out_shape = jax.ShapeDtypeStruct((chans * n1, LANES), F32)
    return pl.pallas_call(
        functools.partial(_hy_spec_body, groups=groups, n1=n1),
        grid=(chans // ct,),
        in_specs=[in_spec, in_spec] + [_const_spec(c, 1) for c in cs],
        out_specs=[out_spec, out_spec],
        out_shape=[out_shape, out_shape],
        compiler_params=_params("parallel"),
        name="hy_spec",
    )(f, b, *cs)


def _hy_main_body(u1_ref, u2_ref, uz_ref, p_ref, hr_ref, hi_ref, g1k, twr, twi, f2, f2i, gik,
                  o_ref, *, groups, n1):
    n_half = u1_ref.shape[2]
    shape = (DFT_GROUP, n_half)
    lane = lax.broadcasted_iota(jnp.int32, shape, 1)
    t_first = lane == 0
    t_last = lane == n_half - 1

    def prev(x):
        return jnp.where(t_first, 0.0, pltpu.roll(x, 1, 1))

    def nxt(x):
        return jnp.where(t_last, 0.0, pltpu.roll(x, n_half - 1, 1))

    def body(t, carry):
        gs = [t * HY_INTERLEAVE + i for i in range(HY_INTERLEAVE)]
        cs = [pl.ds(pl.multiple_of(g * DFT_GROUP, DFT_GROUP), DFT_GROUP) for g in gs]
        hrows = [pl.ds(pl.multiple_of(g * (DFT_GROUP * n1), DFT_GROUP * n1), DFT_GROUP * n1)
                 for g in gs]
        prs = [p_ref[c, :] for c in cs]
        col = lambda i, j: prs[i][:, j:j + 1]

        def sconv(ref, i, j):
            x = ref[0, cs[i], :]
            return prev(x) * col(i, j) + x * col(i, j + 1) + nxt(x) * col(i, j + 2) + col(i, j + 3)

        idx = range(HY_INTERLEAVE)

        def conv(vs, o):
            zs = _fwd_dft(vs, g1k[...], twr[...], twi[...], f2[...])
            ys = []
            for i, (zr, zi) in enumerate(zs):
                hr = hr_ref[o, hrows[i], :]
                hi = hi_ref[o, hrows[i], :]
                ys.append((zr * hr - zi * hi, zr * hi + zi * hr))
            return _inv_dft(ys, twr[...], twi[...], f2i[...], gik[...])

        zs = [sconv(uz_ref, i, 8) for i in idx]
        cv = conv(zs, 0)
        zs = [sconv(u1_ref, i, 0) * (cv[i] + col(i, 12) * zs[i]) for i in idx]
        cv = conv(zs, 1)
        for i in idx:
            o_ref[0, cs[i], :] = sconv(u2_ref, i, 4) * (cv[i] + col(i, 13) * zs[i])
        return carry

    lax.fori_loop(0, groups // HY_INTERLEAVE, body, 0)


def _hy_main(ut, prm, hr, hi, consts, n1):
    b, _, n_half = ut.shape
    ct = 64
    nct = HY_WIDTH // ct
    cs = [consts[k] for k in ("g1k", "twr", "twi", "f2", "f2i", "gik")]
    u_spec = lambda sec: pl.BlockSpec((1, ct, n_half), lambda c, bb: (bb, sec * nct + c, 0))
    h_spec = pl.BlockSpec((HY_ORDER, ct * n1, LANES), lambda c, bb: (0, c, 0))
    return pl.pallas_call(
        functools.partial(_hy_main_body, groups=ct // DFT_GROUP, n1=n1),
        grid=(nct, b),
        in_specs=[u_spec(0), u_spec(1), u_spec(2),
                  pl.BlockSpec((ct, 16), lambda c, bb: (c, 0)),
                  h_spec, h_spec] + [_const_spec(c, 2) for c in cs],
        out_specs=pl.BlockSpec((1, ct, n_half), lambda c, bb: (bb, c, 0)),
        out_shape=jax.ShapeDtypeStruct((b, HY_WIDTH, n_half), F32),
        compiler_params=_params("parallel", "parallel"),
        name="hy_main",
    )(ut, ut, ut, prm, hr, hi, *cs)


def _hyc_consts(n):
    nn = 2 * n
    a = 2 * np.pi * np.arange(n)[:, None] * np.arange(nn)[None, :] / nn
    fc = np.concatenate([np.cos(a), -np.sin(a)], axis=1)
    fi = np.concatenate([np.cos(a.T), -np.sin(a.T)], axis=0)
    return jnp.asarray(fc, BF16), jnp.asarray(fi, BF16)


def _hyc_spec_body(f_ref, b_ref, fc_ref, hr_ref, hi_ref):
    nn = hr_ref.shape[-1]
    zf = jnp.dot(f_ref[...].astype(BF16), fc_ref[...], preferred_element_type=F32)
    zb = jnp.dot(b_ref[...].astype(BF16), fc_ref[...], preferred_element_type=F32)
    hr_ref[...] = (zf[:, :nn] + zb[:, :nn]) * (1.0 / nn)
    hi_ref[...] = (zf[:, nn:] - zb[:, nn:]) * (1.0 / nn)


def _hyc_spec(f, b, fc):
    rows, n = f.shape
    out_shape = jax.ShapeDtypeStruct((rows, 2 * n), F32)
    return pl.pallas_call(
        _hyc_spec_body,
        out_shape=[out_shape, out_shape],
        compiler_params=pltpu.CompilerParams(vmem_limit_bytes=VMEM_LIMIT_BYTES),
        name="hyc_spec",
    )(f, b, fc)


def _hyc_main_body(u1_ref, u2_ref, uz_ref, p_ref, hr_ref, hi_ref, fc_ref, fi_ref, o_ref):
    shape = u1_ref.shape[1:]
    n = shape[1]
    nn = 2 * n
    lane = lax.broadcasted_iota(jnp.int32, shape, 1)
    pr = p_ref[...]
    col = lambda j: pr[:, j:j + 1]

    def sconv(ref, j):
        x = ref[0]
        p = jnp.where(lane == 0, 0.0, pltpu.roll(x, 1, 1))
        q = jnp.where(lane == n - 1, 0.0, pltpu.roll(x, n - 1, 1))
        return p * col(j) + x * col(j + 1) + q * col(j + 2) + col(j + 3)

    def conv(v, o):
        s = jnp.dot(v.astype(BF16), fc_ref[...], preferred_element_type=F32)
        zr, zi = s[:, :nn], s[:, nn:]
        hr, hi = hr_ref[o], hi_ref[o]
        y = jnp.concatenate([zr * hr - zi * hi, zr * hi + zi * hr], axis=1).astype(BF16)
        return jnp.dot(y, fi_ref[...], preferred_element_type=F32)

    x1 = sconv(u1_ref, 0)
    x2 = sconv(u2_ref, 4)
    z = sconv(uz_ref, 8)
    z = x1 * (conv(z, 0) + col(12) * z)
    o_ref[0] = x2 * (conv(z, 1) + col(13) * z)


def _hyc_main(ut, prm, hr, hi, fc, fi, nb, n):
    ct = 128
    nct = HY_WIDTH // ct
    u_spec = lambda sec: pl.BlockSpec((1, ct, n), lambda bb, c: (0, sec * nct + c, bb))
    h_spec = pl.BlockSpec((HY_ORDER, ct, 2 * n), lambda bb, c: (0, c, 0))
    return pl.pallas_call(
        _hyc_main_body,
        grid=(nb, nct),
        in_specs=[u_spec(0), u_spec(1), u_spec(2),
                  pl.BlockSpec((ct, 16), lambda bb, c: (c, 0)),
                  h_spec, h_spec, _const_spec(fc, 2), _const_spec(fi, 2)],
        out_specs=pl.BlockSpec((1, ct, n), lambda bb, c: (0, c, bb)),
        out_shape=jax.ShapeDtypeStruct((1, HY_WIDTH, nb * n), F32),
        compiler_params=_params("parallel", "parallel"),
        name="hyc_main",
    )(ut, ut, ut, prm, hr, hi, fc, fi)


def _merge_body(x_ref, m_ref, g_ref, wbrg, yh_ref, of_ref, ob_ref, og_ref, gn_ref, ya_ref,
                wh, wg, wa, wo, o_ref):
    x = x_ref[0]
    m = m_ref[0]
    h = _norm_mod(x, g_ref[...], m[0:1], m[1:2]).astype(BF16)
    gates = jnp.dot(h, wbrg[...], preferred_element_type=F32)
    yh = yh_ref[0].T
    o = of_ref[0] + ob_ref[0]
    parts = []
    for hh in range(GLA_HEADS):
        t = o[:, hh * GLA_DV:(hh + 1) * GLA_DV]
        parts.append(t * lax.rsqrt(jnp.mean(t * t, axis=-1, keepdims=True) + EPS) * gn_ref[...])
    og = og_ref[0]
    yg = jnp.concatenate(parts, axis=1) * (og * _sigmoid(og))
    d = D_MODEL
    mm = (_sigmoid(gates[:, :d]) * jnp.dot(yh.astype(BF16), wh[...], preferred_element_type=F32)
          + _sigmoid(gates[:, d:2 * d]) * jnp.dot(yg.astype(BF16), wg[...], preferred_element_type=F32)
          + _sigmoid(gates[:, 2 * d:]) * jnp.dot(ya_ref[0].astype(BF16), wa[...],
                                                  preferred_element_type=F32))
    out = jnp.dot(mm.astype(BF16), wo[...], preferred_element_type=F32)
    o_ref[0] = x + m[2:3] * out


def _merge(x, mods, g, wbrg, yht, of, ob, pg, gn, ya, wh, wg, wa, wo, tm):
    bm, r, _ = x.shape
    row = lambda w: pl.BlockSpec((1, tm, w), lambda b, i: (b, i, 0))
    full = lambda a: pl.BlockSpec(a.shape, lambda b, i: (0,) * a.ndim)
    og_blk = (2 * GLA_K_W + GLA_V_W) // GLA_V_W
    return pl.pallas_call(
        _merge_body,
        grid=(bm, r // tm),
        in_specs=[
            row(D_MODEL),
            pl.BlockSpec((1, 6, D_MODEL), lambda b, i: (b, 0, 0)),
            full(g), full(wbrg),
            pl.BlockSpec((1, HY_WIDTH, tm), lambda b, i: (b, 0, i)),
            row(GLA_V_W), row(GLA_V_W),
            pl.BlockSpec((1, tm, GLA_V_W), lambda b, i: (b, i, og_blk)),
            full(gn), row(ATT_Q_W), full(wh), full(wg), full(wa), full(wo),
        ],
        out_specs=row(D_MODEL),
        out_shape=jax.ShapeDtypeStruct(x.shape, F32),
        compiler_params=_params("parallel", "parallel"),
        name="merge",
    )(x, mods, g, wbrg, yht, of, ob, pg, gn, ya, wh, wg, wa, wo)


def _route(logits, rb):
    scores = _sigmoid(logits)
    sel = scores + rb
    lane = lax.broadcasted_iota(jnp.int32, sel.shape, 1).astype(F32)
    neg = -jnp.inf

    def top2(v):
        m1 = jnp.max(v, axis=-1, keepdims=True)
        i1 = jnp.min(jnp.where(v == m1, lane, float(N_EXPERTS)), axis=-1, keepdims=True)
        v2 = jnp.where(lane == i1, neg, v)
        m2 = jnp.max(v2, axis=-1, keepdims=True)
        i2 = jnp.min(jnp.where(v2 == m2, lane, float(N_EXPERTS)), axis=-1, keepdims=True)
        return m1, m2, i1, i2

    group_of = jnp.floor(lane * (1.0 / EXPERTS_PER_GROUP))
    best = None
    best_g = None
    for g in range(N_GROUPS):
        m1, m2, _, _ = top2(jnp.where(group_of == float(g), sel, neg))
        gs = m1 + m2
        if best is None:
            best, best_g = gs, jnp.zeros_like(gs)
        else:
            better = gs > best
            best_g = jnp.where(better, float(g), best_g)
            best = jnp.where(better, gs, best)
    _, _, i1, i2 = top2(jnp.where(group_of == best_g, sel, neg))
    w = jnp.where((lane == i1) | (lane == i2), scores, 0.0)
    return w / jnp.sum(w, axis=-1, keepdims=True), best_g


def _moe_body(x_ref, m_ref, g_ref, rw_ref, rb_ref, wg_ref, wu_ref, wd_ref, o_ref,
              h_scr, ghi_scr, glo_scr, grp_scr, pos_scr, grpt_scr, post_scr, acc_scr):
    g = pl.program_id(2)
    gf = g.astype(F32)
    tm = x_ref.shape[1]
    sub = MOE_SUB
    lane = lax.broadcasted_iota(jnp.int32, (tm, LANES), 1).astype(F32)

    @pl.when(g == 0)
    def _():
        m = m_ref[0]
        h = _norm_mod(x_ref[0], g_ref[...], m[3:4], m[4:5])
        h_scr[...] = h.astype(BF16)
        logits = jnp.dot(h, rw_ref[...], precision=HIGHEST, preferred_element_type=F32)
        gates, grp = _route(logits, rb_ref[...])
        ghi = gates.astype(BF16)
        ghi_scr[...] = ghi
        glo_scr[...] = (gates - ghi.astype(F32)).astype(BF16)
        grp_b = jnp.broadcast_to(grp, (tm, LANES))
        onehot = (lane == grp_b).astype(BF16)
        ranks = []
        for rb_ in range(tm // sub):
            r = lax.broadcasted_iota(jnp.int32, (sub, tm), 0) + rb_ * sub
            c = lax.broadcasted_iota(jnp.int32, (sub, tm), 1)
            ranks.append(jnp.dot((c < r).astype(BF16), onehot, preferred_element_type=F32))
        rank = jnp.concatenate(ranks, axis=0)
        pos = jnp.sum(jnp.where(lane == grp_b, rank, 0.0), axis=-1, keepdims=True)
        pos_b = jnp.broadcast_to(pos, (tm, LANES))
        grp_scr[...] = grp_b
        pos_scr[...] = pos_b
        grpt_scr[...] = grp_b.T
        post_scr[...] = pos_b.T
        acc_scr[...] = jnp.zeros_like(acc_scr)

    in_group = grp_scr[...] == gf
    n_g = jnp.sum(jnp.where(in_group[:, 0:1], 1.0, 0.0)).astype(jnp.int32)
    n_sub = (n_g + (sub - 1)) // sub
    d_iota = lax.broadcasted_iota(jnp.int32, (sub, tm), 0).astype(F32)
    lane16 = lax.broadcasted_iota(jnp.int32, (sub, N_EXPERTS), 1)

    def body(k, carry):
        kf = (k * sub).astype(F32)
        p = ((grpt_scr[...] == gf) & (post_scr[...] - kf == d_iota)).astype(BF16)
        hd = jnp.dot(p, h_scr[...], preferred_element_type=F32).astype(BF16)
        gd = (jnp.dot(p, ghi_scr[...], preferred_element_type=F32)
              + jnp.dot(p, glo_scr[...], preferred_element_type=F32))
        y = jnp.zeros((sub, D_MODEL), F32)
        for j in range(EXPERTS_PER_GROUP):
            a = jnp.dot(hd, wg_ref[j], preferred_element_type=F32)
            b = jnp.dot(hd, wu_ref[j], preferred_element_type=F32)
            hid = (a * _sigmoid(a) * b).astype(BF16)
            ge = jnp.sum(jnp.where(lane16 == g * EXPERTS_PER_GROUP + j, gd, 0.0),
                         axis=-1, keepdims=True)
            y = y + ge * jnp.dot(hid, wd_ref[j], preferred_element_type=F32)
        pt = (in_group & (pos_scr[...] - kf == lane)).astype(BF16)
        acc_scr[...] += jnp.dot(pt, y.astype(BF16), preferred_element_type=F32)
        return carry

    lax.fori_loop(0, n_sub, body, 0)

    @pl.when(g == N_GROUPS - 1)
    def _():
        o_ref[0] = x_ref[0] + m_ref[0][5:6] * acc_scr[...]


def _moe(x, mods, g, rw, rb, wg, wu, wd, tm):
    bm, r, _ = x.shape
    row = pl.BlockSpec((1, tm, D_MODEL), lambda b, i, e: (b, i, 0))
    return pl.pallas_call(
        _moe_body,
        grid=(bm, r // tm, N_GROUPS),
        in_specs=[
            row,
            pl.BlockSpec((1, 6, D_MODEL), lambda b, i, e: (b, 0, 0)),
            pl.BlockSpec((1, D_MODEL), lambda b, i, e: (0, 0)),
            pl.BlockSpec((D_MODEL, N_EXPERTS), lambda b, i, e: (0, 0)),
            pl.BlockSpec((1, N_EXPERTS), lambda b, i, e: (0, 0)),
            pl.BlockSpec((EXPERTS_PER_GROUP, D_MODEL, D_EXPERT), lambda b, i, e: (e, 0, 0)),
            pl.BlockSpec((EXPERTS_PER_GROUP, D_MODEL, D_EXPERT), lambda b, i, e: (e, 0, 0)),
            pl.BlockSpec((EXPERTS_PER_GROUP, D_EXPERT, D_MODEL), lambda b, i, e: (e, 0, 0)),
        ],
        out_specs=row,
        out_shape=jax.ShapeDtypeStruct(x.shape, F32),
        scratch_shapes=[
            pltpu.VMEM((tm, D_MODEL), BF16),
            pltpu.VMEM((tm, N_EXPERTS), BF16),
            pltpu.VMEM((tm, N_EXPERTS), BF16),
            pltpu.VMEM((tm, LANES), F32),
            pltpu.VMEM((tm, LANES), F32),
            pltpu.VMEM((LANES, tm), F32),
            pltpu.VMEM((LANES, tm), F32),
            pltpu.VMEM((tm, D_MODEL), F32),
        ],
        compiler_params=_params("parallel", "parallel", "arbitrary"),
        name="moe",
    )(x, mods, g, rw, rb, wg, wu, wd)


def _rope_tables(n_tokens):
    rows = n_tokens // GRID_W
    row = jnp.broadcast_to(jnp.arange(rows)[:, None], (rows, GRID_W)).reshape(-1).astype(F32)
    col = jnp.broadcast_to(jnp.arange(GRID_W)[None, :], (rows, GRID_W)).reshape(-1).astype(F32)
    inv_freq = ROPE_THETA ** (-jnp.arange(ROPE_PAIRS_PER_AXIS, dtype=F32) / ROPE_PAIRS_PER_AXIS)
    ang = jnp.concatenate([row[:, None] * inv_freq, col[:, None] * inv_freq], axis=-1)
    cos, sin = jnp.cos(ang), jnp.sin(ang)
    return jnp.concatenate([cos, cos], axis=-1), jnp.concatenate([-sin, sin], axis=-1)


def _split_w_in(w):
    sizes = (ATT_KV_W, ATT_KV_W, GLA_K_W, GLA_V_W, 2 * GLA_RANK,
             ATT_Q_W, GLA_K_W, GLA_V_W, 3 * HY_WIDTH, 3 * D_MODEL)
    cuts = [int(v) for v in np.cumsum(sizes)[:-1]]
    a_k, a_v, g_k, g_v, g_a, a_q, g_q, g_og, hy_u, br_g = jnp.split(w, cuts, axis=1)
    perm = np.concatenate([np.arange(0, HEAD_DIM, 2), np.arange(1, HEAD_DIM, 2)])
    perm_q = np.concatenate([h * HEAD_DIM + perm for h in range(ATT_HEADS)])
    perm_k = np.concatenate([h * HEAD_DIM + perm for h in range(ATT_KV_HEADS)])
    w_att = jnp.concatenate([a_q[:, perm_q], a_k[:, perm_k], a_v], axis=1).astype(BF16)
    pad = jnp.zeros((D_MODEL, GLA_A_PAD - 2 * GLA_RANK), w.dtype)
    w_gla = jnp.concatenate([g_q, g_k, g_v, g_og, g_a, pad], axis=1).astype(BF16)
    return w_att, w_gla, hy_u.T.astype(BF16), br_g.astype(BF16), perm


def kernel(x, c, ctx, c_ctx, w_mod, b_mod, norm1_g, norm2_g, w_in, q_norm_g, k_norm_g, gla_wa2, gla_ba, gla_norm_g, hy_conv_w, hy_conv_b, hy_pos_w1, hy_pos_b1, hy_sin_freq, hy_pos_w2, hy_pos_b2, hy_pos_w3, hy_decay, hy_skip, w_br_hy, w_br_gla, w_br_att, w_out, router_w, router_b, moe_w_gate, moe_w_up, moe_w_down):
    nb, seq, d = x.shape
    nctx = ctx.shape[1]
    n1_lat = 2 * seq // LANES

    c_all = jnp.concatenate([c, c_ctx[None, :], jnp.zeros((16 - nb - 1, d), F32)], axis=0)
    mods_all = _mods(c_all, w_mod, b_mod)

    rope_tabs = _rope_tables(seq)
    consts = _dft_consts(n1_lat)
    fc, fi = _hyc_consts(nctx)
    tri_lat = _gla_tri(256)
    zero_state = jnp.zeros((nb, GLA_DV, GLA_K_W), F32)
    rb = router_b.reshape(1, N_EXPERTS)

    xc = ctx.reshape(1, nb * nctx, d)
    for l in range(DEPTH):
        last = l == DEPTH - 1
        mods_lat = mods_all[l, :nb].reshape(nb, 6, d)
        mods_ctx = mods_all[l, nb:nb + 1].reshape(1, 6, d)
        g1 = norm1_g[l].reshape(1, d)
        g2 = norm2_g[l].reshape(1, d)
        w_att, w_gla, w_hyt, w_brg, perm = _split_w_in(w_in[l])
        qg = q_norm_g[l][perm].reshape(1, HEAD_DIM)
        kg = k_norm_g[l][perm].reshape(1, HEAD_DIM)
        wa2 = gla_wa2[l]
        ba = gla_ba[l].reshape(2, 1, GLA_K_W)
        gn = gla_norm_g[l].reshape(1, GLA_DV)
        wh, wg, wa, wo = (w_br_hy[l].astype(BF16), w_br_gla[l].astype(BF16),
                          w_br_att[l].astype(BF16), w_out[l].astype(BF16))
        cw, cb = hy_conv_w[l], hy_conv_b[l]
        sec = lambda s: [cw[0, s], cw[1, s], cw[2, s], cb[s]]
        w_ = HY_WIDTH
        prm = jnp.stack(sec(slice(0, w_)) + sec(slice(w_, 2 * w_)) + sec(slice(2 * w_, 3 * w_))
                        + [hy_skip[l, 0], hy_skip[l, 1], jnp.zeros((w_,), F32), jnp.zeros((w_,), F32)],
                        axis=1)
        filt_args = (hy_pos_w1[l], hy_pos_b1[l], hy_sin_freq[l], hy_pos_w2[l], hy_pos_b2[l],
                     hy_pos_w3[l], hy_decay[l])

        q, k, v = _proj_att(x, mods_lat, g1, w_att, qg, kg, rope_tabs, 512)
        qc, kc, vc = _proj_att(xc, mods_ctx, g1, w_att, qg, kg, None, 256)
        kc = kc.reshape(nb, nctx, ATT_KV_W)
        vc = vc.reshape(nb, nctx, ATT_KV_W)
        pg = _proj_plain(x, mods_lat, g1, w_gla, 512)
        pgc = _proj_plain(xc, mods_ctx, g1, w_gla, 256)
        ut = _proj_t(x, mods_lat, g1, w_hyt, 512)

        y_att = _attn(q, k, v, kc, vc, 256)

        cof, cob, s_f, s_b = _gla(pgc.reshape(nb, nctx, GLA_COLS), wa2, ba, _gla_tri(nctx),
                                  zero_state, zero_state, nctx)
        of, ob, _, _ = _gla(pg, wa2, ba, tri_lat, s_f, s_b, 256)

        ff, fb = _hy_filters(seq, *filt_args)
        hr, hi = _hy_spec(ff.reshape(-1, seq), fb.reshape(-1, seq), consts, n1_lat)
        hr = hr.reshape(HY_ORDER, HY_WIDTH * n1_lat, LANES)
        hi = hi.reshape(HY_ORDER, HY_WIDTH * n1_lat, LANES)
        y_hyt = _hy_main(ut, prm, hr, hi, consts, n1_lat)

        x = _merge(x, mods_lat, g1, w_brg, y_hyt, of, ob, pg, gn, y_att, wh, wg, wa, wo, 256)
        moe_w = (moe_w_gate[l].astype(BF16), moe_w_up[l].astype(BF16), moe_w_down[l].astype(BF16))
        x = _moe(x, mods_lat, g2, router_w, rb, *moe_w, 1024)

        if not last:
            yc_att = _attn(qc.reshape(nb, nctx, ATT_Q_W), None, None, kc, vc, 128)
            uct = _proj_t(xc, mods_ctx, g1, w_hyt, 256)
            cff, cfb = _hy_filters(nctx, *filt_args)
            chr_, chi = _hyc_spec(cff.reshape(-1, nctx), cfb.reshape(-1, nctx), fc)
            chr_ = chr_.reshape(HY_ORDER, HY_WIDTH, 2 * nctx)
            chi = chi.reshape(HY_ORDER, HY_WIDTH, 2 * nctx)
            yc_hyt = _hyc_main(uct, prm, chr_, chi, fc, fi, nb, nctx)
            xc = _merge(xc, mods_ctx, g1, w_brg, yc_hyt,
                        cof.reshape(1, nb * nctx, GLA_V_W), cob.reshape(1, nb * nctx, GLA_V_W),
                        pgc, gn, yc_att.reshape(1, nb * nctx, ATT_Q_W), wh, wg, wa, wo, 256)
            xc = _moe(xc, mods_ctx, g2, router_w, rb, *moe_w, min(1024, nb * nctx))
    return x
```

```python
import functools
import math

import numpy as np
import jax
import jax.numpy as jnp
from jax import lax
from jax.experimental import pallas as pl
from jax.experimental.pallas import tpu as pltpu

F32 = jnp.float32
BF16 = jnp.bfloat16
HIGHEST = lax.Precision.HIGHEST

D_MODEL = 1024
DEPTH = 2
GRID_W = 64
EPS = 1e-6

ATT_HEADS = 8
ATT_KV_HEADS = 2
ATT_GROUP = ATT_HEADS // ATT_KV_HEADS
HEAD_DIM = 128
ROPE_PAIRS_PER_AXIS = HEAD_DIM // 4
ROPE_THETA = 10000.0
Q_SCALE = HEAD_DIM ** -0.5 * math.log2(math.e)

GLA_HEADS = 4
GLA_DK = 64
GLA_DV = 128
GLA_RANK = 16
GLA_TAU = 16.0
GLA_CHUNK = 64

HY_WIDTH = 512
HY_ORDER = 2
HY_BANDS = 16
HY_EMB = 1 + 2 * HY_BANDS
HY_EMB_PAD = 40
HY_FFN = 64

N_EXPERTS = 16
N_GROUPS = 4
EXPERTS_PER_GROUP = N_EXPERTS // N_GROUPS
D_EXPERT = 512

ATT_Q_W = ATT_HEADS * HEAD_DIM
ATT_KV_W = ATT_KV_HEADS * HEAD_DIM
GLA_K_W = GLA_HEADS * GLA_DK
GLA_V_W = GLA_HEADS * GLA_DV
GLA_A_PAD = 128
GLA_COLS = 2 * GLA_K_W + 2 * GLA_V_W + GLA_A_PAD
ATT_COLS = ATT_Q_W + 2 * ATT_KV_W

LANES = 128
SUBLANES = 8
MOE_SUB = LANES
VMEM_LIMIT_BYTES = 60 * 1024 * 1024

DFT_GROUP = SUBLANES
HY_INTERLEAVE = 4

NT_DIMS = (((1,), (1,)), ((), ()))
TN_DIMS = (((0,), (0,)), ((), ()))


def _params(*sem):
    return pltpu.CompilerParams(dimension_semantics=sem, vmem_limit_bytes=VMEM_LIMIT_BYTES)


def _sigmoid(x):
    return 1.0 / (1.0 + jnp.exp(-x))


def _norm_mod(x, g, shift, scale):
    ms = jnp.mean(x * x, axis=-1, keepdims=True)
    return (x * lax.rsqrt(ms + EPS) * g) * (1.0 + scale) + shift


def _mod_body(c_ref, w_ref, b_ref, o_ref):
    c = c_ref[...]
    s = c * _sigmoid(c)
    o_ref[0] = jnp.dot(s, w_ref[0], precision=HIGHEST, preferred_element_type=F32) + b_ref[0]


def _mods(c_all, w_mod, b_mod):
    tn = 512
    rows = c_all.shape[0]
    return pl.pallas_call(
        _mod_body,
        grid=(DEPTH, 6 * D_MODEL // tn),
        in_specs=[
            pl.BlockSpec((rows, D_MODEL), lambda l, j: (0, 0)),
            pl.BlockSpec((1, D_MODEL, tn), lambda l, j: (l, 0, j)),
            pl.BlockSpec((1, 1, tn), lambda l, j: (l, 0, j)),
        ],
        out_specs=pl.BlockSpec((1, rows, tn), lambda l, j: (l, 0, j)),
        out_shape=jax.ShapeDtypeStruct((DEPTH, rows, 6 * D_MODEL), F32),
        compiler_params=_params("parallel", "parallel"),
        name="mods",
    )(c_all, w_mod, b_mod.reshape(DEPTH, 1, 6 * D_MODEL))


def _proj_att_body(*refs, rope):
    if rope:
        (x_ref, m_ref, g_ref, wqv_ref, wk_ref, qg_ref, kg_ref, cos_ref, sin_ref, cost_ref, sint_ref,
         qt_ref, k_ref, vt_ref) = refs
    else:
        x_ref, m_ref, g_ref, wqv_ref, wk_ref, qg_ref, kg_ref, qt_ref, k_ref, vt_ref = refs
    m = m_ref[0]
    h = _norm_mod(x_ref[0], g_ref[...], m[0:1], m[1:2]).astype(BF16)
    pt = lax.dot_general(wqv_ref[...], h, NT_DIMS, preferred_element_type=F32)
    pk = jnp.dot(h, wk_ref[...], preferred_element_type=F32)
    half = HEAD_DIM // 2

    for i in range(ATT_HEADS):
        sl = slice(i * HEAD_DIM, (i + 1) * HEAD_DIM)
        t = pt[sl]
        t = t * lax.rsqrt(jnp.mean(t * t, axis=0, keepdims=True) + EPS) * qg_ref[...]
        if rope:
            t = t * cost_ref[...] + pltpu.roll(t, half, 0) * sint_ref[...]
        qt_ref[0, sl, :] = t.astype(BF16)
    for i in range(ATT_KV_HEADS):
        sl = slice(i * HEAD_DIM, (i + 1) * HEAD_DIM)
        t = pk[:, sl]
        t = t * lax.rsqrt(jnp.mean(t * t, axis=-1, keepdims=True) + EPS) * kg_ref[...]
        if rope:
            t = t * cos_ref[...] + pltpu.roll(t, half, 1) * sin_ref[...]
        k_ref[0, :, sl] = t.astype(BF16)
    vt_ref[0] = pt[ATT_Q_W:].astype(BF16)


def _proj_att(x, mods, g, wqv_t, wk, qg_col, kg, rope_tabs, tm):
    bm, r, _ = x.shape
    rope = rope_tabs is not None
    in_specs = [
        pl.BlockSpec((1, tm, D_MODEL), lambda b, i: (b, i, 0)),
        pl.BlockSpec((1, 6, D_MODEL), lambda b, i: (b, 0, 0)),
        pl.BlockSpec((1, D_MODEL), lambda b, i: (0, 0)),
        pl.BlockSpec((ATT_Q_W + ATT_KV_W, D_MODEL), lambda b, i: (0, 0)),
        pl.BlockSpec((D_MODEL, ATT_KV_W), lambda b, i: (0, 0)),
        pl.BlockSpec((HEAD_DIM, 1), lambda b, i: (0, 0)),
        pl.BlockSpec((1, HEAD_DIM), lambda b, i: (0, 0)),
    ]
    args = [x, mods, g, wqv_t, wk, qg_col, kg]
    if rope:
        in_specs += [pl.BlockSpec((tm, HEAD_DIM), lambda b, i: (i, 0))] * 2
        in_specs += [pl.BlockSpec((HEAD_DIM, tm), lambda b, i: (0, i))] * 2
        args += list(rope_tabs)
    return pl.pallas_call(
        functools.partial(_proj_att_body, rope=rope),
        grid=(bm, r // tm),
        in_specs=in_specs,
        out_specs=[
            pl.BlockSpec((1, ATT_Q_W, tm), lambda b, i: (b, 0, i)),
            pl.BlockSpec((1, tm, ATT_KV_W), lambda b, i: (b, i, 0)),
            pl.BlockSpec((1, ATT_KV_W, tm), lambda b, i: (b, 0, i)),
        ],
        out_shape=[
            jax.ShapeDtypeStruct((bm, ATT_Q_W, r), BF16),
            jax.ShapeDtypeStruct((bm, r, ATT_KV_W), BF16),
            jax.ShapeDtypeStruct((bm, ATT_KV_W, r), BF16),
        ],
        compiler_params=_params("parallel", "parallel"),
        name="proj_att",
    )(*args)


def _proj_plain_body(x_ref, m_ref, g_ref, w_ref, o_ref):
    m = m_ref[0]
    h = _norm_mod(x_ref[0], g_ref[...], m[0:1], m[1:2]).astype(BF16)
    o_ref[0] = jnp.dot(h, w_ref[...], preferred_element_type=F32)


def _proj_plain(x, mods, g, w, tm):
    bm, r, _ = x.shape
    n = w.shape[1]
    return pl.pallas_call(
        _proj_plain_body,
        grid=(bm, r // tm),
        in_specs=[
            pl.BlockSpec((1, tm, D_MODEL), lambda b, i: (b, i, 0)),
            pl.BlockSpec((1, 6, D_MODEL), lambda b, i: (b, 0, 0)),
            pl.BlockSpec((1, D_MODEL), lambda b, i: (0, 0)),
            pl.BlockSpec((D_MODEL, n), lambda b, i: (0, 0)),
        ],
        out_specs=pl.BlockSpec((1, tm, n), lambda b, i: (b, i, 0)),
        out_shape=jax.ShapeDtypeStruct((bm, r, n), F32),
        compiler_params=_params("parallel", "parallel"),
        name="proj_gla",
    )(x, mods, g, w)


def _proj_t_body(x_ref, m_ref, g_ref, wt_ref, o_ref):
    m = m_ref[0]
    h = _norm_mod(x_ref[0], g_ref[...], m[0:1], m[1:2]).astype(BF16)
    o_ref[0] = lax.dot_general(wt_ref[...], h, NT_DIMS, preferred_element_type=F32)


def _proj_t(x, mods, g, wt, tm):
    bm, r, _ = x.shape
    n = wt.shape[0]
    return pl.pallas_call(
        _proj_t_body,
        grid=(bm, r // tm),
        in_specs=[
            pl.BlockSpec((1, tm, D_MODEL), lambda b, i: (b, i, 0)),
            pl.BlockSpec((1, 6, D_MODEL), lambda b, i: (b, 0, 0)),
            pl.BlockSpec((1, D_MODEL), lambda b, i: (0, 0)),
            pl.BlockSpec((n, D_MODEL), lambda b, i: (0, 0)),
        ],
        out_specs=pl.BlockSpec((1, n, tm), lambda b, i: (b, 0, i)),
        out_shape=jax.ShapeDtypeStruct((bm, n, r), F32),
        compiler_params=_params("parallel", "parallel"),
        name="proj_hy",
    )(x, mods, g, wt)


def _attn_body(*refs, has_lat, tk):
    if has_lat:
        qt_ref, kl_ref, vlt_ref, kc_ref, vct_ref, o_ref = refs
    else:
        qt_ref, kc_ref, vct_ref, o_ref = refs
    qt = qt_ref[0]
    tq = qt.shape[1]
    q4t = jnp.concatenate([qt[g * HEAD_DIM:(g + 1) * HEAD_DIM, :] for g in range(ATT_GROUP)], axis=1)

    def update(k_c, vt_c, state):
        s = jnp.dot(k_c, q4t, preferred_element_type=F32)
        mc = jnp.max(s, axis=0, keepdims=True)
        if state is None:
            p = jnp.exp2(s - mc)
            return mc, jnp.sum(p, axis=0, keepdims=True), jnp.dot(vt_c, p.astype(BF16),
                                                                   preferred_element_type=F32)
        m, den, acc = state
        m_new = jnp.maximum(m, mc)
        alpha = jnp.exp2(m - m_new)
        p = jnp.exp2(s - m_new)
        den = alpha * den + jnp.sum(p, axis=0, keepdims=True)
        acc = alpha * acc + jnp.dot(vt_c, p.astype(BF16), preferred_element_type=F32)
        return m_new, den, acc

    state = update(kc_ref[0], vct_ref[0], None)
    if has_lat:
        for j in range(kl_ref.shape[1] // tk):
            state = update(kl_ref[0, j * tk:(j + 1) * tk, :], vlt_ref[0, :, j * tk:(j + 1) * tk], state)
    _, den, acc = state
    o = acc / den
    for g in range(ATT_GROUP):
        o_ref[0, g * HEAD_DIM:(g + 1) * HEAD_DIM, :] = o[:, g * tq:(g + 1) * tq]


def _attn(qt, k_lat, vt_lat, k_ctx, vt_ctx, nb, tq, tk=4096):
    has_lat = k_lat is not None
    gw = ATT_GROUP * HEAD_DIM
    sc = k_ctx.shape[1] // nb
    if has_lat:
        sq = qt.shape[2]
        q_map = lambda bb, h, i: (bb, h, i)
    else:
        sq = tq
        q_map = lambda bb, h, i: (0, h, bb)
    in_specs = [pl.BlockSpec((1, gw, tq), q_map)]
    args = [qt]
    if has_lat:
        sk = k_lat.shape[1]
        in_specs += [pl.BlockSpec((1, sk, HEAD_DIM), lambda bb, h, i: (bb, 0, h)),
                     pl.BlockSpec((1, HEAD_DIM, sk), lambda bb, h, i: (bb, h, 0))]
        args += [k_lat, vt_lat]
    in_specs += [pl.BlockSpec((1, sc, HEAD_DIM), lambda bb, h, i: (0, bb, h)),
                 pl.BlockSpec((1, HEAD_DIM, sc), lambda bb, h, i: (0, h, bb))]
    args += [k_ctx, vt_ctx]
    return pl.pallas_call(
        functools.partial(_attn_body, has_lat=has_lat, tk=tk),
        grid=(nb, ATT_KV_HEADS, sq // tq),
        in_specs=in_specs,
        out_specs=pl.BlockSpec((1, gw, tq), q_map),
        out_shape=jax.ShapeDtypeStruct(qt.shape, F32),
        compiler_params=_params("parallel", "parallel", "parallel"),
        name="attn",
    )(*args)


def _gla_pair(qs, ks, vs, a_s, wa2, ba, tri, st_refs):
    tt = qs[0].shape[0]
    nc = tt // GLA_CHUNK
    dirs = range(2)
    heads = range(GLA_HEADS)
    zs = [jnp.dot(a_s[d], wa2[d], precision=HIGHEST, preferred_element_type=F32) + ba[d] for d in dirs]
    las = [(jnp.minimum(z, 0.0) - jnp.log(1.0 + jnp.exp(-jnp.abs(z)))) * (1.0 / GLA_TAU) for z in zs]
    cums = [jnp.dot(tri[d], las[d], precision=HIGHEST, preferred_element_type=F32) for d in dirs]
    qd = [(qs[d] * (GLA_DK ** -0.5) * jnp.exp(cums[d])).astype(BF16) for d in dirs]
    ki = [(ks[d] * jnp.exp(-cums[d])).astype(BF16) for d in dirs]
    vb = [vs[d].astype(BF16) for d in dirs]
    row = lax.broadcasted_iota(jnp.int32, (GLA_CHUNK, GLA_CHUNK), 0)
    col = lax.broadcasted_iota(jnp.int32, (GLA_CHUNK, GLA_CHUNK), 1)
    masks = [col <= row, col >= row]
    hk = lambda h: slice(h * GLA_DK, (h + 1) * GLA_DK)
    hv = lambda h: slice(h * GLA_DV, (h + 1) * GLA_DV)
    outs = [[None] * nc, [None] * nc]
    for step in range(nc):
        chunk = [step, nc - 1 - step]
        rs = [slice(c * GLA_CHUNK, (c + 1) * GLA_CHUNK) for c in chunk]
        ends = [rs[0].stop - 1, rs[1].start]
        cl = [cums[d][ends[d]:ends[d] + 1, :] for d in dirs]
        kte = [(ks[d][rs[d]] * jnp.exp(cl[d] - cums[d][rs[d]])).astype(BF16) for d in dirs]
        st = [st_refs[d][...] for d in dirs]
        stb = [s.astype(BF16) for s in st]
        att = [[lax.dot_general(qd[d][rs[d], hk(h)], ki[d][rs[d], hk(h)], NT_DIMS,
                                preferred_element_type=F32) for h in heads] for d in dirs]
        upd = [[lax.dot_general(vb[d][rs[d], hv(h)], kte[d][:, hk(h)], TN_DIMS,
                                preferred_element_type=F32) for h in heads] for d in dirs]
        inter = [[lax.dot_general(qd[d][rs[d], hk(h)], stb[d][:, hk(h)], NT_DIMS,
                                  preferred_element_type=F32) for h in heads] for d in dirs]
        intra = [[jnp.dot(jnp.where(masks[d], att[d][h], 0.0).astype(BF16), vb[d][rs[d], hv(h)],
                          preferred_element_type=F32) for h in heads] for d in dirs]
        for d in dirs:
            st_refs[d][...] = st[d] * jnp.exp(cl[d]) + jnp.concatenate(upd[d], axis=1)
            outs[d][chunk[d]] = jnp.concatenate([intra[d][h] + inter[d][h] for h in heads], axis=1)
    return [jnp.concatenate(o, axis=0) for o in outs]


def _gla_body(qf, kf, vf, af, qb, kb, vb, ab, wa2, ba, tri, s0f, s0b,
              of, ob, sf_out, sb_out, sf_scr, sb_scr):
    @pl.when(pl.program_id(1) == 0)
    def _():
        sf_scr[...] = s0f[0]
        sb_scr[...] = s0b[0]

    o_f, o_b = _gla_pair([qf[0], qb[0]], [kf[0], kb[0]], [vf[0], vb[0]],
                         [af[0][:, :GLA_RANK], ab[0][:, GLA_RANK:2 * GLA_RANK]],
                         wa2, ba, tri, [sf_scr, sb_scr])
    of[0] = o_f
    ob[0] = o_b
    sf_out[0] = sf_scr[...]
    sb_out[0] = sb_scr[...]


def _gla(p, wa2, ba, tri, s0f, s0b, tt):
    b, s, _ = p.shape
    n = s // tt
    a_blk = (2 * GLA_K_W + 2 * GLA_V_W) // GLA_A_PAD

    def specs(rev):
        t = (lambda j: n - 1 - j) if rev else (lambda j: j)
        return [
            pl.BlockSpec((1, tt, GLA_K_W), lambda bb, j: (bb, t(j), 0)),
            pl.BlockSpec((1, tt, GLA_K_W), lambda bb, j: (bb, t(j), 1)),
            pl.BlockSpec((1, tt, GLA_V_W), lambda bb, j: (bb, t(j), 1)),
            pl.BlockSpec((1, tt, GLA_A_PAD), lambda bb, j: (bb, t(j), a_blk)),
        ]

    st_spec = pl.BlockSpec((1, GLA_DV, GLA_K_W), lambda bb, j: (bb, 0, 0))
    st_shape = jax.ShapeDtypeStruct((b, GLA_DV, GLA_K_W), F32)
    return pl.pallas_call(
        _gla_body,
        grid=(b, n),
        in_specs=specs(False) + specs(True) + [
            pl.BlockSpec((2, GLA_RANK, GLA_K_W), lambda bb, j: (0, 0, 0)),
            pl.BlockSpec((2, 1, GLA_K_W), lambda bb, j: (0, 0, 0)),
            pl.BlockSpec((2, tt, tt), lambda bb, j: (0, 0, 0)),
            st_spec, st_spec,
        ],
        out_specs=[
            pl.BlockSpec((1, tt, GLA_V_W), lambda bb, j: (bb, j, 0)),
            pl.BlockSpec((1, tt, GLA_V_W), lambda bb, j: (bb, n - 1 - j, 0)),
            st_spec, st_spec,
        ],
        out_shape=[
            jax.ShapeDtypeStruct((b, s, GLA_V_W), F32),
            jax.ShapeDtypeStruct((b, s, GLA_V_W), F32),
            st_shape, st_shape,
        ],
        scratch_shapes=[pltpu.VMEM((GLA_DV, GLA_K_W), F32), pltpu.VMEM((GLA_DV, GLA_K_W), F32)],
        compiler_params=_params("parallel", "arbitrary"),
        name="gla",
    )(p, p, p, p, p, p, p, p, wa2, ba, tri, s0f, s0b)


def _gla_tri(tt):
    t = np.arange(tt)
    same = (t[:, None] // GLA_CHUNK) == (t[None, :] // GLA_CHUNK)
    fwd = same & (t[None, :] <= t[:, None])
    bwd = same & (t[None, :] >= t[:, None])
    return jnp.asarray(np.stack([fwd, bwd]).astype(np.float32))


def _hy_filter_body(zt_ref, w1t_ref, b1_ref, fr_ref, w2t_ref, b2_ref, w3t_ref, dec_ref,
                    f_ref, b_ref):
    zt = zt_ref[...]
    fr = fr_ref[...]
    h1 = jnp.sin(fr * (jnp.dot(w1t_ref[...], zt, precision=HIGHEST, preferred_element_type=F32)
                       + b1_ref[...]))
    h2 = jnp.sin(fr * (jnp.dot(w2t_ref[...], h1, precision=HIGHEST, preferred_element_type=F32)
                       + b2_ref[...]))
    tn = zt[0:1, :]
    f = jnp.dot(w3t_ref[0, 0], h2, precision=HIGHEST, preferred_element_type=F32)
    f = f * jnp.exp(-tn * jnp.abs(dec_ref[0, 0]))
    b = jnp.dot(w3t_ref[0, 1], h2, precision=HIGHEST, preferred_element_type=F32)
    b = b * jnp.exp(-tn * jnp.abs(dec_ref[0, 1]))
    lane = lax.broadcasted_iota(jnp.int32, b.shape, 1)
    b = jnp.where(lane == 0, 0.0, b)
    den = (jnp.sum(jnp.abs(f), axis=-1, keepdims=True)
           + jnp.sum(jnp.abs(b), axis=-1, keepdims=True) + EPS)
    f_ref[0] = f / den
    b_ref[0] = b / den


def _hy_filters(n, w1, b1, fr, w2, b2, w3, dec):
    t = jnp.arange(n, dtype=F32)
    t_norm = t / n
    bands = jnp.linspace(1e-4, HY_BANDS - 1, HY_BANDS, dtype=F32)
    phase = (2 * math.pi / n) * t[:, None] * bands[None, :]
    z = jnp.concatenate([t_norm[:, None], jnp.cos(phase), -jnp.sin(phase)], axis=-1)
    zt = jnp.pad(z.T, ((0, HY_EMB_PAD - HY_EMB), (0, 0)))
    w1t = jnp.pad(w1.T, ((0, 0), (0, HY_EMB_PAD - HY_EMB)))
    w3t = w3.T.reshape(HY_ORDER, 2, HY_WIDTH, HY_FFN)
    ct = 128
    col = lambda v: v.reshape(HY_FFN, 1)
    full = lambda shape: pl.BlockSpec(shape, lambda o, c: (0,) * len(shape))
    out_spec = pl.BlockSpec((1, ct, n), lambda o, c: (o, c, 0))
    out_shape = jax.ShapeDtypeStruct((HY_ORDER, HY_WIDTH, n), F32)
    return pl.pallas_call(
        _hy_filter_body,
        grid=(HY_ORDER, HY_WIDTH // ct),
        in_specs=[
            full((HY_EMB_PAD, n)), full((HY_FFN, HY_EMB_PAD)), full((HY_FFN, 1)), full((HY_FFN, 1)),
            full((HY_FFN, HY_FFN)), full((HY_FFN, 1)),
            pl.BlockSpec((1, 2, ct, HY_FFN), lambda o, c: (o, 0, c, 0)),
            pl.BlockSpec((1, 2, ct, 1), lambda o, c: (o, 0, c, 0)),
        ],
        out_specs=[out_spec, out_spec],
        out_shape=[out_shape, out_shape],
        compiler_params=_params("parallel", "parallel"),
        name="hy_filter",
    )(zt, w1t, col(b1), col(fr), w2.T, col(b2), w3t, dec.reshape(HY_ORDER, 2, HY_WIDTH, 1))


def _dft_consts(n1):
    n = n1 * LANES
    half = n1 // 2
    g = DFT_GROUP
    rows = g * n1
    i1 = np.arange(half)
    g1k = np.zeros((2 * rows, g * half))
    for r in range(rows):
        c, k1 = divmod(r, n1)
        tau, j = divmod(r, SUBLANES)
        ang = 2 * np.pi * k1 * i1 / n1
        g1k[2 * SUBLANES * tau + j, i1 * g + c] = np.cos(ang)
        g1k[2 * SUBLANES * tau + SUBLANES + j, i1 * g + c] = -np.sin(ang)
    gik = g1k.T
    k1 = np.arange(n1)[:, None]
    i2 = np.arange(LANES)[None, :]
    at = 2 * np.pi * k1 * i2 / n
    twr = np.tile(np.cos(at), (g, 1))
    twi = np.tile(-np.sin(at), (g, 1))
    a2 = 2 * np.pi * np.arange(LANES)[:, None] * np.arange(LANES)[None, :] / LANES
    cplx = lambda fr, fi: np.block([[fr, fi], [-fi, fr]])
    f2 = cplx(np.cos(a2), -np.sin(a2))
    f2i = cplx(np.cos(a2), np.sin(a2))
    bf = lambda m: jnp.asarray(m, dtype=BF16)
    return dict(g1k=bf(g1k), twr=jnp.asarray(twr, F32), twi=jnp.asarray(twi, F32),
                f2=bf(f2), f2i=bf(f2i), gik=bf(gik))


def _split_tiles(a):
    nt = a.shape[0] // (2 * SUBLANES)
    re = jnp.concatenate([a[2 * SUBLANES * t:2 * SUBLANES * t + SUBLANES] for t in range(nt)], axis=0)
    im = jnp.concatenate([a[2 * SUBLANES * t + SUBLANES:2 * SUBLANES * (t + 1)] for t in range(nt)],
                         axis=0)
    return re, im


def _merge_tiles(re, im):
    nt = re.shape[0] // SUBLANES
    parts = []
    for t in range(nt):
        parts += [re[SUBLANES * t:SUBLANES * (t + 1)], im[SUBLANES * t:SUBLANES * (t + 1)]]
    return jnp.concatenate(parts, axis=0)


def _dot_lane_pairs(m, xs):
    out = []
    for i in range(0, len(xs) - 1, 2):
        r = jnp.dot(m, jnp.concatenate([xs[i], xs[i + 1]], axis=1), preferred_element_type=F32)
        out += [r[:, :LANES], r[:, LANES:]]
    if len(xs) % 2:
        out.append(jnp.dot(m, xs[-1], preferred_element_type=F32))
    return out


def _fwd_dft(xs, g1k, twr, twi, f2):
    half = xs[0].shape[1] // LANES
    rows = [jnp.concatenate([x[:, i * LANES:(i + 1) * LANES] for i in range(half)], axis=0)
            for x in xs]
    a = _dot_lane_pairs(g1k, [r.astype(BF16) for r in rows])
    t = []
    for v in a:
        ar, ai = _split_tiles(v)
        t.append(jnp.concatenate([ar * twr - ai * twi, ar * twi + ai * twr], axis=1).astype(BF16))
    p = [jnp.dot(v, f2, preferred_element_type=F32) for v in t]
    return [(v[:, :LANES], v[:, LANES:]) for v in p]


def _inv_dft(ys, twr, twi, f2i, gik):
    p = [jnp.dot(jnp.concatenate([yr, yi], axis=1).astype(BF16), f2i, preferred_element_type=F32)
         for yr, yi in ys]
    t = []
    for v in p:
        br, bi = v[:, :LANES], v[:, LANES:]
        t.append(_merge_tiles(br * twr + bi * twi, bi * twr - br * twi).astype(BF16))
    y = _dot_lane_pairs(gik, t)
    half = y[0].shape[0] // DFT_GROUP
    return [jnp.concatenate([v[i * DFT_GROUP:(i + 1) * DFT_GROUP] for i in range(half)], axis=1)
            for v in y]


def _hy_spec_body(f_ref, b_ref, g1k, twr, twi, f2, hr_ref, hi_ref, *, groups, n1):
    scale = 1.0 / (n1 * LANES)

    def body(g, carry):
        rows = pl.ds(pl.multiple_of(g * DFT_GROUP, DFT_GROUP), DFT_GROUP)
        (zfr, zfi), (zbr, zbi) = _fwd_dft([f_ref[rows, :], b_ref[rows, :]],
                                          g1k[...], twr[...], twi[...], f2[...])
        orow = pl.ds(pl.multiple_of(g * (DFT_GROUP * n1), DFT_GROUP * n1), DFT_GROUP * n1)
        hr_ref[orow, :] = (zfr + zbr) * scale
        hi_ref[orow, :] = (zfi - zbi) * scale
        return carry

    lax.fori_loop(0, groups, body, 0)


def _const_spec(arr, ngrid):
    nd = arr.ndim
    return pl.BlockSpec(arr.shape, lambda *_: (0,) * nd)


def _hy_spec(f, b, consts, n1):
    chans, n_half = f.shape
    ct = 64
    groups = ct // DFT_GROUP
    cs = [consts[k] for k in ("g1k", "twr", "twi", "f2")]
    in_spec = pl.BlockSpec((ct, n_half), lambda c: (c, 0))
    out_spec = pl.BlockSpec((ct * n1, LANES), lambda c: (c, 0))
    out_shape = jax.ShapeDtypeStruct((chans * n1, LANES), F32)
    return pl.pallas_call(
        functools.partial(_hy_spec_body, groups=groups, n1=n1),
        grid=(chans // ct,),
        in_specs=[in_spec, in_spec] + [_const_spec(c, 1) for c in cs],
        out_specs=[out_spec, out_spec],
        out_shape=[out_shape, out_shape],
        compiler_params=_params("parallel"),
        name="hy_spec",
    )(f, b, *cs)


def _hy_main_body(u1_ref, u2_ref, uz_ref, p_ref, hr_ref, hi_ref, g1k, twr, twi, f2, f2i, gik,
                  o_ref, *, groups, n1):
    n_half = u1_ref.shape[2]
    shape = (DFT_GROUP, n_half)
    lane = lax.broadcasted_iota(jnp.int32, shape, 1)
    t_first = lane == 0
    t_last = lane == n_half - 1

    def prev(x):
        return jnp.where(t_first, 0.0, pltpu.roll(x, 1, 1))

    def nxt(x):
        return jnp.where(t_last, 0.0, pltpu.roll(x, n_half - 1, 1))

    def body(t, carry):
        gs = [t * HY_INTERLEAVE + i for i in range(HY_INTERLEAVE)]
        cs = [pl.ds(pl.multiple_of(g * DFT_GROUP, DFT_GROUP), DFT_GROUP) for g in gs]
        hrows = [pl.ds(pl.multiple_of(g * (DFT_GROUP * n1), DFT_GROUP * n1), DFT_GROUP * n1)
                 for g in gs]
        prs = [p_ref[c, :] for c in cs]
        col = lambda i, j: prs[i][:, j:j + 1]

        def sconv(ref, i, j):
            x = ref[0, cs[i], :]
            return prev(x) * col(i, j) + x * col(i, j + 1) + nxt(x) * col(i, j + 2) + col(i, j + 3)

        idx = range(HY_INTERLEAVE)

        def conv(vs, o):
            zs = _fwd_dft(vs, g1k[...], twr[...], twi[...], f2[...])
            ys = []
            for i, (zr, zi) in enumerate(zs):
                hr = hr_ref[o, hrows[i], :]
                hi = hi_ref[o, hrows[i], :]
                ys.append((zr * hr - zi * hi, zr * hi + zi * hr))
            return _inv_dft(ys, twr[...], twi[...], f2i[...], gik[...])

        zs = [sconv(uz_ref, i, 8) for i in idx]
        cv = conv(zs, 0)
        zs = [sconv(u1_ref, i, 0) * (cv[i] + col(i, 12) * zs[i]) for i in idx]
        cv = conv(zs, 1)
        for i in idx:
            o_ref[0, cs[i], :] = sconv(u2_ref, i, 4) * (cv[i] + col(i, 13) * zs[i])
        return carry

    lax.fori_loop(0, groups // HY_INTERLEAVE, body, 0)


def _hy_main(ut, prm, hr, hi, consts, n1):
    b, _, n_half = ut.shape
    ct = 64
    nct = HY_WIDTH // ct
    cs = [consts[k] for k in ("g1k", "twr", "twi", "f2", "f2i", "gik")]
    u_spec = lambda sec: pl.BlockSpec((1, ct, n_half), lambda c, bb: (bb, sec * nct + c, 0))
    h_spec = pl.BlockSpec((HY_ORDER, ct * n1, LANES), lambda c, bb: (0, c, 0))
    return pl.pallas_call(
        functools.partial(_hy_main_body, groups=ct // DFT_GROUP, n1=n1),
        grid=(nct, b),
        in_specs=[u_spec(0), u_spec(1), u_spec(2),
                  pl.BlockSpec((ct, 16), lambda c, bb: (c, 0)),
                  h_spec, h_spec] + [_const_spec(c, 2) for c in cs],
        out_specs=pl.BlockSpec((1, ct, n_half), lambda c, bb: (bb, c, 0)),
        out_shape=jax.ShapeDtypeStruct((b, HY_WIDTH, n_half), F32),
        compiler_params=_params("parallel", "parallel"),
        name="hy_main",
    )(ut, ut, ut, prm, hr, hi, *cs)


def _hyc_consts(n):
    nn = 2 * n
    a = 2 * np.pi * np.arange(n)[:, None] * np.arange(nn)[None, :] / nn
    fc = np.concatenate([np.cos(a), -np.sin(a)], axis=1)
    fi = np.concatenate([np.cos(a.T), -np.sin(a.T)], axis=0)
    return jnp.asarray(fc, BF16), jnp.asarray(fi, BF16)


def _hyc_spec_body(f_ref, b_ref, fc_ref, hr_ref, hi_ref):
    nn = hr_ref.shape[-1]
    zf = jnp.dot(f_ref[...].astype(BF16), fc_ref[...], preferred_element_type=F32)
    zb = jnp.dot(b_ref[...].astype(BF16), fc_ref[...], preferred_element_type=F32)
    hr_ref[...] = (zf[:, :nn] + zb[:, :nn]) * (1.0 / nn)
    hi_ref[...] = (zf[:, nn:] - zb[:, nn:]) * (1.0 / nn)


def _hyc_spec(f, b, fc):
    rows, n = f.shape
    out_shape = jax.ShapeDtypeStruct((rows, 2 * n), F32)
    return pl.pallas_call(
        _hyc_spec_body,
        out_shape=[out_shape, out_shape],
        compiler_params=pltpu.CompilerParams(vmem_limit_bytes=VMEM_LIMIT_BYTES),
        name="hyc_spec",
    )(f, b, fc)


def _hyc_main_body(u1_ref, u2_ref, uz_ref, p_ref, hr_ref, hi_ref, fc_ref, fi_ref, o_ref):
    shape = u1_ref.shape[1:]
    n = shape[1]
    nn = 2 * n
    lane = lax.broadcasted_iota(jnp.int32, shape, 1)
    pr = p_ref[...]
    col = lambda j: pr[:, j:j + 1]

    def sconv(ref, j):
        x = ref[0]
        p = jnp.where(lane == 0, 0.0, pltpu.roll(x, 1, 1))
        q = jnp.where(lane == n - 1, 0.0, pltpu.roll(x, n - 1, 1))
        return p * col(j) + x * col(j + 1) + q * col(j + 2) + col(j + 3)

    def conv(v, o):
        s = jnp.dot(v.astype(BF16), fc_ref[...], preferred_element_type=F32)
        zr, zi = s[:, :nn], s[:, nn:]
        hr, hi = hr_ref[o], hi_ref[o]
        y = jnp.concatenate([zr * hr - zi * hi, zr * hi + zi * hr], axis=1).astype(BF16)
        return jnp.dot(y, fi_ref[...], preferred_element_type=F32)

    x1 = sconv(u1_ref, 0)
    x2 = sconv(u2_ref, 4)
    z = sconv(uz_ref, 8)
    z = x1 * (conv(z, 0) + col(12) * z)
    o_ref[0] = x2 * (conv(z, 1) + col(13) * z)


def _hyc_main(ut, prm, hr, hi, fc, fi, nb, n):
    ct = 128
    nct = HY_WIDTH // ct
    u_spec = lambda sec: pl.BlockSpec((1, ct, n), lambda bb, c: (0, sec * nct + c, bb))
    h_spec = pl.BlockSpec((HY_ORDER, ct, 2 * n), lambda bb, c: (0, c, 0))
    return pl.pallas_call(
        _hyc_main_body,
        grid=(nb, nct),
        in_specs=[u_spec(0), u_spec(1), u_spec(2),
                  pl.BlockSpec((ct, 16), lambda bb, c: (c, 0)),
                  h_spec, h_spec, _const_spec(fc, 2), _const_spec(fi, 2)],
        out_specs=pl.BlockSpec((1, ct, n), lambda bb, c: (0, c, bb)),
        out_shape=jax.ShapeDtypeStruct((1, HY_WIDTH, nb * n), F32),
        compiler_params=_params("parallel", "parallel"),
        name="hyc_main",
    )(ut, ut, ut, prm, hr, hi, fc, fi)


def _merge_body(x_ref, m_ref, g_ref, wbrg, yh_ref, of_ref, ob_ref, og_ref, gn_ref, ya_ref,
                wh, wg, wa, wo, o_ref):
    x = x_ref[0]
    m = m_ref[0]
    h = _norm_mod(x, g_ref[...], m[0:1], m[1:2]).astype(BF16)
    gates = jnp.dot(h, wbrg[...], preferred_element_type=F32)
    yh = yh_ref[0].T
    o = of_ref[0] + ob_ref[0]
    parts = []
    for hh in range(GLA_HEADS):
        t = o[:, hh * GLA_DV:(hh + 1) * GLA_DV]
        parts.append(t * lax.rsqrt(jnp.mean(t * t, axis=-1, keepdims=True) + EPS) * gn_ref[...])
    og = og_ref[0]
    yg = jnp.concatenate(parts, axis=1) * (og * _sigmoid(og))
    d = D_MODEL
    mm = (_sigmoid(gates[:, :d]) * jnp.dot(yh.astype(BF16), wh[...], preferred_element_type=F32)
          + _sigmoid(gates[:, d:2 * d]) * jnp.dot(yg.astype(BF16), wg[...], preferred_element_type=F32)
          + _sigmoid(gates[:, 2 * d:]) * jnp.dot(ya_ref[0].T.astype(BF16), wa[...],
                                                  preferred_element_type=F32))
    out = jnp.dot(mm.astype(BF16), wo[...], preferred_element_type=F32)
    o_ref[0] = x + m[2:3] * out


def _merge(x, mods, g, wbrg, yht, of, ob, pg, gn, yat, wh, wg, wa, wo, tm):
    bm, r, _ = x.shape
    row = lambda w: pl.BlockSpec((1, tm, w), lambda b, i: (b, i, 0))
    full = lambda a: pl.BlockSpec(a.shape, lambda b, i: (0,) * a.ndim)
    og_blk = (2 * GLA_K_W + GLA_V_W) // GLA_V_W
    return pl.pallas_call(
        _merge_body,
        grid=(bm, r // tm),
        in_specs=[
            row(D_MODEL),
            pl.BlockSpec((1, 6, D_MODEL), lambda b, i: (b, 0, 0)),
            full(g), full(wbrg),
            pl.BlockSpec((1, HY_WIDTH, tm), lambda b, i: (b, 0, i)),
            row(GLA_V_W), row(GLA_V_W),
            pl.BlockSpec((1, tm, GLA_V_W), lambda b, i: (b, i, og_blk)),
            full(gn),
            pl.BlockSpec((1, ATT_Q_W, tm), lambda b, i: (b, 0, i)),
            full(wh), full(wg), full(wa), full(wo),
        ],
        out_specs=row(D_MODEL),
        out_shape=jax.ShapeDtypeStruct(x.shape, F32),
        compiler_params=_params("parallel", "parallel"),
        name="merge",
    )(x, mods, g, wbrg, yht, of, ob, pg, gn, yat, wh, wg, wa, wo)


def _route(logits, rb):
    scores = _sigmoid(logits)
    sel = scores + rb
    lane = lax.broadcasted_iota(jnp.int32, sel.shape, 1).astype(F32)
    neg = -jnp.inf

    def top2(v):
        m1 = jnp.max(v, axis=-1, keepdims=True)
        i1 = jnp.min(jnp.where(v == m1, lane, float(N_EXPERTS)), axis=-1, keepdims=True)
        v2 = jnp.where(lane == i1, neg, v)
        m2 = jnp.max(v2, axis=-1, keepdims=True)
        i2 = jnp.min(jnp.where(v2 == m2, lane, float(N_EXPERTS)), axis=-1, keepdims=True)
        return m1, m2, i1, i2

    group_of = jnp.floor(lane * (1.0 / EXPERTS_PER_GROUP))
    best = None
    best_g = None
    for g in range(N_GROUPS):
        m1, m2, _, _ = top2(jnp.where(group_of == float(g), sel, neg))
        gs = m1 + m2
        if best is None:
            best, best_g = gs, jnp.zeros_like(gs)
        else:
            better = gs > best
            best_g = jnp.where(better, float(g), best_g)
            best = jnp.where(better, gs, best)
    _, _, i1, i2 = top2(jnp.where(group_of == best_g, sel, neg))
    w = jnp.where((lane == i1) | (lane == i2), scores, 0.0)
    return w / jnp.sum(w, axis=-1, keepdims=True), best_g


def _moe_body(x_ref, m_ref, g_ref, rw_ref, rb_ref, wg_ref, wu_ref, wd_ref, o_ref,
              h_scr, ghi_scr, glo_scr, grp_scr, pos_scr, grpt_scr, post_scr, acc_scr):
    g = pl.program_id(2)
    gf = g.astype(F32)
    tm = x_ref.shape[1]
    sub = MOE_SUB
    lane = lax.broadcasted_iota(jnp.int32, (tm, LANES), 1).astype(F32)

    @pl.when(g == 0)
    def _():
        m = m_ref[0]
        h = _norm_mod(x_ref[0], g_ref[...], m[3:4], m[4:5])
        h_scr[...] = h.astype(BF16)
        logits = jnp.dot(h, rw_ref[...], precision=HIGHEST, preferred_element_type=F32)
        gates, grp = _route(logits, rb_ref[...])
        ghi = gates.astype(BF16)
        ghi_scr[...] = ghi
        glo_scr[...] = (gates - ghi.astype(F32)).astype(BF16)
        grp_b = jnp.broadcast_to(grp, (tm, LANES))
        onehot = (lane == grp_b).astype(BF16)
        ranks = []
        for rb_ in range(tm // sub):
            r = lax.broadcasted_iota(jnp.int32, (sub, tm), 0) + rb_ * sub
            c = lax.broadcasted_iota(jnp.int32, (sub, tm), 1)
            ranks.append(jnp.dot((c < r).astype(BF16), onehot, preferred_element_type=F32))
        rank = jnp.concatenate(ranks, axis=0)
        pos = jnp.sum(jnp.where(lane == grp_b, rank, 0.0), axis=-1, keepdims=True)
        pos_b = jnp.broadcast_to(pos, (tm, LANES))
        grp_scr[...] = grp_b
        pos_scr[...] = pos_b
        grpt_scr[...] = grp_b.T
        post_scr[...] = pos_b.T
        acc_scr[...] = jnp.zeros_like(acc_scr)

    in_group = grp_scr[...] == gf
    n_g = jnp.sum(jnp.where(in_group[:, 0:1], 1.0, 0.0)).astype(jnp.int32)
    n_sub = (n_g + (sub - 1)) // sub
    d_iota = lax.broadcasted_iota(jnp.int32, (sub, tm), 0).astype(F32)
    lane16 = lax.broadcasted_iota(jnp.int32, (sub, N_EXPERTS), 1)

    def body(k, carry):
        kf = (k * sub).astype(F32)
        p = ((grpt_scr[...] == gf) & (post_scr[...] - kf == d_iota)).astype(BF16)
        hd = jnp.dot(p, h_scr[...], preferred_element_type=F32).astype(BF16)
        gd = (jnp.dot(p, ghi_scr[...], preferred_element_type=F32)
              + jnp.dot(p, glo_scr[...], preferred_element_type=F32))
        experts = range(EXPERTS_PER_GROUP)
        a = [jnp.dot(hd, wg_ref[j], preferred_element_type=F32) for j in experts]
        b = [jnp.dot(hd, wu_ref[j], preferred_element_type=F32) for j in experts]
        hid = [(a[j] * _sigmoid(a[j]) * b[j]).astype(BF16) for j in experts]
        dn = [jnp.dot(hid[j], wd_ref[j], preferred_element_type=F32) for j in experts]
        y = jnp.zeros((sub, D_MODEL), F32)
        for j in experts:
            ge = jnp.sum(jnp.where(lane16 == g * EXPERTS_PER_GROUP + j, gd, 0.0),
                         axis=-1, keepdims=True)
            y = y + ge * dn[j]
        pt = (in_group & (pos_scr[...] - kf == lane)).astype(BF16)
        acc_scr[...] += jnp.dot(pt, y.astype(BF16), preferred_element_type=F32)
        return carry

    lax.fori_loop(0, n_sub, body, 0)

    @pl.when(g == N_GROUPS - 1)
    def _():
        o_ref[0] = x_ref[0] + m_ref[0][5:6] * acc_scr[...]


def _moe(x, mods, g, rw, rb, wg, wu, wd, tm):
    bm, r, _ = x.shape
    row = pl.BlockSpec((1, tm, D_MODEL), lambda b, i, e: (b, i, 0))
    return pl.pallas_call(
        _moe_body,
        grid=(bm, r // tm, N_GROUPS),
        in_specs=[
            row,
            pl.BlockSpec((1, 6, D_MODEL), lambda b, i, e: (b, 0, 0)),
            pl.BlockSpec((1, D_MODEL), lambda b, i, e: (0, 0)),
            pl.BlockSpec((D_MODEL, N_EXPERTS), lambda b, i, e: (0, 0)),
            pl.BlockSpec((1, N_EXPERTS), lambda b, i, e: (0, 0)),
            pl.BlockSpec((EXPERTS_PER_GROUP, D_MODEL, D_EXPERT), lambda b, i, e: (e, 0, 0)),
            pl.BlockSpec((EXPERTS_PER_GROUP, D_MODEL, D_EXPERT), lambda b, i, e: (e, 0, 0)),
            pl.BlockSpec((EXPERTS_PER_GROUP, D_EXPERT, D_MODEL), lambda b, i, e: (e, 0, 0)),
        ],
        out_specs=row,
        out_shape=jax.ShapeDtypeStruct(x.shape, F32),
        scratch_shapes=[
            pltpu.VMEM((tm, D_MODEL), BF16),
            pltpu.VMEM((tm, N_EXPERTS), BF16),
            pltpu.VMEM((tm, N_EXPERTS), BF16),
            pltpu.VMEM((tm, LANES), F32),
            pltpu.VMEM((tm, LANES), F32),
            pltpu.VMEM((LANES, tm), F32),
            pltpu.VMEM((LANES, tm), F32),
            pltpu.VMEM((tm, D_MODEL), F32),
        ],
        compiler_params=_params("parallel", "parallel", "arbitrary"),
        name="moe",
    )(x, mods, g, rw, rb, wg, wu, wd)


def _rope_tables(n_tokens):
    rows = n_tokens // GRID_W
    row = jnp.broadcast_to(jnp.arange(rows)[:, None], (rows, GRID_W)).reshape(-1).astype(F32)
    col = jnp.broadcast_to(jnp.arange(GRID_W)[None, :], (rows, GRID_W)).reshape(-1).astype(F32)
    inv_freq = ROPE_THETA ** (-jnp.arange(ROPE_PAIRS_PER_AXIS, dtype=F32) / ROPE_PAIRS_PER_AXIS)
    ang = jnp.concatenate([row[:, None] * inv_freq, col[:, None] * inv_freq], axis=-1)
    cos, sin = jnp.cos(ang), jnp.sin(ang)
    cos2, sin2 = jnp.concatenate([cos, cos], axis=-1), jnp.concatenate([-sin, sin], axis=-1)
    return cos2, sin2, cos2.T, sin2.T


def _split_w_in(w):
    sizes = (ATT_KV_W, ATT_KV_W, GLA_K_W, GLA_V_W, 2 * GLA_RANK,
             ATT_Q_W, GLA_K_W, GLA_V_W, 3 * HY_WIDTH, 3 * D_MODEL)
    cuts = [int(v) for v in np.cumsum(sizes)[:-1]]
    a_k, a_v, g_k, g_v, g_a, a_q, g_q, g_og, hy_u, br_g = jnp.split(w, cuts, axis=1)
    perm = np.concatenate([np.arange(0, HEAD_DIM, 2), np.arange(1, HEAD_DIM, 2)])
    perm_q = np.concatenate([h * HEAD_DIM + perm for h in range(ATT_HEADS)])
    perm_k = np.concatenate([h * HEAD_DIM + perm for h in range(ATT_KV_HEADS)])
    w_qv_t = jnp.concatenate([a_q[:, perm_q], a_v], axis=1).T.astype(BF16)
    w_k = a_k[:, perm_k].astype(BF16)
    pad = jnp.zeros((D_MODEL, GLA_A_PAD - 2 * GLA_RANK), w.dtype)
    w_gla = jnp.concatenate([g_q, g_k, g_v, g_og, g_a, pad], axis=1).astype(BF16)
    return w_qv_t, w_k, w_gla, hy_u.T.astype(BF16), br_g.astype(BF16), perm


def kernel(x, c, ctx, c_ctx, w_mod, b_mod, norm1_g, norm2_g, w_in, q_norm_g, k_norm_g, gla_wa2, gla_ba, gla_norm_g, hy_conv_w, hy_conv_b, hy_pos_w1, hy_pos_b1, hy_sin_freq, hy_pos_w2, hy_pos_b2, hy_pos_w3, hy_decay, hy_skip, w_br_hy, w_br_gla, w_br_att, w_out, router_w, router_b, moe_w_gate, moe_w_up, moe_w_down):
    nb, seq, d = x.shape
    nctx = ctx.shape[1]
    n1_lat = 2 * seq // LANES

    c_all = jnp.concatenate([c, c_ctx[None, :], jnp.zeros((16 - nb - 1, d), F32)], axis=0)
    mods_all = _mods(c_all, w_mod, b_mod)

    rope_tabs = _rope_tables(seq)
    consts = _dft_consts(n1_lat)
    fc, fi = _hyc_consts(nctx)
    tri_lat = _gla_tri(256)
    zero_state = jnp.zeros((nb, GLA_DV, GLA_K_W), F32)
    rb = router_b.reshape(1, N_EXPERTS)

    xc = ctx.reshape(1, nb * nctx, d)
    for l in range(DEPTH):
        last = l == DEPTH - 1
        mods_lat = mods_all[l, :nb].reshape(nb, 6, d)
        mods_ctx = mods_all[l, nb:nb + 1].reshape(1, 6, d)
        g1 = norm1_g[l].reshape(1, d)
        g2 = norm2_g[l].reshape(1, d)
        w_qv_t, w_k, w_gla, w_hyt, w_brg, perm = _split_w_in(w_in[l])
        qg = (q_norm_g[l][perm] * Q_SCALE).reshape(HEAD_DIM, 1)
        kg = k_norm_g[l][perm].reshape(1, HEAD_DIM)
        wa2 = gla_wa2[l]
        ba = gla_ba[l].reshape(2, 1, GLA_K_W)
        gn = gla_norm_g[l].reshape(1, GLA_DV)
        wh, wg, wa, wo = (w_br_hy[l].astype(BF16), w_br_gla[l].astype(BF16),
                          w_br_att[l].astype(BF16), w_out[l].astype(BF16))
        cw, cb = hy_conv_w[l], hy_conv_b[l]
        sec = lambda s: [cw[0, s], cw[1, s], cw[2, s], cb[s]]
        w_ = HY_WIDTH
        prm = jnp.stack(sec(slice(0, w_)) + sec(slice(w_, 2 * w_)) + sec(slice(2 * w_, 3 * w_))
                        + [hy_skip[l, 0], hy_skip[l, 1], jnp.zeros((w_,), F32), jnp.zeros((w_,), F32)],
                        axis=1)
        filt_args = (hy_pos_w1[l], hy_pos_b1[l], hy_sin_freq[l], hy_pos_w2[l], hy_pos_b2[l],
                     hy_pos_w3[l], hy_decay[l])

        qt, k, vt = _proj_att(x, mods_lat, g1, w_qv_t, w_k, qg, kg, rope_tabs, 512)
        qct, kc, vct = _proj_att(xc, mods_ctx, g1, w_qv_t, w_k, qg, kg, None, 256)
        pg = _proj_plain(x, mods_lat, g1, w_gla, 512)
        pgc = _proj_plain(xc, mods_ctx, g1, w_gla, 256)
        ut = _proj_t(x, mods_lat, g1, w_hyt, 512)

        y_att = _attn(qt, k, vt, kc, vct, nb, 256)

        cof, cob, s_f, s_b = _gla(pgc.reshape(nb, nctx, GLA_COLS), wa2, ba, _gla_tri(nctx),
                                  zero_state, zero_state, nctx)
        of, ob, _, _ = _gla(pg, wa2, ba, tri_lat, s_f, s_b, 256)

        ff, fb = _hy_filters(seq, *filt_args)
        hr, hi = _hy_spec(ff.reshape(-1, seq), fb.reshape(-1, seq), consts, n1_lat)
        hr = hr.reshape(HY_ORDER, HY_WIDTH * n1_lat, LANES)
        hi = hi.reshape(HY_ORDER, HY_WIDTH * n1_lat, LANES)
        y_hyt = _hy_main(ut, prm, hr, hi, consts, n1_lat)

        x = _merge(x, mods_lat, g1, w_brg, y_hyt, of, ob, pg, gn, y_att, wh, wg, wa, wo, 256)
        moe_w = (moe_w_gate[l].astype(BF16), moe_w_up[l].astype(BF16), moe_w_down[l].astype(BF16))
        x = _moe(x, mods_lat, g2, router_w, rb, *moe_w, 1024)

        if not last:
            yc_att = _attn(qct, None, None, kc, vct, nb, nctx)
            uct = _proj_t(xc, mods_ctx, g1, w_hyt, 256)
            cff, cfb = _hy_filters(nctx, *filt_args)
            chr_, chi = _hyc_spec(cff.reshape(-1, nctx), cfb.reshape(-1, nctx), fc)
            chr_ = chr_.reshape(HY_ORDER, HY_WIDTH, 2 * nctx)
            chi = chi.reshape(HY_ORDER, HY_WIDTH, 2 * nctx)
            yc_hyt = _hyc_main(uct, prm, chr_, chi, fc, fi, nb, nctx)
            xc = _merge(xc, mods_ctx, g1, w_brg, yc_hyt,
                        cof.reshape(1, nb * nctx, GLA_V_W), cob.reshape(1, nb * nctx, GLA_V_W),
                        pgc, gn, yc_att, wh, wg, wa, wo, 256)
            xc = _moe(xc, mods_ctx, g2, router_w, rb, *moe_w, min(1024, nb * nctx))
    return x
```

```python
import functools
import math

import numpy as np
import jax
import jax.numpy as jnp
from jax import lax
from jax.experimental import pallas as pl
from jax.experimental.pallas import tpu as pltpu

F32 = jnp.float32
BF16 = jnp.bfloat16
HIGHEST = lax.Precision.HIGHEST

D_MODEL = 1024
DEPTH = 2
GRID_W = 64
EPS = 1e-6

ATT_HEADS = 8
ATT_KV_HEADS = 2
ATT_GROUP = ATT_HEADS // ATT_KV_HEADS
HEAD_DIM = 128
ROPE_PAIRS_PER_AXIS = HEAD_DIM // 4
ROPE_THETA = 10000.0
Q_SCALE = HEAD_DIM ** -0.5 * math.log2(math.e)

GLA_HEADS = 4
GLA_DK = 64
GLA_DV = 128
GLA_RANK = 16
GLA_TAU = 16.0
GLA_CHUNK = 64

HY_WIDTH = 512
HY_ORDER = 2
HY_BANDS = 16
HY_EMB = 1 + 2 * HY_BANDS
HY_EMB_PAD = 40
HY_FFN = 64

N_EXPERTS = 16
N_GROUPS = 4
EXPERTS_PER_GROUP = N_EXPERTS // N_GROUPS
D_EXPERT = 512

ATT_Q_W = ATT_HEADS * HEAD_DIM
ATT_KV_W = ATT_KV_HEADS * HEAD_DIM
GLA_K_W = GLA_HEADS * GLA_DK
GLA_V_W = GLA_HEADS * GLA_DV
GLA_A_PAD = 128
GLA_COLS = 2 * GLA_K_W + 2 * GLA_V_W + GLA_A_PAD
ATT_COLS = ATT_Q_W + 2 * ATT_KV_W

LANES = 128
SUBLANES = 8
MOE_SUB = LANES
VMEM_LIMIT_BYTES = 60 * 1024 * 1024

DFT_GROUP = SUBLANES
HY_INTERLEAVE = 4

NT_DIMS = (((1,), (1,)), ((), ()))
TN_DIMS = (((0,), (0,)), ((), ()))


def _params(*sem):
    return pltpu.CompilerParams(dimension_semantics=sem, vmem_limit_bytes=VMEM_LIMIT_BYTES)


def _sigmoid(x):
    return 1.0 / (1.0 + jnp.exp(-x))


def _norm_mod(x, g, shift, scale):
    ms = jnp.mean(x * x, axis=-1, keepdims=True)
    return (x * lax.rsqrt(ms + EPS) * g) * (1.0 + scale) + shift


def _mod_body(c_ref, w_ref, b_ref, o_ref):
    c = c_ref[...]
    s = c * _sigmoid(c)
    o_ref[0] = jnp.dot(s, w_ref[0], precision=HIGHEST, preferred_element_type=F32) + b_ref[0]


def _mods(c_all, w_mod, b_mod):
    tn = 512
    rows = c_all.shape[0]
    return pl.pallas_call(
        _mod_body,
        grid=(DEPTH, 6 * D_MODEL // tn),
        in_specs=[
            pl.BlockSpec((rows, D_MODEL), lambda l, j: (0, 0)),
            pl.BlockSpec((1, D_MODEL, tn), lambda l, j: (l, 0, j)),
            pl.BlockSpec((1, 1, tn), lambda l, j: (l, 0, j)),
        ],
        out_specs=pl.BlockSpec((1, rows, tn), lambda l, j: (l, 0, j)),
        out_shape=jax.ShapeDtypeStruct((DEPTH, rows, 6 * D_MODEL), F32),
        compiler_params=_params("parallel", "parallel"),
        name="mods",
    )(c_all, w_mod, b_mod.reshape(DEPTH, 1, 6 * D_MODEL))


def _proj_att_body(*refs, rope):
    if rope:
        (x_ref, m_ref, g_ref, wqv_ref, wk_ref, qg_ref, kg_ref, cos_ref, sin_ref, cost_ref, sint_ref,
         qt_ref, k_ref, vt_ref) = refs
    else:
        x_ref, m_ref, g_ref, wqv_ref, wk_ref, qg_ref, kg_ref, qt_ref, k_ref, vt_ref = refs
    m = m_ref[0]
    h = _norm_mod(x_ref[0], g_ref[...], m[0:1], m[1:2]).astype(BF16)
    pt = lax.dot_general(wqv_ref[...], h, NT_DIMS, preferred_element_type=F32)
    pk = jnp.dot(h, wk_ref[...], preferred_element_type=F32)
    half = HEAD_DIM // 2

    for i in range(ATT_HEADS):
        sl = slice(i * HEAD_DIM, (i + 1) * HEAD_DIM)
        t = pt[sl]
        t = t * lax.rsqrt(jnp.mean(t * t, axis=0, keepdims=True) + EPS) * qg_ref[...]
        if rope:
            t = t * cost_ref[...] + pltpu.roll(t, half, 0) * sint_ref[...]
        qt_ref[0, sl, :] = t.astype(BF16)
    for i in range(ATT_KV_HEADS):
        sl = slice(i * HEAD_DIM, (i + 1) * HEAD_DIM)
        t = pk[:, sl]
        t = t * lax.rsqrt(jnp.mean(t * t, axis=-1, keepdims=True) + EPS) * kg_ref[...]
        if rope:
            t = t * cos_ref[...] + pltpu.roll(t, half, 1) * sin_ref[...]
        k_ref[0, :, sl] = t.astype(BF16)
    vt_ref[0] = pt[ATT_Q_W:].astype(BF16)


def _proj_att(x, mods, g, wqv_t, wk, qg_col, kg, rope_tabs, tm):
    bm, r, _ = x.shape
    rope = rope_tabs is not None
    in_specs = [
        pl.BlockSpec((1, tm, D_MODEL), lambda b, i: (b, i, 0)),
        pl.BlockSpec((1, 6, D_MODEL), lambda b, i: (b, 0, 0)),
        pl.BlockSpec((1, D_MODEL), lambda b, i: (0, 0)),
        pl.BlockSpec((ATT_Q_W + ATT_KV_W, D_MODEL), lambda b, i: (0, 0)),
        pl.BlockSpec((D_MODEL, ATT_KV_W), lambda b, i: (0, 0)),
        pl.BlockSpec((HEAD_DIM, 1), lambda b, i: (0, 0)),
        pl.BlockSpec((1, HEAD_DIM), lambda b, i: (0, 0)),
    ]
    args = [x, mods, g, wqv_t, wk, qg_col, kg]
    if rope:
        in_specs += [pl.BlockSpec((tm, HEAD_DIM), lambda b, i: (i, 0))] * 2
        in_specs += [pl.BlockSpec((HEAD_DIM, tm), lambda b, i: (0, i))] * 2
        args += list(rope_tabs)
    return pl.pallas_call(
        functools.partial(_proj_att_body, rope=rope),
        grid=(bm, r // tm),
        in_specs=in_specs,
        out_specs=[
            pl.BlockSpec((1, ATT_Q_W, tm), lambda b, i: (b, 0, i)),
            pl.BlockSpec((1, tm, ATT_KV_W), lambda b, i: (b, i, 0)),
            pl.BlockSpec((1, ATT_KV_W, tm), lambda b, i: (b, 0, i)),
        ],
        out_shape=[
            jax.ShapeDtypeStruct((bm, ATT_Q_W, r), BF16),
            jax.ShapeDtypeStruct((bm, r, ATT_KV_W), BF16),
            jax.ShapeDtypeStruct((bm, ATT_KV_W, r), BF16),
        ],
        compiler_params=_params("parallel", "parallel"),
        name="proj_att",
    )(*args)


def _proj_plain_body(x_ref, m_ref, g_ref, w_ref, o_ref):
    m = m_ref[0]
    h = _norm_mod(x_ref[0], g_ref[...], m[0:1], m[1:2]).astype(BF16)
    o_ref[0] = jnp.dot(h, w_ref[...], preferred_element_type=F32)


def _proj_plain(x, mods, g, w, tm):
    bm, r, _ = x.shape
    n = w.shape[1]
    return pl.pallas_call(
        _proj_plain_body,
        grid=(bm, r // tm),
        in_specs=[
            pl.BlockSpec((1, tm, D_MODEL), lambda b, i: (b, i, 0)),
            pl.BlockSpec((1, 6, D_MODEL), lambda b, i: (b, 0, 0)),
            pl.BlockSpec((1, D_MODEL), lambda b, i: (0, 0)),
            pl.BlockSpec((D_MODEL, n), lambda b, i: (0, 0)),
        ],
        out_specs=pl.BlockSpec((1, tm, n), lambda b, i: (b, i, 0)),
        out_shape=jax.ShapeDtypeStruct((bm, r, n), F32),
        compiler_params=_params("parallel", "parallel"),
        name="proj_gla",
    )(x, mods, g, w)


def _proj_t_body(x_ref, m_ref, g_ref, wt_ref, o_ref):
    m = m_ref[0]
    h = _norm_mod(x_ref[0], g_ref[...], m[0:1], m[1:2]).astype(BF16)
    o_ref[0] = lax.dot_general(wt_ref[...], h, NT_DIMS, preferred_element_type=F32)


def _proj_t(x, mods, g, wt, tm):
    bm, r, _ = x.shape
    n = wt.shape[0]
    return pl.pallas_call(
        _proj_t_body,
        grid=(bm, r // tm),
        in_specs=[
            pl.BlockSpec((1, tm, D_MODEL), lambda b, i: (b, i, 0)),
            pl.BlockSpec((1, 6, D_MODEL), lambda b, i: (b, 0, 0)),
            pl.BlockSpec((1, D_MODEL), lambda b, i: (0, 0)),
            pl.BlockSpec((n, D_MODEL), lambda b, i: (0, 0)),
        ],
        out_specs=pl.BlockSpec((1, n, tm), lambda b, i: (b, 0, i)),
        out_shape=jax.ShapeDtypeStruct((bm, n, r), F32),
        compiler_params=_params("parallel", "parallel"),
        name="proj_hy",
    )(x, mods, g, wt)


def _attn_body(*refs, has_lat, tk):
    if has_lat:
        qt_ref, kl_ref, vlt_ref, kc_ref, vct_ref, o_ref = refs
    else:
        qt_ref, kc_ref, vct_ref, o_ref = refs
    qt = qt_ref[0]
    tq = qt.shape[1]
    q4t = jnp.concatenate([qt[g * HEAD_DIM:(g + 1) * HEAD_DIM, :] for g in range(ATT_GROUP)], axis=1)

    def update(k_c, vt_c, state):
        s = jnp.dot(k_c, q4t, preferred_element_type=F32)
        mc = jnp.max(s, axis=0, keepdims=True)
        if state is None:
            p = jnp.exp2(s - mc)
            return mc, jnp.sum(p, axis=0, keepdims=True), jnp.dot(vt_c, p.astype(BF16),
                                                                   preferred_element_type=F32)
        m, den, acc = state
        m_new = jnp.maximum(m, mc)
        alpha = jnp.exp2(m - m_new)
        p = jnp.exp2(s - m_new)
        den = alpha * den + jnp.sum(p, axis=0, keepdims=True)
        acc = alpha * acc + jnp.dot(vt_c, p.astype(BF16), preferred_element_type=F32)
        return m_new, den, acc

    state = update(kc_ref[0], vct_ref[0], None)
    if has_lat:
        for j in range(kl_ref.shape[1] // tk):
            state = update(kl_ref[0, j * tk:(j + 1) * tk, :], vlt_ref[0, :, j * tk:(j + 1) * tk], state)
    _, den, acc = state
    o = acc / den
    for g in range(ATT_GROUP):
        o_ref[0, g * HEAD_DIM:(g + 1) * HEAD_DIM, :] = o[:, g * tq:(g + 1) * tq]


def _attn(qt, k_lat, vt_lat, k_ctx, vt_ctx, nb, tq, tk=4096):
    has_lat = k_lat is not None
    gw = ATT_GROUP * HEAD_DIM
    sc = k_ctx.shape[1] // nb
    if has_lat:
        sq = qt.shape[2]
        q_map = lambda bb, h, i: (bb, h, i)
    else:
        sq = tq
        q_map = lambda bb, h, i: (0, h, bb)
    in_specs = [pl.BlockSpec((1, gw, tq), q_map)]
    args = [qt]
    if has_lat:
        sk = k_lat.shape[1]
        in_specs += [pl.BlockSpec((1, sk, HEAD_DIM), lambda bb, h, i: (bb, 0, h)),
                     pl.BlockSpec((1, HEAD_DIM, sk), lambda bb, h, i: (bb, h, 0))]
        args += [k_lat, vt_lat]
    in_specs += [pl.BlockSpec((1, sc, HEAD_DIM), lambda bb, h, i: (0, bb, h)),
                 pl.BlockSpec((1, HEAD_DIM, sc), lambda bb, h, i: (0, h, bb))]
    args += [k_ctx, vt_ctx]
    return pl.pallas_call(
        functools.partial(_attn_body, has_lat=has_lat, tk=tk),
        grid=(nb, ATT_KV_HEADS, sq // tq),
        in_specs=in_specs,
        out_specs=pl.BlockSpec((1, gw, tq), q_map),
        out_shape=jax.ShapeDtypeStruct(qt.shape, F32),
        compiler_params=_params("parallel", "parallel", "parallel"),
        name="attn",
    )(*args)


def _gla_pair(qs, ks, vs, a_s, wa2, ba, tri, st_refs):
    tt = qs[0].shape[0]
    nc = tt // GLA_CHUNK
    dirs = range(2)
    heads = range(GLA_HEADS)
    zs = [jnp.dot(a_s[d], wa2[d], precision=HIGHEST, preferred_element_type=F32) + ba[d] for d in dirs]
    las = [(jnp.minimum(z, 0.0) - jnp.log(1.0 + jnp.exp(-jnp.abs(z)))) * (1.0 / GLA_TAU) for z in zs]
    cums = [jnp.dot(tri[d], las[d], precision=HIGHEST, preferred_element_type=F32) for d in dirs]
    qd = [(qs[d] * (GLA_DK ** -0.5) * jnp.exp(cums[d])).astype(BF16) for d in dirs]
    ki = [(ks[d] * jnp.exp(-cums[d])).astype(BF16) for d in dirs]
    vb = [vs[d].astype(BF16) for d in dirs]
    row = lax.broadcasted_iota(jnp.int32, (GLA_CHUNK, GLA_CHUNK), 0)
    col = lax.broadcasted_iota(jnp.int32, (GLA_CHUNK, GLA_CHUNK), 1)
    masks = [col <= row, col >= row]
    hk = lambda h: slice(h * GLA_DK, (h + 1) * GLA_DK)
    hv = lambda h: slice(h * GLA_DV, (h + 1) * GLA_DV)
    outs = [[None] * nc, [None] * nc]
    for step in range(nc):
        chunk = [step, nc - 1 - step]
        rs = [slice(c * GLA_CHUNK, (c + 1) * GLA_CHUNK) for c in chunk]
        ends = [rs[0].stop - 1, rs[1].start]
        cl = [cums[d][ends[d]:ends[d] + 1, :] for d in dirs]
        kte = [(ks[d][rs[d]] * jnp.exp(cl[d] - cums[d][rs[d]])).astype(BF16) for d in dirs]
        st = [st_refs[d][...] for d in dirs]
        stb = [s.astype(BF16) for s in st]
        att = [[lax.dot_general(qd[d][rs[d], hk(h)], ki[d][rs[d], hk(h)], NT_DIMS,
                                preferred_element_type=F32) for h in heads] for d in dirs]
        upd = [[lax.dot_general(vb[d][rs[d], hv(h)], kte[d][:, hk(h)], TN_DIMS,
                                preferred_element_type=F32) for h in heads] for d in dirs]
        inter = [[lax.dot_general(qd[d][rs[d], hk(h)], stb[d][:, hk(h)], NT_DIMS,
                                  preferred_element_type=F32) for h in heads] for d in dirs]
        intra = [[jnp.dot(jnp.where(masks[d], att[d][h], 0.0).astype(BF16), vb[d][rs[d], hv(h)],
                          preferred_element_type=F32) for h in heads] for d in dirs]
        for d in dirs:
            st_refs[d][...] = st[d] * jnp.exp(cl[d]) + jnp.concatenate(upd[d], axis=1)
            outs[d][chunk[d]] = jnp.concatenate([intra[d][h] + inter[d][h] for h in heads], axis=1)
    return [jnp.concatenate(o, axis=0) for o in outs]


def _gla_body(qf, kf, vf, af, qb, kb, vb, ab, wa2, ba, tri, s0f, s0b,
              of, ob, sf_out, sb_out, sf_scr, sb_scr):
    @pl.when(pl.program_id(1) == 0)
    def _():
        sf_scr[...] = s0f[0]
        sb_scr[...] = s0b[0]

    o_f, o_b = _gla_pair([qf[0], qb[0]], [kf[0], kb[0]], [vf[0], vb[0]],
                         [af[0][:, :GLA_RANK], ab[0][:, GLA_RANK:2 * GLA_RANK]],
                         wa2, ba, tri, [sf_scr, sb_scr])
    of[0] = o_f
    ob[0] = o_b
    sf_out[0] = sf_scr[...]
    sb_out[0] = sb_scr[...]


def _gla(p, wa2, ba, tri, s0f, s0b, tt):
    b, s, _ = p.shape
    n = s // tt
    a_blk = (2 * GLA_K_W + 2 * GLA_V_W) // GLA_A_PAD

    def specs(rev):
        t = (lambda j: n - 1 - j) if rev else (lambda j: j)
        return [
            pl.BlockSpec((1, tt, GLA_K_W), lambda bb, j: (bb, t(j), 0)),
            pl.BlockSpec((1, tt, GLA_K_W), lambda bb, j: (bb, t(j), 1)),
            pl.BlockSpec((1, tt, GLA_V_W), lambda bb, j: (bb, t(j), 1)),
            pl.BlockSpec((1, tt, GLA_A_PAD), lambda bb, j: (bb, t(j), a_blk)),
        ]

    st_spec = pl.BlockSpec((1, GLA_DV, GLA_K_W), lambda bb, j: (bb, 0, 0))
    st_shape = jax.ShapeDtypeStruct((b, GLA_DV, GLA_K_W), F32)
    return pl.pallas_call(
        _gla_body,
        grid=(b, n),
        in_specs=specs(False) + specs(True) + [
            pl.BlockSpec((2, GLA_RANK, GLA_K_W), lambda bb, j: (0, 0, 0)),
            pl.BlockSpec((2, 1, GLA_K_W), lambda bb, j: (0, 0, 0)),
            pl.BlockSpec((2, tt, tt), lambda bb, j: (0, 0, 0)),
            st_spec, st_spec,
        ],
        out_specs=[
            pl.BlockSpec((1, tt, GLA_V_W), lambda bb, j: (bb, j, 0)),
            pl.BlockSpec((1, tt, GLA_V_W), lambda bb, j: (bb, n - 1 - j, 0)),
            st_spec, st_spec,
        ],
        out_shape=[
            jax.ShapeDtypeStruct((b, s, GLA_V_W), F32),
            jax.ShapeDtypeStruct((b, s, GLA_V_W), F32),
            st_shape, st_shape,
        ],
        scratch_shapes=[pltpu.VMEM((GLA_DV, GLA_K_W), F32), pltpu.VMEM((GLA_DV, GLA_K_W), F32)],
        compiler_params=_params("parallel", "arbitrary"),
        name="gla",
    )(p, p, p, p, p, p, p, p, wa2, ba, tri, s0f, s0b)


def _gla_tri(tt):
    t = np.arange(tt)
    same = (t[:, None] // GLA_CHUNK) == (t[None, :] // GLA_CHUNK)
    fwd = same & (t[None, :] <= t[:, None])
    bwd = same & (t[None, :] >= t[:, None])
    return jnp.asarray(np.stack([fwd, bwd]).astype(np.float32))


def _hy_filter_body(zt_ref, w1t_ref, b1_ref, fr_ref, w2t_ref, b2_ref, w3t_ref, dec_ref,
                    f_ref, b_ref, h2_scr):
    @pl.when((pl.program_id(0) == 0) & (pl.program_id(1) == 0))
    def _():
        fr = fr_ref[...]
        h1 = jnp.sin(fr * (jnp.dot(w1t_ref[...], zt_ref[...], precision=HIGHEST,
                                   preferred_element_type=F32) + b1_ref[...]))
        h2_scr[...] = jnp.sin(fr * (jnp.dot(w2t_ref[...], h1, precision=HIGHEST,
                                            preferred_element_type=F32) + b2_ref[...]))

    h2 = h2_scr[...]
    tn = zt_ref[0:1, :]
    f = jnp.dot(w3t_ref[0, 0], h2, precision=HIGHEST, preferred_element_type=F32)
    f = f * jnp.exp(-tn * jnp.abs(dec_ref[0, 0]))
    b = jnp.dot(w3t_ref[0, 1], h2, precision=HIGHEST, preferred_element_type=F32)
    b = b * jnp.exp(-tn * jnp.abs(dec_ref[0, 1]))
    lane = lax.broadcasted_iota(jnp.int32, b.shape, 1)
    b = jnp.where(lane == 0, 0.0, b)
    den = (jnp.sum(jnp.abs(f), axis=-1, keepdims=True)
           + jnp.sum(jnp.abs(b), axis=-1, keepdims=True) + EPS)
    f_ref[0] = f / den
    b_ref[0] = b / den


def _hy_filters(n, w1, b1, fr, w2, b2, w3, dec):
    t = jnp.arange(n, dtype=F32)
    t_norm = t / n
    bands = jnp.linspace(1e-4, HY_BANDS - 1, HY_BANDS, dtype=F32)
    phase = (2 * math.pi / n) * t[:, None] * bands[None, :]
    z = jnp.concatenate([t_norm[:, None], jnp.cos(phase), -jnp.sin(phase)], axis=-1)
    zt = jnp.pad(z.T, ((0, HY_EMB_PAD - HY_EMB), (0, 0)))
    w1t = jnp.pad(w1.T, ((0, 0), (0, HY_EMB_PAD - HY_EMB)))
    w3t = w3.T.reshape(HY_ORDER, 2, HY_WIDTH, HY_FFN)
    ct = 128
    col = lambda v: v.reshape(HY_FFN, 1)
    full = lambda shape: pl.BlockSpec(shape, lambda o, c: (0,) * len(shape))
    out_spec = pl.BlockSpec((1, ct, n), lambda o, c: (o, c, 0))
    out_shape = jax.ShapeDtypeStruct((HY_ORDER, HY_WIDTH, n), F32)
    return pl.pallas_call(
        _hy_filter_body,
        grid=(HY_ORDER, HY_WIDTH // ct),
        in_specs=[
            full((HY_EMB_PAD, n)), full((HY_FFN, HY_EMB_PAD)), full((HY_FFN, 1)), full((HY_FFN, 1)),
            full((HY_FFN, HY_FFN)), full((HY_FFN, 1)),
            pl.BlockSpec((1, 2, ct, HY_FFN), lambda o, c: (o, 0, c, 0)),
            pl.BlockSpec((1, 2, ct, 1), lambda o, c: (o, 0, c, 0)),
        ],
        out_specs=[out_spec, out_spec],
        out_shape=[out_shape, out_shape],
        scratch_shapes=[pltpu.VMEM((HY_FFN, n), F32)],
        compiler_params=_params("arbitrary", "arbitrary"),
        name="hy_filter",
    )(zt, w1t, col(b1), col(fr), w2.T, col(b2), w3t, dec.reshape(HY_ORDER, 2, HY_WIDTH, 1))


def _dft_consts(n1):
    n = n1 * LANES
    half = n1 // 2
    g = DFT_GROUP
    rows = g * n1
    i1 = np.arange(half)
    g1k = np.zeros((2 * rows, g * half))
    for r in range(rows):
        c, k1 = divmod(r, n1)
        tau, j = divmod(r, SUBLANES)
        ang = 2 * np.pi * k1 * i1 / n1
        g1k[2 * SUBLANES * tau + j, i1 * g + c] = np.cos(ang)
        g1k[2 * SUBLANES * tau + SUBLANES + j, i1 * g + c] = -np.sin(ang)
    gik = g1k.T
    k1 = np.arange(n1)[:, None]
    i2 = np.arange(LANES)[None, :]
    at = 2 * np.pi * k1 * i2 / n
    twr = np.tile(np.cos(at), (g, 1))
    twi = np.tile(-np.sin(at), (g, 1))
    a2 = 2 * np.pi * np.arange(LANES)[:, None] * np.arange(LANES)[None, :] / LANES
    cplx = lambda fr, fi: np.block([[fr, fi], [-fi, fr]])
    f2 = cplx(np.cos(a2), -np.sin(a2))
    f2i = cplx(np.cos(a2), np.sin(a2))
    bf = lambda m: jnp.asarray(m, dtype=BF16)
    return dict(g1k=bf(g1k), twr=jnp.asarray(twr, F32), twi=jnp.asarray(twi, F32),
                f2=bf(f2), f2i=bf(f2i), gik=bf(gik))


def _split_tiles(a):
    nt = a.shape[0] // (2 * SUBLANES)
    re = jnp.concatenate([a[2 * SUBLANES * t:2 * SUBLANES * t + SUBLANES] for t in range(nt)], axis=0)
    im = jnp.concatenate([a[2 * SUBLANES * t + SUBLANES:2 * SUBLANES * (t + 1)] for t in range(nt)],
                         axis=0)
    return re, im


def _merge_tiles(re, im):
    nt = re.shape[0] // SUBLANES
    parts = []
    for t in range(nt):
        parts += [re[SUBLANES * t:SUBLANES * (t + 1)], im[SUBLANES * t:SUBLANES * (t + 1)]]
    return jnp.concatenate(parts, axis=0)


def _dot_lane_pairs(m, xs):
    out = []
    for i in range(0, len(xs) - 1, 2):
        r = jnp.dot(m, jnp.concatenate([xs[i], xs[i + 1]], axis=1), preferred_element_type=F32)
        out += [r[:, :LANES], r[:, LANES:]]
    if len(xs) % 2:
        out.append(jnp.dot(m, xs[-1], preferred_element_type=F32))
    return out


def _fwd_dft(xs, g1k, twr, twi, f2):
    half = xs[0].shape[1] // LANES
    rows = [jnp.concatenate([x[:, i * LANES:(i + 1) * LANES] for i in range(half)], axis=0)
            for x in xs]
    a = _dot_lane_pairs(g1k, [r.astype(BF16) for r in rows])
    t = []
    for v in a:
        ar, ai = _split_tiles(v)
        t.append(jnp.concatenate([ar * twr - ai * twi, ar * twi + ai * twr], axis=1).astype(BF16))
    p = [jnp.dot(v, f2, preferred_element_type=F32) for v in t]
    return [(v[:, :LANES], v[:, LANES:]) for v in p]


def _inv_dft(ys, twr, twi, f2i, gik):
    p = [jnp.dot(jnp.concatenate([yr, yi], axis=1).astype(BF16), f2i, preferred_element_type=F32)
         for yr, yi in ys]
    t = []
    for v in p:
        br, bi = v[:, :LANES], v[:, LANES:]
        t.append(_merge_tiles(br * twr + bi * twi, bi * twr - br * twi).astype(BF16))
    y = _dot_lane_pairs(gik, t)
    half = y[0].shape[0] // DFT_GROUP
    return [jnp.concatenate([v[i * DFT_GROUP:(i + 1) * DFT_GROUP] for i in range(half)], axis=1)
            for v in y]


def _hy_spec_body(f_ref, b_ref, g1k, twr, twi, f2, hr_ref, hi_ref, *, groups, n1):
    scale = 1.0 / (n1 * LANES)

    def body(t, carry):
        gs = [2 * t, 2 * t + 1]
        rows = [pl.ds(pl.multiple_of(g * DFT_GROUP, DFT_GROUP), DFT_GROUP) for g in gs]
        slabs = []
        for r in rows:
            slabs += [f_ref[r, :], b_ref[r, :]]
        z = _fwd_dft(slabs, g1k[...], twr[...], twi[...], f2[...])
        for i, g in enumerate(gs):
            (zfr, zfi), (zbr, zbi) = z[2 * i], z[2 * i + 1]
            orow = pl.ds(pl.multiple_of(g * (DFT_GROUP * n1), DFT_GROUP * n1), DFT_GROUP * n1)
            hr_ref[orow, :] = (zfr + zbr) * scale
            hi_ref[orow, :] = (zfi - zbi) * scale
        return carry

    lax.fori_loop(0, groups // 2, body, 0)


def _const_spec(arr, ngrid):
    nd = arr.ndim
    return pl.BlockSpec(arr.shape, lambda *_: (0,) * nd)


def _hy_spec(f, b, consts, n1):
    chans, n_half = f.shape
    ct = 64
    groups = ct // DFT_GROUP
    cs = [consts[k] for k in ("g1k", "twr", "twi", "f2")]
    in_spec = pl.BlockSpec((ct, n_half), lambda c: (c, 0))
    out_spec = pl.BlockSpec((ct * n1, LANES), lambda c: (c, 0))
    out_shape = jax.ShapeDtypeStruct((chans * n1, LANES), F32)
    return pl.pallas_call(
        functools.partial(_hy_spec_body, groups=groups, n1=n1),
        grid=(chans // ct,),
        in_specs=[in_spec, in_spec] + [_const_spec(c, 1) for c in cs],
        out_specs=[out_spec, out_spec],
        out_shape=[out_shape, out_shape],
        compiler_params=_params("parallel"),
        name="hy_spec",
    )(f, b, *cs)


def _hy_main_body(u1_ref, u2_ref, uz_ref, p_ref, hr_ref, hi_ref, g1k, twr, twi, f2, f2i, gik,
                  o_ref, *, groups, n1):
    n_half = u1_ref.shape[2]
    shape = (DFT_GROUP, n_half)
    lane = lax.broadcasted_iota(jnp.int32, shape, 1)
    t_first = lane == 0
    t_last = lane == n_half - 1

    def prev(x):
        return jnp.where(t_first, 0.0, pltpu.roll(x, 1, 1))

    def nxt(x):
        return jnp.where(t_last, 0.0, pltpu.roll(x, n_half - 1, 1))

    def body(t, carry):
        gs = [t * HY_INTERLEAVE + i for i in range(HY_INTERLEAVE)]
        cs = [pl.ds(pl.multiple_of(g * DFT_GROUP, DFT_GROUP), DFT_GROUP) for g in gs]
        hrows = [pl.ds(pl.multiple_of(g * (DFT_GROUP * n1), DFT_GROUP * n1), DFT_GROUP * n1)
                 for g in gs]
        prs = [p_ref[c, :] for c in cs]
        col = lambda i, j: prs[i][:, j:j + 1]

        def sconv(ref, i, j):
            x = ref[0, cs[i], :]
            return prev(x) * col(i, j) + x * col(i, j + 1) + nxt(x) * col(i, j + 2) + col(i, j + 3)

        idx = range(HY_INTERLEAVE)

        def conv(vs, o):
            zs = _fwd_dft(vs, g1k[...], twr[...], twi[...], f2[...])
            ys = []
            for i, (zr, zi) in enumerate(zs):
                hr = hr_ref[o, hrows[i], :]
                hi = hi_ref[o, hrows[i], :]
                ys.append((zr * hr - zi * hi, zr * hi + zi * hr))
            return _inv_dft(ys, twr[...], twi[...], f2i[...], gik[...])

        zs = [sconv(uz_ref, i, 8) for i in idx]
        cv = conv(zs, 0)
        zs = [sconv(u1_ref, i, 0) * (cv[i] + col(i, 12) * zs[i]) for i in idx]
        cv = conv(zs, 1)
        for i in idx:
            o_ref[0, cs[i], :] = sconv(u2_ref, i, 4) * (cv[i] + col(i, 13) * zs[i])
        return carry

    lax.fori_loop(0, groups // HY_INTERLEAVE, body, 0)


def _hy_main(ut, prm, hr, hi, consts, n1):
    b, _, n_half = ut.shape
    ct = 64
    nct = HY_WIDTH // ct
    cs = [consts[k] for k in ("g1k", "twr", "twi", "f2", "f2i", "gik")]
    u_spec = lambda sec: pl.BlockSpec((1, ct, n_half), lambda c, bb: (bb, sec * nct + c, 0))
    h_spec = pl.BlockSpec((HY_ORDER, ct * n1, LANES), lambda c, bb: (0, c, 0))
    return pl.pallas_call(
        functools.partial(_hy_main_body, groups=ct // DFT_GROUP, n1=n1),
        grid=(nct, b),
        in_specs=[u_spec(0), u_spec(1), u_spec(2),
                  pl.BlockSpec((ct, 16), lambda c, bb: (c, 0)),
                  h_spec, h_spec] + [_const_spec(c, 2) for c in cs],
        out_specs=pl.BlockSpec((1, ct, n_half), lambda c, bb: (bb, c, 0)),
        out_shape=jax.ShapeDtypeStruct((b, HY_WIDTH, n_half), F32),
        compiler_params=_params("parallel", "parallel"),
        name="hy_main",
    )(ut, ut, ut, prm, hr, hi, *cs)


def _hyc_consts(n):
    nn = 2 * n
    a = 2 * np.pi * np.arange(n)[:, None] * np.arange(nn)[None, :] / nn
    fc = np.concatenate([np.cos(a), -np.sin(a)], axis=1)
    fi = np.concatenate([np.cos(a.T), -np.sin(a.T)], axis=0)
    return jnp.asarray(fc, BF16), jnp.asarray(fi, BF16)


def _hyc_spec_body(f_ref, b_ref, fc_ref, hr_ref, hi_ref):
    nn = hr_ref.shape[-1]
    zf = jnp.dot(f_ref[...].astype(BF16), fc_ref[...], preferred_element_type=F32)
    zb = jnp.dot(b_ref[...].astype(BF16), fc_ref[...], preferred_element_type=F32)
    hr_ref[...] = (zf[:, :nn] + zb[:, :nn]) * (1.0 / nn)
    hi_ref[...] = (zf[:, nn:] - zb[:, nn:]) * (1.0 / nn)


def _hyc_spec(f, b, fc):
    rows, n = f.shape
    out_shape = jax.ShapeDtypeStruct((rows, 2 * n), F32)
    return pl.pallas_call(
        _hyc_spec_body,
        out_shape=[out_shape, out_shape],
        compiler_params=pltpu.CompilerParams(vmem_limit_bytes=VMEM_LIMIT_BYTES),
        name="hyc_spec",
    )(f, b, fc)


def _hyc_main_body(u1_ref, u2_ref, uz_ref, p_ref, hr_ref, hi_ref, fc_ref, fi_ref, o_ref):
    shape = u1_ref.shape[1:]
    n = shape[1]
    nn = 2 * n
    lane = lax.broadcasted_iota(jnp.int32, shape, 1)
    pr = p_ref[...]
    col = lambda j: pr[:, j:j + 1]

    def sconv(ref, j):
        x = ref[0]
        p = jnp.where(lane == 0, 0.0, pltpu.roll(x, 1, 1))
        q = jnp.where(lane == n - 1, 0.0, pltpu.roll(x, n - 1, 1))
        return p * col(j) + x * col(j + 1) + q * col(j + 2) + col(j + 3)

    def conv(v, o):
        s = jnp.dot(v.astype(BF16), fc_ref[...], preferred_element_type=F32)
        zr, zi = s[:, :nn], s[:, nn:]
        hr, hi = hr_ref[o], hi_ref[o]
        y = jnp.concatenate([zr * hr - zi * hi, zr * hi + zi * hr], axis=1).astype(BF16)
        return jnp.dot(y, fi_ref[...], preferred_element_type=F32)

    x1 = sconv(u1_ref, 0)
    x2 = sconv(u2_ref, 4)
    z = sconv(uz_ref, 8)
    z = x1 * (conv(z, 0) + col(12) * z)
    o_ref[0] = x2 * (conv(z, 1) + col(13) * z)


def _hyc_main(ut, prm, hr, hi, fc, fi, nb, n):
    ct = 128
    nct = HY_WIDTH // ct
    u_spec = lambda sec: pl.BlockSpec((1, ct, n), lambda bb, c: (0, sec * nct + c, bb))
    h_spec = pl.BlockSpec((HY_ORDER, ct, 2 * n), lambda bb, c: (0, c, 0))
    return pl.pallas_call(
        _hyc_main_body,
        grid=(nb, nct),
        in_specs=[u_spec(0), u_spec(1), u_spec(2),
                  pl.BlockSpec((ct, 16), lambda bb, c: (c, 0)),
                  h_spec, h_spec, _const_spec(fc, 2), _const_spec(fi, 2)],
        out_specs=pl.BlockSpec((1, ct, n), lambda bb, c: (0, c, bb)),
        out_shape=jax.ShapeDtypeStruct((1, HY_WIDTH, nb * n), F32),
        compiler_params=_params("parallel", "parallel"),
        name="hyc_main",
    )(ut, ut, ut, prm, hr, hi, fc, fi)


def _merge_body(x_ref, m_ref, g_ref, wbrg, yh_ref, of_ref, ob_ref, og_ref, gn_ref, ya_ref,
                wh, wg, wa, wo, o_ref):
    x = x_ref[0]
    m = m_ref[0]
    h = _norm_mod(x, g_ref[...], m[0:1], m[1:2]).astype(BF16)
    gates = jnp.dot(h, wbrg[...], preferred_element_type=F32)
    yh = yh_ref[0].T
    o = of_ref[0] + ob_ref[0]
    parts = []
    for hh in range(GLA_HEADS):
        t = o[:, hh * GLA_DV:(hh + 1) * GLA_DV]
        parts.append(t * lax.rsqrt(jnp.mean(t * t, axis=-1, keepdims=True) + EPS) * gn_ref[...])
    og = og_ref[0]
    yg = jnp.concatenate(parts, axis=1) * (og * _sigmoid(og))
    d = D_MODEL
    mm = (_sigmoid(gates[:, :d]) * jnp.dot(yh.astype(BF16), wh[...], preferred_element_type=F32)
          + _sigmoid(gates[:, d:2 * d]) * jnp.dot(yg.astype(BF16), wg[...], preferred_element_type=F32)
          + _sigmoid(gates[:, 2 * d:]) * jnp.dot(ya_ref[0].T.astype(BF16), wa[...],
                                                  preferred_element_type=F32))
    out = jnp.dot(mm.astype(BF16), wo[...], preferred_element_type=F32)
    o_ref[0] = x + m[2:3] * out


def _merge(x, mods, g, wbrg, yht, of, ob, pg, gn, yat, wh, wg, wa, wo, tm):
    bm, r, _ = x.shape
    row = lambda w: pl.BlockSpec((1, tm, w), lambda b, i: (b, i, 0))
    full = lambda a: pl.BlockSpec(a.shape, lambda b, i: (0,) * a.ndim)
    og_blk = (2 * GLA_K_W + GLA_V_W) // GLA_V_W
    return pl.pallas_call(
        _merge_body,
        grid=(bm, r // tm),
        in_specs=[
            row(D_MODEL),
            pl.BlockSpec((1, 6, D_MODEL), lambda b, i: (b, 0, 0)),
            full(g), full(wbrg),
            pl.BlockSpec((1, HY_WIDTH, tm), lambda b, i: (b, 0, i)),
            row(GLA_V_W), row(GLA_V_W),
            pl.BlockSpec((1, tm, GLA_V_W), lambda b, i: (b, i, og_blk)),
            full(gn),
            pl.BlockSpec((1, ATT_Q_W, tm), lambda b, i: (b, 0, i)),
            full(wh), full(wg), full(wa), full(wo),
        ],
        out_specs=row(D_MODEL),
        out_shape=jax.ShapeDtypeStruct(x.shape, F32),
        compiler_params=_params("parallel", "parallel"),
        name="merge",
    )(x, mods, g, wbrg, yht, of, ob, pg, gn, yat, wh, wg, wa, wo)


def _route(logits, rb):
    scores = _sigmoid(logits)
    sel = scores + rb
    lane = lax.broadcasted_iota(jnp.int32, sel.shape, 1).astype(F32)
    neg = -jnp.inf

    def top2(v):
        m1 = jnp.max(v, axis=-1, keepdims=True)
        i1 = jnp.min(jnp.where(v == m1, lane, float(N_EXPERTS)), axis=-1, keepdims=True)
        v2 = jnp.where(lane == i1, neg, v)
        m2 = jnp.max(v2, axis=-1, keepdims=True)
        i2 = jnp.min(jnp.where(v2 == m2, lane, float(N_EXPERTS)), axis=-1, keepdims=True)
        return m1, m2, i1, i2

    group_of = jnp.floor(lane * (1.0 / EXPERTS_PER_GROUP))
    best = None
    best_g = None
    for g in range(N_GROUPS):
        m1, m2, _, _ = top2(jnp.where(group_of == float(g), sel, neg))
        gs = m1 + m2
        if best is None:
            best, best_g = gs, jnp.zeros_like(gs)
        else:
            better = gs > best
            best_g = jnp.where(better, float(g), best_g)
            best = jnp.where(better, gs, best)
    _, _, i1, i2 = top2(jnp.where(group_of == best_g, sel, neg))
    w = jnp.where((lane == i1) | (lane == i2), scores, 0.0)
    return w / jnp.sum(w, axis=-1, keepdims=True), best_g


def _moe_body(x_ref, m_ref, g_ref, rw_ref, rb_ref, wg_ref, wu_ref, wd_ref, o_ref,
              h_scr, ghi_scr, glo_scr, grp_scr, pos_scr, grpt_scr, post_scr, acc_scr):
    g = pl.program_id(2)
    gf = g.astype(F32)
    tm = x_ref.shape[1]
    sub = MOE_SUB
    lane = lax.broadcasted_iota(jnp.int32, (tm, LANES), 1).astype(F32)

    @pl.when(g == 0)
    def _():
        m = m_ref[0]
        h = _norm_mod(x_ref[0], g_ref[...], m[3:4], m[4:5])
        h_hi = h.astype(BF16)
        h_scr[...] = h_hi
        h_lo = (h - h_hi.astype(F32)).astype(BF16)
        p2 = jnp.dot(h_hi, rw_ref[...], preferred_element_type=F32)
        p1 = jnp.dot(h_lo, rw_ref[:, :LANES], preferred_element_type=F32)
        ne = N_EXPERTS
        logits = p2[:, :ne] + p2[:, ne:2 * ne] + p1[:, :ne]
        gates, grp = _route(logits, rb_ref[...])
        ghi = gates.astype(BF16)
        ghi_scr[...] = ghi
        glo_scr[...] = (gates - ghi.astype(F32)).astype(BF16)
        grp_b = jnp.broadcast_to(grp, (tm, LANES))
        onehot = (lane == grp_b).astype(BF16)
        ranks = []
        for rb_ in range(tm // sub):
            r = lax.broadcasted_iota(jnp.int32, (sub, tm), 0) + rb_ * sub
            c = lax.broadcasted_iota(jnp.int32, (sub, tm), 1)
            ranks.append(jnp.dot((c < r).astype(BF16), onehot, preferred_element_type=F32))
        rank = jnp.concatenate(ranks, axis=0)
        pos = jnp.sum(jnp.where(lane == grp_b, rank, 0.0), axis=-1, keepdims=True)
        pos_b = jnp.broadcast_to(pos, (tm, LANES))
        grp_scr[...] = grp_b
        pos_scr[...] = pos_b
        grpt_scr[...] = grp_b.T
        post_scr[...] = pos_b.T
        acc_scr[...] = jnp.zeros_like(acc_scr)

    in_group = grp_scr[...] == gf
    n_g = jnp.sum(jnp.where(in_group[:, 0:1], 1.0, 0.0)).astype(jnp.int32)
    n_sub = (n_g + (sub - 1)) // sub
    d_iota = lax.broadcasted_iota(jnp.int32, (sub, tm), 0).astype(F32)
    lane16 = lax.broadcasted_iota(jnp.int32, (sub, N_EXPERTS), 1)

    def body(k, carry):
        kf = (k * sub).astype(F32)
        p = ((grpt_scr[...] == gf) & (post_scr[...] - kf == d_iota)).astype(BF16)
        hd = jnp.dot(p, h_scr[...], preferred_element_type=F32).astype(BF16)
        gd = (jnp.dot(p, ghi_scr[...], preferred_element_type=F32)
              + jnp.dot(p, glo_scr[...], preferred_element_type=F32))
        experts = range(EXPERTS_PER_GROUP)
        a = [jnp.dot(hd, wg_ref[j], preferred_element_type=F32) for j in experts]
        b = [jnp.dot(hd, wu_ref[j], preferred_element_type=F32) for j in experts]
        hid = [(a[j] * _sigmoid(a[j]) * b[j]).astype(BF16) for j in experts]
        dn = [jnp.dot(hid[j], wd_ref[j], preferred_element_type=F32) for j in experts]
        y = jnp.zeros((sub, D_MODEL), F32)
        for j in experts:
            ge = jnp.sum(jnp.where(lane16 == g * EXPERTS_PER_GROUP + j, gd, 0.0),
                         axis=-1, keepdims=True)
            y = y + ge * dn[j]
        pt = (in_group & (pos_scr[...] - kf == lane)).astype(BF16)
        acc_scr[...] += jnp.dot(pt, y.astype(BF16), preferred_element_type=F32)
        return carry

    lax.fori_loop(0, n_sub, body, 0)

    @pl.when(g == N_GROUPS - 1)
    def _():
        o_ref[0] = x_ref[0] + m_ref[0][5:6] * acc_scr[...]


def _moe(x, mods, g, rw, rb, wg, wu, wd, tm):
    bm, r, _ = x.shape
    row = pl.BlockSpec((1, tm, D_MODEL), lambda b, i, e: (b, i, 0))
    return pl.pallas_call(
        _moe_body,
        grid=(bm, r // tm, N_GROUPS),
        in_specs=[
            row,
            pl.BlockSpec((1, 6, D_MODEL), lambda b, i, e: (b, 0, 0)),
            pl.BlockSpec((1, D_MODEL), lambda b, i, e: (0, 0)),
            pl.BlockSpec((D_MODEL, 2 * LANES), lambda b, i, e: (0, 0)),
            pl.BlockSpec((1, N_EXPERTS), lambda b, i, e: (0, 0)),
            pl.BlockSpec((EXPERTS_PER_GROUP, D_MODEL, D_EXPERT), lambda b, i, e: (e, 0, 0)),
            pl.BlockSpec((EXPERTS_PER_GROUP, D_MODEL, D_EXPERT), lambda b, i, e: (e, 0, 0)),
            pl.BlockSpec((EXPERTS_PER_GROUP, D_EXPERT, D_MODEL), lambda b, i, e: (e, 0, 0)),
        ],
        out_specs=row,
        out_shape=jax.ShapeDtypeStruct(x.shape, F32),
        scratch_shapes=[
            pltpu.VMEM((tm, D_MODEL), BF16),
            pltpu.VMEM((tm, N_EXPERTS), BF16),
            pltpu.VMEM((tm, N_EXPERTS), BF16),
            pltpu.VMEM((tm, LANES), F32),
            pltpu.VMEM((tm, LANES), F32),
            pltpu.VMEM((LANES, tm), F32),
            pltpu.VMEM((LANES, tm), F32),
            pltpu.VMEM((tm, D_MODEL), F32),
        ],
        compiler_params=_params("parallel", "parallel", "arbitrary"),
        name="moe",
    )(x, mods, g, rw, rb, wg, wu, wd)


def _rope_tables(n_tokens):
    rows = n_tokens // GRID_W
    row = jnp.broadcast_to(jnp.arange(rows)[:, None], (rows, GRID_W)).reshape(-1).astype(F32)
    col = jnp.broadcast_to(jnp.arange(GRID_W)[None, :], (rows, GRID_W)).reshape(-1).astype(F32)
    inv_freq = ROPE_THETA ** (-jnp.arange(ROPE_PAIRS_PER_AXIS, dtype=F32) / ROPE_PAIRS_PER_AXIS)
    ang = jnp.concatenate([row[:, None] * inv_freq, col[:, None] * inv_freq], axis=-1)
    cos, sin = jnp.cos(ang), jnp.sin(ang)
    cos2, sin2 = jnp.concatenate([cos, cos], axis=-1), jnp.concatenate([-sin, sin], axis=-1)
    return cos2, sin2, cos2.T, sin2.T


def _split_w_in(w):
    sizes = (ATT_KV_W, ATT_KV_W, GLA_K_W, GLA_V_W, 2 * GLA_RANK,
             ATT_Q_W, GLA_K_W, GLA_V_W, 3 * HY_WIDTH, 3 * D_MODEL)
    cuts = [int(v) for v in np.cumsum(sizes)[:-1]]
    a_k, a_v, g_k, g_v, g_a, a_q, g_q, g_og, hy_u, br_g = jnp.split(w, cuts, axis=1)
    perm = np.concatenate([np.arange(0, HEAD_DIM, 2), np.arange(1, HEAD_DIM, 2)])
    perm_q = np.concatenate([h * HEAD_DIM + perm for h in range(ATT_HEADS)])
    perm_k = np.concatenate([h * HEAD_DIM + perm for h in range(ATT_KV_HEADS)])
    w_qv_t = jnp.concatenate([a_q[:, perm_q], a_v], axis=1).T.astype(BF16)
    w_k = a_k[:, perm_k].astype(BF16)
    pad = jnp.zeros((D_MODEL, GLA_A_PAD - 2 * GLA_RANK), w.dtype)
    w_gla = jnp.concatenate([g_q, g_k, g_v, g_og, g_a, pad], axis=1).astype(BF16)
    return w_qv_t, w_k, w_gla, hy_u.T.astype(BF16), br_g.astype(BF16), perm


def kernel(x, c, ctx, c_ctx, w_mod, b_mod, norm1_g, norm2_g, w_in, q_norm_g, k_norm_g, gla_wa2, gla_ba, gla_norm_g, hy_conv_w, hy_conv_b, hy_pos_w1, hy_pos_b1, hy_sin_freq, hy_pos_w2, hy_pos_b2, hy_pos_w3, hy_decay, hy_skip, w_br_hy, w_br_gla, w_br_att, w_out, router_w, router_b, moe_w_gate, moe_w_up, moe_w_down):
    nb, seq, d = x.shape
    nctx = ctx.shape[1]
    n1_lat = 2 * seq // LANES

    c_all = jnp.concatenate([c, c_ctx[None, :], jnp.zeros((16 - nb - 1, d), F32)], axis=0)
    mods_all = _mods(c_all, w_mod, b_mod)

    rope_tabs = _rope_tables(seq)
    consts = _dft_consts(n1_lat)
    fc, fi = _hyc_consts(nctx)
    tri_lat = _gla_tri(256)
    zero_state = jnp.zeros((nb, GLA_DV, GLA_K_W), F32)
    rb = router_b.reshape(1, N_EXPERTS)
    rw_hi = router_w.astype(BF16)
    rw_lo = (router_w - rw_hi.astype(F32)).astype(BF16)
    rw2 = jnp.concatenate([rw_hi, rw_lo, jnp.zeros((d, 2 * LANES - 2 * N_EXPERTS), BF16)], axis=1)

    xc = ctx.reshape(1, nb * nctx, d)
    for l in range(DEPTH):
        last = l == DEPTH - 1
        mods_lat = mods_all[l, :nb].reshape(nb, 6, d)
        mods_ctx = mods_all[l, nb:nb + 1].reshape(1, 6, d)
        g1 = norm1_g[l].reshape(1, d)
        g2 = norm2_g[l].reshape(1, d)
        w_qv_t, w_k, w_gla, w_hyt, w_brg, perm = _split_w_in(w_in[l])
        qg = (q_norm_g[l][perm] * Q_SCALE).reshape(HEAD_DIM, 1)
        kg = k_norm_g[l][perm].reshape(1, HEAD_DIM)
        wa2 = gla_wa2[l]
        ba = gla_ba[l].reshape(2, 1, GLA_K_W)
        gn = gla_norm_g[l].reshape(1, GLA_DV)
        wh, wg, wa, wo = (w_br_hy[l].astype(BF16), w_br_gla[l].astype(BF16),
                          w_br_att[l].astype(BF16), w_out[l].astype(BF16))
        cw, cb = hy_conv_w[l], hy_conv_b[l]
        sec = lambda s: [cw[0, s], cw[1, s], cw[2, s], cb[s]]
        w_ = HY_WIDTH
        prm = jnp.stack(sec(slice(0, w_)) + sec(slice(w_, 2 * w_)) + sec(slice(2 * w_, 3 * w_))
                        + [hy_skip[l, 0], hy_skip[l, 1], jnp.zeros((w_,), F32), jnp.zeros((w_,), F32)],
                        axis=1)
        filt_args = (hy_pos_w1[l], hy_pos_b1[l], hy_sin_freq[l], hy_pos_w2[l], hy_pos_b2[l],
                     hy_pos_w3[l], hy_decay[l])

        qt, k, vt = _proj_att(x, mods_lat, g1, w_qv_t, w_k, qg, kg, rope_tabs, 512)
        qct, kc, vct = _proj_att(xc, mods_ctx, g1, w_qv_t, w_k, qg, kg, None, 256)
        pg = _proj_plain(x, mods_lat, g1, w_gla, 512)
        pgc = _proj_plain(xc, mods_ctx, g1, w_gla, 256)
        ut = _proj_t(x, mods_lat, g1, w_hyt, 512)

        y_att = _attn(qt, k, vt, kc, vct, nb, 256)

        cof, cob, s_f, s_b = _gla(pgc.reshape(nb, nctx, GLA_COLS), wa2, ba, _gla_tri(nctx),
                                  zero_state, zero_state, nctx)
        of, ob, _, _ = _gla(pg, wa2, ba, tri_lat, s_f, s_b, 256)

        ff, fb = _hy_filters(seq, *filt_args)
        hr, hi = _hy_spec(ff.reshape(-1, seq), fb.reshape(-1, seq), consts, n1_lat)
        hr = hr.reshape(HY_ORDER, HY_WIDTH * n1_lat, LANES)
        hi = hi.reshape(HY_ORDER, HY_WIDTH * n1_lat, LANES)
        y_hyt = _hy_main(ut, prm, hr, hi, consts, n1_lat)

        x = _merge(x, mods_lat, g1, w_brg, y_hyt, of, ob, pg, gn, y_att, wh, wg, wa, wo, 256)
        moe_w = (moe_w_gate[l].astype(BF16), moe_w_up[l].astype(BF16), moe_w_down[l].astype(BF16))
        x = _moe(x, mods_lat, g2, rw2, rb, *moe_w, 1024)

        if not last:
            yc_att = _attn(qct, None, None, kc, vct, nb, nctx)
            uct = _proj_t(xc, mods_ctx, g1, w_hyt, 256)
            cff, cfb = _hy_filters(nctx, *filt_args)
            chr_, chi = _hyc_spec(cff.reshape(-1, nctx), cfb.reshape(-1, nctx), fc)
            chr_ = chr_.reshape(HY_ORDER, HY_WIDTH, 2 * nctx)
            chi = chi.reshape(HY_ORDER, HY_WIDTH, 2 * nctx)
            yc_hyt = _hyc_main(uct, prm, chr_, chi, fc, fi, nb, nctx)
            xc = _merge(xc, mods_ctx, g1, w_brg, yc_hyt,
                        cof.reshape(1, nb * nctx, GLA_V_W), cob.reshape(1, nb * nctx, GLA_V_W),
                        pgc, gn, yc_att, wh, wg, wa, wo, 256)
            xc = _moe(xc, mods_ctx, g2, rw2, rb, *moe_w, min(1024, nb * nctx))
    return x
```

```python
import functools
import math

import numpy as np
import jax
import jax.numpy as jnp
from jax import lax
from jax.experimental import pallas as pl
from jax.experimental.pallas import tpu as pltpu

F32 = jnp.float32
BF16 = jnp.bfloat16
HIGHEST = lax.Precision.HIGHEST

D_MODEL = 1024
DEPTH = 2
GRID_W = 64
EPS = 1e-6

ATT_HEADS = 8
ATT_KV_HEADS = 2
ATT_GROUP = ATT_HEADS // ATT_KV_HEADS
HEAD_DIM = 128
ROPE_PAIRS_PER_AXIS = HEAD_DIM // 4
ROPE_THETA = 10000.0
Q_SCALE = HEAD_DIM ** -0.5 * math.log2(math.e)

GLA_HEADS = 4
GLA_DK = 64
GLA_DV = 128
GLA_RANK = 16
GLA_TAU = 16.0
GLA_CHUNK = 64
GLA_BATCH_ROWS = 4

HY_WIDTH = 512
HY_ORDER = 2
HY_BANDS = 16
HY_EMB = 1 + 2 * HY_BANDS
HY_EMB_PAD = 40
HY_FFN = 64

N_EXPERTS = 16
N_GROUPS = 4
EXPERTS_PER_GROUP = N_EXPERTS // N_GROUPS
D_EXPERT = 512

ATT_Q_W = ATT_HEADS * HEAD_DIM
ATT_KV_W = ATT_KV_HEADS * HEAD_DIM
GLA_K_W = GLA_HEADS * GLA_DK
GLA_V_W = GLA_HEADS * GLA_DV
GLA_A_PAD = 128
GLA_COLS = 2 * GLA_K_W + 2 * GLA_V_W + GLA_A_PAD
ATT_COLS = ATT_Q_W + 2 * ATT_KV_W

LANES = 128
SUBLANES = 8
MOE_SUB = LANES
VMEM_LIMIT_BYTES = 60 * 1024 * 1024

DFT_GROUP = SUBLANES
HY_INTERLEAVE = 4

NT_DIMS = (((1,), (1,)), ((), ()))
TN_DIMS = (((0,), (0,)), ((), ()))


def _params(*sem):
    return pltpu.CompilerParams(dimension_semantics=sem, vmem_limit_bytes=VMEM_LIMIT_BYTES)


def _sigmoid(x):
    return 1.0 / (1.0 + jnp.exp(-x))


def _norm_mod(x, g, shift, scale):
    ms = jnp.mean(x * x, axis=-1, keepdims=True)
    return (x * lax.rsqrt(ms + EPS) * g) * (1.0 + scale) + shift


def _mod_body(c_ref, w_ref, b_ref, o_ref):
    c = c_ref[...]
    s = c * _sigmoid(c)
    o_ref[0] = jnp.dot(s, w_ref[0], precision=HIGHEST, preferred_element_type=F32) + b_ref[0]


def _mods(c_all, w_mod, b_mod):
    tn = 512
    rows = c_all.shape[0]
    return pl.pallas_call(
        _mod_body,
        grid=(DEPTH, 6 * D_MODEL // tn),
        in_specs=[
            pl.BlockSpec((rows, D_MODEL), lambda l, j: (0, 0)),
            pl.BlockSpec((1, D_MODEL, tn), lambda l, j: (l, 0, j)),
            pl.BlockSpec((1, 1, tn), lambda l, j: (l, 0, j)),
        ],
        out_specs=pl.BlockSpec((1, rows, tn), lambda l, j: (l, 0, j)),
        out_shape=jax.ShapeDtypeStruct((DEPTH, rows, 6 * D_MODEL), F32),
        compiler_params=_params("parallel", "parallel"),
        name="mods",
    )(c_all, w_mod, b_mod.reshape(DEPTH, 1, 6 * D_MODEL))


def _proj_att_body(*refs, rope):
    if rope:
        (x_ref, m_ref, g_ref, wqv_ref, wk_ref, qg_ref, kg_ref, cos_ref, sin_ref, cost_ref, sint_ref,
         qt_ref, k_ref, vt_ref) = refs
    else:
        x_ref, m_ref, g_ref, wqv_ref, wk_ref, qg_ref, kg_ref, qt_ref, k_ref, vt_ref = refs
    m = m_ref[0]
    h = _norm_mod(x_ref[0], g_ref[...], m[0:1], m[1:2]).astype(BF16)
    pt = lax.dot_general(wqv_ref[...], h, NT_DIMS, preferred_element_type=F32)
    pk = jnp.dot(h, wk_ref[...], preferred_element_type=F32)
    half = HEAD_DIM // 2

    for i in range(ATT_HEADS):
        sl = slice(i * HEAD_DIM, (i + 1) * HEAD_DIM)
        t = pt[sl]
        t = t * lax.rsqrt(jnp.mean(t * t, axis=0, keepdims=True) + EPS) * qg_ref[...]
        if rope:
            t = t * cost_ref[...] + pltpu.roll(t, half, 0) * sint_ref[...]
        qt_ref[0, sl, :] = t.astype(BF16)
    for i in range(ATT_KV_HEADS):
        sl = slice(i * HEAD_DIM, (i + 1) * HEAD_DIM)
        t = pk[:, sl]
        t = t * lax.rsqrt(jnp.mean(t * t, axis=-1, keepdims=True) + EPS) * kg_ref[...]
        if rope:
            t = t * cos_ref[...] + pltpu.roll(t, half, 1) * sin_ref[...]
        k_ref[0, :, sl] = t.astype(BF16)
    vt_ref[0] = pt[ATT_Q_W:].astype(BF16)


def _proj_att(x, mods, g, wqv_t, wk, qg_col, kg, rope_tabs, tm):
    bm, r, _ = x.shape
    rope = rope_tabs is not None
    in_specs = [
        pl.BlockSpec((1, tm, D_MODEL), lambda b, i: (b, i, 0)),
        pl.BlockSpec((1, 6, D_MODEL), lambda b, i: (b, 0, 0)),
        pl.BlockSpec((1, D_MODEL), lambda b, i: (0, 0)),
        pl.BlockSpec((ATT_Q_W + ATT_KV_W, D_MODEL), lambda b, i: (0, 0)),
        pl.BlockSpec((D_MODEL, ATT_KV_W), lambda b, i: (0, 0)),
        pl.BlockSpec((HEAD_DIM, 1), lambda b, i: (0, 0)),
        pl.BlockSpec((1, HEAD_DIM), lambda b, i: (0, 0)),
    ]
    args = [x, mods, g, wqv_t, wk, qg_col, kg]
    if rope:
        in_specs += [pl.BlockSpec((tm, HEAD_DIM), lambda b, i: (i, 0))] * 2
        in_specs += [pl.BlockSpec((HEAD_DIM, tm), lambda b, i: (0, i))] * 2
        args += list(rope_tabs)
    return pl.pallas_call(
        functools.partial(_proj_att_body, rope=rope),
        grid=(bm, r // tm),
        in_specs=in_specs,
        out_specs=[
            pl.BlockSpec((1, ATT_Q_W, tm), lambda b, i: (b, 0, i)),
            pl.BlockSpec((1, tm, ATT_KV_W), lambda b, i: (b, i, 0)),
            pl.BlockSpec((1, ATT_KV_W, tm), lambda b, i: (b, 0, i)),
        ],
        out_shape=[
            jax.ShapeDtypeStruct((bm, ATT_Q_W, r), BF16),
            jax.ShapeDtypeStruct((bm, r, ATT_KV_W), BF16),
            jax.ShapeDtypeStruct((bm, ATT_KV_W, r), BF16),
        ],
        compiler_params=_params("parallel", "parallel"),
        name="proj_att",
    )(*args)


def _proj_plain_body(x_ref, m_ref, g_ref, w_ref, o_ref):
    m = m_ref[0]
    h = _norm_mod(x_ref[0], g_ref[...], m[0:1], m[1:2]).astype(BF16)
    o_ref[0] = jnp.dot(h, w_ref[...], preferred_element_type=F32)


def _proj_plain(x, mods, g, w, tm):
    bm, r, _ = x.shape
    n = w.shape[1]
    return pl.pallas_call(
        _proj_plain_body,
        grid=(bm, r // tm),
        in_specs=[
            pl.BlockSpec((1, tm, D_MODEL), lambda b, i: (b, i, 0)),
            pl.BlockSpec((1, 6, D_MODEL), lambda b, i: (b, 0, 0)),
            pl.BlockSpec((1, D_MODEL), lambda b, i: (0, 0)),
            pl.BlockSpec((D_MODEL, n), lambda b, i: (0, 0)),
        ],
        out_specs=pl.BlockSpec((1, tm, n), lambda b, i: (b, i, 0)),
        out_shape=jax.ShapeDtypeStruct((bm, r, n), F32),
        compiler_params=_params("parallel", "parallel"),
        name="proj_gla",
    )(x, mods, g, w)


def _proj_t_body(x_ref, m_ref, g_ref, wt_ref, o_ref):
    m = m_ref[0]
    h = _norm_mod(x_ref[0], g_ref[...], m[0:1], m[1:2]).astype(BF16)
    o_ref[0] = lax.dot_general(wt_ref[...], h, NT_DIMS, preferred_element_type=F32)


def _proj_t(x, mods, g, wt, tm):
    bm, r, _ = x.shape
    n = wt.shape[0]
    return pl.pallas_call(
        _proj_t_body,
        grid=(bm, r // tm),
        in_specs=[
            pl.BlockSpec((1, tm, D_MODEL), lambda b, i: (b, i, 0)),
            pl.BlockSpec((1, 6, D_MODEL), lambda b, i: (b, 0, 0)),
            pl.BlockSpec((1, D_MODEL), lambda b, i: (0, 0)),
            pl.BlockSpec((n, D_MODEL), lambda b, i: (0, 0)),
        ],
        out_specs=pl.BlockSpec((1, n, tm), lambda b, i: (b, 0, i)),
        out_shape=jax.ShapeDtypeStruct((bm, n, r), F32),
        compiler_params=_params("parallel", "parallel"),
        name="proj_hy",
    )(x, mods, g, wt)


def _attn_body(*refs, has_lat, tk):
    if has_lat:
        qt_ref, kl_ref, vlt_ref, kc_ref, vct_ref, o_ref = refs
    else:
        qt_ref, kc_ref, vct_ref, o_ref = refs
    qt = qt_ref[0]
    tq = qt.shape[1]
    q4t = jnp.concatenate([qt[g * HEAD_DIM:(g + 1) * HEAD_DIM, :] for g in range(ATT_GROUP)], axis=1)

    def update(k_c, vt_c, state):
        s = jnp.dot(k_c, q4t, preferred_element_type=F32)
        mc = jnp.max(s, axis=0, keepdims=True)
        if state is None:
            p = jnp.exp2(s - mc)
            return mc, jnp.sum(p, axis=0, keepdims=True), jnp.dot(vt_c, p.astype(BF16),
                                                                   preferred_element_type=F32)
        m, den, acc = state
        m_new = jnp.maximum(m, mc)
        alpha = jnp.exp2(m - m_new)
        p = jnp.exp2(s - m_new)
        den = alpha * den + jnp.sum(p, axis=0, keepdims=True)
        acc = alpha * acc + jnp.dot(vt_c, p.astype(BF16), preferred_element_type=F32)
        return m_new, den, acc

    state = update(kc_ref[0], vct_ref[0], None)
    if has_lat:
        for j in range(kl_ref.shape[1] // tk):
            state = update(kl_ref[0, j * tk:(j + 1) * tk, :], vlt_ref[0, :, j * tk:(j + 1) * tk], state)
    _, den, acc = state
    o = acc / den
    for g in range(ATT_GROUP):
        o_ref[0, g * HEAD_DIM:(g + 1) * HEAD_DIM, :] = o[:, g * tq:(g + 1) * tq]


def _attn(qt, k_lat, vt_lat, k_ctx, vt_ctx, nb, tq, tk=4096):
    has_lat = k_lat is not None
    gw = ATT_GROUP * HEAD_DIM
    sc = k_ctx.shape[1] // nb
    if has_lat:
        sq = qt.shape[2]
        q_map = lambda bb, h, i: (bb, h, i)
    else:
        sq = tq
        q_map = lambda bb, h, i: (0, h, bb)
    in_specs = [pl.BlockSpec((1, gw, tq), q_map)]
    args = [qt]
    if has_lat:
        sk = k_lat.shape[1]
        in_specs += [pl.BlockSpec((1, sk, HEAD_DIM), lambda bb, h, i: (bb, 0, h)),
                     pl.BlockSpec((1, HEAD_DIM, sk), lambda bb, h, i: (bb, h, 0))]
        args += [k_lat, vt_lat]
    in_specs += [pl.BlockSpec((1, sc, HEAD_DIM), lambda bb, h, i: (0, bb, h)),
                 pl.BlockSpec((1, HEAD_DIM, sc), lambda bb, h, i: (0, h, bb))]
    args += [k_ctx, vt_ctx]
    return pl.pallas_call(
        functools.partial(_attn_body, has_lat=has_lat, tk=tk),
        grid=(nb, ATT_KV_HEADS, sq // tq),
        in_specs=in_specs,
        out_specs=pl.BlockSpec((1, gw, tq), q_map),
        out_shape=jax.ShapeDtypeStruct(qt.shape, F32),
        compiler_params=_params("parallel", "parallel", "parallel"),
        name="attn",
    )(*args)


def _gla_streams(streams):
    tt = streams[0][0].shape[0]
    nc = tt // GLA_CHUNK
    ns = range(len(streams))
    heads = range(GLA_HEADS)
    qs, ks, vs, a_s, wa2, ba, tri, rev, get_st, set_st = zip(*streams)
    zs = [jnp.dot(a_s[i], wa2[i], precision=HIGHEST, preferred_element_type=F32) + ba[i] for i in ns]
    las = [(jnp.minimum(z, 0.0) - jnp.log(1.0 + jnp.exp(-jnp.abs(z)))) * (1.0 / GLA_TAU) for z in zs]
    cums = [jnp.dot(tri[i], las[i], precision=HIGHEST, preferred_element_type=F32) for i in ns]
    qd = [(qs[i] * (GLA_DK ** -0.5) * jnp.exp(cums[i])).astype(BF16) for i in ns]
    ki = [(ks[i] * jnp.exp(-cums[i])).astype(BF16) for i in ns]
    vb = [vs[i].astype(BF16) for i in ns]
    row = lax.broadcasted_iota(jnp.int32, (GLA_CHUNK, GLA_CHUNK), 0)
    col = lax.broadcasted_iota(jnp.int32, (GLA_CHUNK, GLA_CHUNK), 1)
    masks = [(col >= row) if rev[i] else (col <= row) for i in ns]
    hk = lambda h: slice(h * GLA_DK, (h + 1) * GLA_DK)
    hv = lambda h: slice(h * GLA_DV, (h + 1) * GLA_DV)
    outs = [[None] * nc for _ in ns]
    for step in range(nc):
        chunk = [nc - 1 - step if rev[i] else step for i in ns]
        rs = [slice(c * GLA_CHUNK, (c + 1) * GLA_CHUNK) for c in chunk]
        ends = [rs[i].start if rev[i] else rs[i].stop - 1 for i in ns]
        cl = [cums[i][ends[i]:ends[i] + 1, :] for i in ns]
        kte = [(ks[i][rs[i]] * jnp.exp(cl[i] - cums[i][rs[i]])).astype(BF16) for i in ns]
        st = [get_st[i]() for i in ns]
        stb = [s.astype(BF16) for s in st]
        att = [[lax.dot_general(qd[i][rs[i], hk(h)], ki[i][rs[i], hk(h)], NT_DIMS,
                                preferred_element_type=F32) for h in heads] for i in ns]
        upd = [[lax.dot_general(vb[i][rs[i], hv(h)], kte[i][:, hk(h)], TN_DIMS,
                                preferred_element_type=F32) for h in heads] for i in ns]
        inter = [[lax.dot_general(qd[i][rs[i], hk(h)], stb[i][:, hk(h)], NT_DIMS,
                                  preferred_element_type=F32) for h in heads] for i in ns]
        intra = [[jnp.dot(jnp.where(masks[i], att[i][h], 0.0).astype(BF16), vb[i][rs[i], hv(h)],
                          preferred_element_type=F32) for h in heads] for i in ns]
        for i in ns:
            set_st[i](st[i] * jnp.exp(cl[i]) + jnp.concatenate(upd[i], axis=1))
            outs[i][chunk[i]] = jnp.concatenate([intra[i][h] + inter[i][h] for h in heads], axis=1)
    return [jnp.concatenate(o, axis=0) for o in outs]


def _gla_body(qf, kf, vf, af, qb, kb, vb, ab, wa2, ba, tri, s0f, s0b,
              of, ob, sf_out, sb_out, sf_scr, sb_scr):
    @pl.when(pl.program_id(1) == 0)
    def _():
        sf_scr[...] = s0f[...]
        sb_scr[...] = s0b[...]

    def stream(i, q, k, v, a, d, scr):
        def set_state(val):
            scr[i] = val
        a_d = a[i][:, d * GLA_RANK:(d + 1) * GLA_RANK]
        return (q[i], k[i], v[i], a_d, wa2[d], ba[d], tri[d], d == 1, lambda: scr[i], set_state)

    rows = range(qf.shape[0])
    outs = _gla_streams([stream(i, qf, kf, vf, af, 0, sf_scr) for i in rows]
                        + [stream(i, qb, kb, vb, ab, 1, sb_scr) for i in rows])
    for i in rows:
        of[i] = outs[i]
        ob[i] = outs[len(rows) + i]
    sf_out[...] = sf_scr[...]
    sb_out[...] = sb_scr[...]


def _gla(p, wa2, ba, tri, s0f, s0b, tt):
    b, s, _ = p.shape
    n = s // tt
    a_blk = (2 * GLA_K_W + 2 * GLA_V_W) // GLA_A_PAD
    rb = GLA_BATCH_ROWS if b % GLA_BATCH_ROWS == 0 else 1

    def specs(rev):
        t = (lambda j: n - 1 - j) if rev else (lambda j: j)
        return [
            pl.BlockSpec((rb, tt, GLA_K_W), lambda bb, j: (bb, t(j), 0)),
            pl.BlockSpec((rb, tt, GLA_K_W), lambda bb, j: (bb, t(j), 1)),
            pl.BlockSpec((rb, tt, GLA_V_W), lambda bb, j: (bb, t(j), 1)),
            pl.BlockSpec((rb, tt, GLA_A_PAD), lambda bb, j: (bb, t(j), a_blk)),
        ]

    st_spec = pl.BlockSpec((rb, GLA_DV, GLA_K_W), lambda bb, j: (bb, 0, 0))
    st_shape = jax.ShapeDtypeStruct((b, GLA_DV, GLA_K_W), F32)
    return pl.pallas_call(
        _gla_body,
        grid=(b // rb, n),
        in_specs=specs(False) + specs(True) + [
            pl.BlockSpec((2, GLA_RANK, GLA_K_W), lambda bb, j: (0, 0, 0)),
            pl.BlockSpec((2, 1, GLA_K_W), lambda bb, j: (0, 0, 0)),
            pl.BlockSpec((2, tt, tt), lambda bb, j: (0, 0, 0)),
            st_spec, st_spec,
        ],
        out_specs=[
            pl.BlockSpec((rb, tt, GLA_V_W), lambda bb, j: (bb, j, 0)),
            pl.BlockSpec((rb, tt, GLA_V_W), lambda bb, j: (bb, n - 1 - j, 0)),
            st_spec, st_spec,
        ],
        out_shape=[
            jax.ShapeDtypeStruct((b, s, GLA_V_W), F32),
            jax.ShapeDtypeStruct((b, s, GLA_V_W), F32),
            st_shape, st_shape,
        ],
        scratch_shapes=[pltpu.VMEM((rb, GLA_DV, GLA_K_W), F32), pltpu.VMEM((rb, GLA_DV, GLA_K_W), F32)],
        compiler_params=_params("parallel", "arbitrary"),
        name="gla",
    )(p, p, p, p, p, p, p, p, wa2, ba, tri, s0f, s0b)


def _gla_tri(tt):
    t = np.arange(tt)
    same = (t[:, None] // GLA_CHUNK) == (t[None, :] // GLA_CHUNK)
    fwd = same & (t[None, :] <= t[:, None])
    bwd = same & (t[None, :] >= t[:, None])
    return jnp.asarray(np.stack([fwd, bwd]).astype(np.float32))


def _hy_filter_body(zt_ref, w1t_ref, b1_ref, fr_ref, w2t_ref, b2_ref, w3t_ref, dec_ref,
                    f_ref, b_ref, h2_scr):
    @pl.when((pl.program_id(0) == 0) & (pl.program_id(1) == 0))
    def _():
        fr = fr_ref[...]
        h1 = jnp.sin(fr * (jnp.dot(w1t_ref[...], zt_ref[...], precision=HIGHEST,
                                   preferred_element_type=F32) + b1_ref[...]))
        h2_scr[...] = jnp.sin(fr * (jnp.dot(w2t_ref[...], h1, precision=HIGHEST,
                                            preferred_element_type=F32) + b2_ref[...]))

    h2 = h2_scr[...]
    tn = zt_ref[0:1, :]
    f = jnp.dot(w3t_ref[0, 0], h2, precision=HIGHEST, preferred_element_type=F32)
    f = f * jnp.exp(-tn * jnp.abs(dec_ref[0, 0]))
    b = jnp.dot(w3t_ref[0, 1], h2, precision=HIGHEST, preferred_element_type=F32)
    b = b * jnp.exp(-tn * jnp.abs(dec_ref[0, 1]))
    lane = lax.broadcasted_iota(jnp.int32, b.shape, 1)
    b = jnp.where(lane == 0, 0.0, b)
    den = (jnp.sum(jnp.abs(f), axis=-1, keepdims=True)
           + jnp.sum(jnp.abs(b), axis=-1, keepdims=True) + EPS)
    f_ref[0] = f / den
    b_ref[0] = b / den


def _hy_filters(n, w1, b1, fr, w2, b2, w3, dec):
    t = jnp.arange(n, dtype=F32)
    t_norm = t / n
    bands = jnp.linspace(1e-4, HY_BANDS - 1, HY_BANDS, dtype=F32)
    phase = (2 * math.pi / n) * t[:, None] * bands[None, :]
    z = jnp.concatenate([t_norm[:, None], jnp.cos(phase), -jnp.sin(phase)], axis=-1)
    zt = jnp.pad(z.T, ((0, HY_EMB_PAD - HY_EMB), (0, 0)))
    w1t = jnp.pad(w1.T, ((0, 0), (0, HY_EMB_PAD - HY_EMB)))
    w3t = w3.T.reshape(HY_ORDER, 2, HY_WIDTH, HY_FFN)
    ct = 128
    col = lambda v: v.reshape(HY_FFN, 1)
    full = lambda shape: pl.BlockSpec(shape, lambda o, c: (0,) * len(shape))
    out_spec = pl.BlockSpec((1, ct, n), lambda o, c: (o, c, 0))
    out_shape = jax.ShapeDtypeStruct((HY_ORDER, HY_WIDTH, n), F32)
    return pl.pallas_call(
        _hy_filter_body,
        grid=(HY_ORDER, HY_WIDTH // ct),
        in_specs=[
            full((HY_EMB_PAD, n)), full((HY_FFN, HY_EMB_PAD)), full((HY_FFN, 1)), full((HY_FFN, 1)),
            full((HY_FFN, HY_FFN)), full((HY_FFN, 1)),
            pl.BlockSpec((1, 2, ct, HY_FFN), lambda o, c: (o, 0, c, 0)),
            pl.BlockSpec((1, 2, ct, 1), lambda o, c: (o, 0, c, 0)),
        ],
        out_specs=[out_spec, out_spec],
        out_shape=[out_shape, out_shape],
        scratch_shapes=[pltpu.VMEM((HY_FFN, n), F32)],
        compiler_params=_params("arbitrary", "arbitrary"),
        name="hy_filter",
    )(zt, w1t, col(b1), col(fr), w2.T, col(b2), w3t, dec.reshape(HY_ORDER, 2, HY_WIDTH, 1))


def _dft_consts(n1):
    n = n1 * LANES
    half = n1 // 2
    g = DFT_GROUP
    rows = g * n1
    i1 = np.arange(half)
    g1k = np.zeros((2 * rows, g * half))
    for r in range(rows):
        c, k1 = divmod(r, n1)
        tau, j = divmod(r, SUBLANES)
        ang = 2 * np.pi * k1 * i1 / n1
        g1k[2 * SUBLANES * tau + j, i1 * g + c] = np.cos(ang)
        g1k[2 * SUBLANES * tau + SUBLANES + j, i1 * g + c] = -np.sin(ang)
    gik = g1k.T
    k1 = np.arange(n1)[:, None]
    i2 = np.arange(LANES)[None, :]
    at = 2 * np.pi * k1 * i2 / n
    twr = np.tile(np.cos(at), (g, 1))
    twi = np.tile(-np.sin(at), (g, 1))
    a2 = 2 * np.pi * np.arange(LANES)[:, None] * np.arange(LANES)[None, :] / LANES
    cplx = lambda fr, fi: np.block([[fr, fi], [-fi, fr]])
    f2 = cplx(np.cos(a2), -np.sin(a2))
    f2i = cplx(np.cos(a2), np.sin(a2))
    bf = lambda m: jnp.asarray(m, dtype=BF16)
    return dict(g1k=bf(g1k), twr=jnp.asarray(twr, F32), twi=jnp.asarray(twi, F32),
                f2=bf(f2), f2i=bf(f2i), gik=bf(gik))


def _split_tiles(a):
    nt = a.shape[0] // (2 * SUBLANES)
    re = jnp.concatenate([a[2 * SUBLANES * t:2 * SUBLANES * t + SUBLANES] for t in range(nt)], axis=0)
    im = jnp.concatenate([a[2 * SUBLANES * t + SUBLANES:2 * SUBLANES * (t + 1)] for t in range(nt)],
                         axis=0)
    return re, im


def _merge_tiles(re, im):
    nt = re.shape[0] // SUBLANES
    parts = []
    for t in range(nt):
        parts += [re[SUBLANES * t:SUBLANES * (t + 1)], im[SUBLANES * t:SUBLANES * (t + 1)]]
    return jnp.concatenate(parts, axis=0)


def _dot_lane_pairs(m, xs):
    out = []
    for i in range(0, len(xs) - 1, 2):
        r = jnp.dot(m, jnp.concatenate([xs[i], xs[i + 1]], axis=1), preferred_element_type=F32)
        out += [r[:, :LANES], r[:, LANES:]]
    if len(xs) % 2:
        out.append(jnp.dot(m, xs[-1], preferred_element_type=F32))
    return out


def _fwd_dft(xs, g1k, twr, twi, f2):
    half = xs[0].shape[1] // LANES
    rows = [jnp.concatenate([x[:, i * LANES:(i + 1) * LANES] for i in range(half)], axis=0)
            for x in xs]
    a = _dot_lane_pairs(g1k, [r.astype(BF16) for r in rows])
    t = []
    for v in a:
        ar, ai = _split_tiles(v)
        t.append(jnp.concatenate([ar * twr - ai * twi, ar * twi + ai * twr], axis=1).astype(BF16))
    p = [jnp.dot(v, f2, preferred_element_type=F32) for v in t]
    return [(v[:, :LANES], v[:, LANES:]) for v in p]


def _inv_dft(ys, twr, twi, f2i, gik):
    p = [jnp.dot(jnp.concatenate([yr, yi], axis=1).astype(BF16), f2i, preferred_element_type=F32)
         for yr, yi in ys]
    t = []
    for v in p:
        br, bi = v[:, :LANES], v[:, LANES:]
        t.append(_merge_tiles(br * twr + bi * twi, bi * twr - br * twi).astype(BF16))
    y = _dot_lane_pairs(gik, t)
    half = y[0].shape[0] // DFT_GROUP
    return [jnp.concatenate([v[i * DFT_GROUP:(i + 1) * DFT_GROUP] for i in range(half)], axis=1)
            for v in y]


def _hy_spec_body(f_ref, b_ref, g1k, twr, twi, f2, hr_ref, hi_ref, *, groups, n1):
    scale = 1.0 / (n1 * LANES)

    def body(t, carry):
        gs = [2 * t, 2 * t + 1]
        rows = [pl.ds(pl.multiple_of(g * DFT_GROUP, DFT_GROUP), DFT_GROUP) for g in gs]
        slabs = []
        for r in rows:
            slabs += [f_ref[r, :], b_ref[r, :]]
        z = _fwd_dft(slabs, g1k[...], twr[...], twi[...], f2[...])
        for i, g in enumerate(gs):
            (zfr, zfi), (zbr, zbi) = z[2 * i], z[2 * i + 1]
            orow = pl.ds(pl.multiple_of(g * (DFT_GROUP * n1), DFT_GROUP * n1), DFT_GROUP * n1)
            hr_ref[orow, :] = (zfr + zbr) * scale
            hi_ref[orow, :] = (zfi - zbi) * scale
        return carry

    lax.fori_loop(0, groups // 2, body, 0)


def _const_spec(arr, ngrid):
    nd = arr.ndim
    return pl.BlockSpec(arr.shape, lambda *_: (0,) * nd)


def _hy_spec(f, b, consts, n1):
    chans, n_half = f.shape
    ct = 64
    groups = ct // DFT_GROUP
    cs = [consts[k] for k in ("g1k", "twr", "twi", "f2")]
    in_spec = pl.BlockSpec((ct, n_half), lambda c: (c, 0))
    out_spec = pl.BlockSpec((ct * n1, LANES), lambda c: (c, 0))
    out_shape = jax.ShapeDtypeStruct((chans * n1, LANES), F32)
    return pl.pallas_call(
        functools.partial(_hy_spec_body, groups=groups, n1=n1),
        grid=(chans // ct,),
        in_specs=[in_spec, in_spec] + [_const_spec(c, 1) for c in cs],
        out_specs=[out_spec, out_spec],
        out_shape=[out_shape, out_shape],
        compiler_params=_params("parallel"),
        name="hy_spec",
    )(f, b, *cs)


def _hy_main_body(u1_ref, u2_ref, uz_ref, p_ref, hr_ref, hi_ref, g1k, twr, twi, f2, f2i, gik,
                  o_ref, *, groups, n1):
    n_half = u1_ref.shape[2]
    shape = (DFT_GROUP, n_half)
    lane = lax.broadcasted_iota(jnp.int32, shape, 1)
    t_first = lane == 0
    t_last = lane == n_half - 1

    def prev(x):
        return jnp.where(t_first, 0.0, pltpu.roll(x, 1, 1))

    def nxt(x):
        return jnp.where(t_last, 0.0, pltpu.roll(x, n_half - 1, 1))

    def body(t, carry):
        gs = [t * HY_INTERLEAVE + i for i in range(HY_INTERLEAVE)]
        cs = [pl.ds(pl.multiple_of(g * DFT_GROUP, DFT_GROUP), DFT_GROUP) for g in gs]
        hrows = [pl.ds(pl.multiple_of(g * (DFT_GROUP * n1), DFT_GROUP * n1), DFT_GROUP * n1)
                 for g in gs]
        prs = [p_ref[c, :] for c in cs]
        col = lambda i, j: prs[i][:, j:j + 1]

        def sconv(ref, i, j):
            x = ref[0, cs[i], :]
            return prev(x) * col(i, j) + x * col(i, j + 1) + nxt(x) * col(i, j + 2) + col(i, j + 3)

        idx = range(HY_INTERLEAVE)

        def conv(vs, o):
            zs = _fwd_dft(vs, g1k[...], twr[...], twi[...], f2[...])
            ys = []
            for i, (zr, zi) in enumerate(zs):
                hr = hr_ref[o, hrows[i], :]
                hi = hi_ref[o, hrows[i], :]
                ys.append((zr * hr - zi * hi, zr * hi + zi * hr))
            return _inv_dft(ys, twr[...], twi[...], f2i[...], gik[...])

        zs = [sconv(uz_ref, i, 8) for i in idx]
        cv = conv(zs, 0)
        zs = [sconv(u1_ref, i, 0) * (cv[i] + col(i, 12) * zs[i]) for i in idx]
        cv = conv(zs, 1)
        for i in idx:
            o_ref[0, cs[i], :] = sconv(u2_ref, i, 4) * (cv[i] + col(i, 13) * zs[i])
        return carry

    lax.fori_loop(0, groups // HY_INTERLEAVE, body, 0)


def _hy_main(ut, prm, hr, hi, consts, n1):
    b, _, n_half = ut.shape
    ct = 64
    nct = HY_WIDTH // ct
    cs = [consts[k] for k in ("g1k", "twr", "twi", "f2", "f2i", "gik")]
    u_spec = lambda sec: pl.BlockSpec((1, ct, n_half), lambda c, bb: (bb, sec * nct + c, 0))
    h_spec = pl.BlockSpec((HY_ORDER, ct * n1, LANES), lambda c, bb: (0, c, 0))
    return pl.pallas_call(
        functools.partial(_hy_main_body, groups=ct // DFT_GROUP, n1=n1),
        grid=(nct, b),
        in_specs=[u_spec(0), u_spec(1), u_spec(2),
                  pl.BlockSpec((ct, 16), lambda c, bb: (c, 0)),
                  h_spec, h_spec] + [_const_spec(c, 2) for c in cs],
        out_specs=pl.BlockSpec((1, ct, n_half), lambda c, bb: (bb, c, 0)),
        out_shape=jax.ShapeDtypeStruct((b, HY_WIDTH, n_half), F32),
        compiler_params=_params("parallel", "parallel"),
        name="hy_main",
    )(ut, ut, ut, prm, hr, hi, *cs)


def _hyc_consts(n):
    nn = 2 * n
    a = 2 * np.pi * np.arange(n)[:, None] * np.arange(nn)[None, :] / nn
    fc = np.concatenate([np.cos(a), -np.sin(a)], axis=1)
    fi = np.concatenate([np.cos(a.T), -np.sin(a.T)], axis=0)
    return jnp.asarray(fc, BF16), jnp.asarray(fi, BF16)


def _hyc_spec_body(f_ref, b_ref, fc_ref, hr_ref, hi_ref):
    nn = hr_ref.shape[-1]
    zf = jnp.dot(f_ref[...].astype(BF16), fc_ref[...], preferred_element_type=F32)
    zb = jnp.dot(b_ref[...].astype(BF16), fc_ref[...], preferred_element_type=F32)
    hr_ref[...] = (zf[:, :nn] + zb[:, :nn]) * (1.0 / nn)
    hi_ref[...] = (zf[:, nn:] - zb[:, nn:]) * (1.0 / nn)


def _hyc_spec(f, b, fc):
    rows, n = f.shape
    out_shape = jax.ShapeDtypeStruct((rows, 2 * n), F32)
    return pl.pallas_call(
        _hyc_spec_body,
        out_shape=[out_shape, out_shape],
        compiler_params=pltpu.CompilerParams(vmem_limit_bytes=VMEM_LIMIT_BYTES),
        name="hyc_spec",
    )(f, b, fc)


def _hyc_main_body(u1_ref, u2_ref, uz_ref, p_ref, hr_ref, hi_ref, fc_ref, fi_ref, o_ref):
    shape = u1_ref.shape[1:]
    n = shape[1]
    nn = 2 * n
    lane = lax.broadcasted_iota(jnp.int32, shape, 1)
    pr = p_ref[...]
    col = lambda j: pr[:, j:j + 1]

    def sconv(ref, j):
        x = ref[0]
        p = jnp.where(lane == 0, 0.0, pltpu.roll(x, 1, 1))
        q = jnp.where(lane == n - 1, 0.0, pltpu.roll(x, n - 1, 1))
        return p * col(j) + x * col(j + 1) + q * col(j + 2) + col(j + 3)

    def conv(v, o):
        s = jnp.dot(v.astype(BF16), fc_ref[...], preferred_element_type=F32)
        zr, zi = s[:, :nn], s[:, nn:]
        hr, hi = hr_ref[o], hi_ref[o]
        y = jnp.concatenate([zr * hr - zi * hi, zr * hi + zi * hr], axis=1).astype(BF16)
        return jnp.dot(y, fi_ref[...], preferred_element_type=F32)

    x1 = sconv(u1_ref, 0)
    x2 = sconv(u2_ref, 4)
    z = sconv(uz_ref, 8)
    z = x1 * (conv(z, 0) + col(12) * z)
    o_ref[0] = x2 * (conv(z, 1) + col(13) * z)


def _hyc_main(ut, prm, hr, hi, fc, fi, nb, n):
    ct = 128
    nct = HY_WIDTH // ct
    u_spec = lambda sec: pl.BlockSpec((1, ct, n), lambda bb, c: (0, sec * nct + c, bb))
    h_spec = pl.BlockSpec((HY_ORDER, ct, 2 * n), lambda bb, c: (0, c, 0))
    return pl.pallas_call(
        _hyc_main_body,
        grid=(nb, nct),
        in_specs=[u_spec(0), u_spec(1), u_spec(2),
                  pl.BlockSpec((ct, 16), lambda bb, c: (c, 0)),
                  h_spec, h_spec, _const_spec(fc, 2), _const_spec(fi, 2)],
        out_specs=pl.BlockSpec((1, ct, n), lambda bb, c: (0, c, bb)),
        out_shape=jax.ShapeDtypeStruct((1, HY_WIDTH, nb * n), F32),
        compiler_params=_params("parallel", "parallel"),
        name="hyc_main",
    )(ut, ut, ut, prm, hr, hi, fc, fi)


def _merge_body(x_ref, m_ref, g_ref, wbrg, yh_ref, of_ref, ob_ref, og_ref, gn_ref, ya_ref,
                wh, wg, wa, wo, o_ref):
    x = x_ref[0]
    m = m_ref[0]
    h = _norm_mod(x, g_ref[...], m[0:1], m[1:2]).astype(BF16)
    gates = jnp.dot(h, wbrg[...], preferred_element_type=F32)
    yh = yh_ref[0].T
    o = of_ref[0] + ob_ref[0]
    parts = []
    for hh in range(GLA_HEADS):
        t = o[:, hh * GLA_DV:(hh + 1) * GLA_DV]
        parts.append(t * lax.rsqrt(jnp.mean(t * t, axis=-1, keepdims=True) + EPS) * gn_ref[...])
    og = og_ref[0]
    yg = jnp.concatenate(parts, axis=1) * (og * _sigmoid(og))
    d = D_MODEL
    mm = (_sigmoid(gates[:, :d]) * jnp.dot(yh.astype(BF16), wh[...], preferred_element_type=F32)
          + _sigmoid(gates[:, d:2 * d]) * jnp.dot(yg.astype(BF16), wg[...], preferred_element_type=F32)
          + _sigmoid(gates[:, 2 * d:]) * jnp.dot(ya_ref[0].T.astype(BF16), wa[...],
                                                  preferred_element_type=F32))
    out = jnp.dot(mm.astype(BF16), wo[...], preferred_element_type=F32)
    o_ref[0] = x + m[2:3] * out


def _merge(x, mods, g, wbrg, yht, of, ob, pg, gn, yat, wh, wg, wa, wo, tm):
    bm, r, _ = x.shape
    row = lambda w: pl.BlockSpec((1, tm, w), lambda b, i: (b, i, 0))
    full = lambda a: pl.BlockSpec(a.shape, lambda b, i: (0,) * a.ndim)
    og_blk = (2 * GLA_K_W + GLA_V_W) // GLA_V_W
    return pl.pallas_call(
        _merge_body,
        grid=(bm, r // tm),
        in_specs=[
            row(D_MODEL),
            pl.BlockSpec((1, 6, D_MODEL), lambda b, i: (b, 0, 0)),
            full(g), full(wbrg),
            pl.BlockSpec((1, HY_WIDTH, tm), lambda b, i: (b, 0, i)),
            row(GLA_V_W), row(GLA_V_W),
            pl.BlockSpec((1, tm, GLA_V_W), lambda b, i: (b, i, og_blk)),
            full(gn),
            pl.BlockSpec((1, ATT_Q_W, tm), lambda b, i: (b, 0, i)),
            full(wh), full(wg), full(wa), full(wo),
        ],
        out_specs=row(D_MODEL),
        out_shape=jax.ShapeDtypeStruct(x.shape, F32),
        compiler_params=_params("parallel", "parallel"),
        name="merge",
    )(x, mods, g, wbrg, yht, of, ob, pg, gn, yat, wh, wg, wa, wo)


def _route(logits, rb):
    scores = _sigmoid(logits)
    sel = scores + rb
    lane = lax.broadcasted_iota(jnp.int32, sel.shape, 1).astype(F32)
    neg = -jnp.inf

    def top2(v):
        m1 = jnp.max(v, axis=-1, keepdims=True)
        i1 = jnp.min(jnp.where(v == m1, lane, float(N_EXPERTS)), axis=-1, keepdims=True)
        v2 = jnp.where(lane == i1, neg, v)
        m2 = jnp.max(v2, axis=-1, keepdims=True)
        i2 = jnp.min(jnp.where(v2 == m2, lane, float(N_EXPERTS)), axis=-1, keepdims=True)
        return m1, m2, i1, i2

    group_of = jnp.floor(lane * (1.0 / EXPERTS_PER_GROUP))
    best = None
    best_g = None
    for g in range(N_GROUPS):
        m1, m2, _, _ = top2(jnp.where(group_of == float(g), sel, neg))
        gs = m1 + m2
        if best is None:
            best, best_g = gs, jnp.zeros_like(gs)
        else:
            better = gs > best
            best_g = jnp.where(better, float(g), best_g)
            best = jnp.where(better, gs, best)
    _, _, i1, i2 = top2(jnp.where(group_of == best_g, sel, neg))
    w = jnp.where((lane == i1) | (lane == i2), scores, 0.0)
    return w / jnp.sum(w, axis=-1, keepdims=True), best_g


def _moe_body(x_ref, m_ref, g_ref, rw_ref, rb_ref, wg_ref, wu_ref, wd_ref, o_ref,
              h_scr, ghi_scr, glo_scr, grp_scr, pos_scr, grpt_scr, post_scr, acc_scr):
    g = pl.program_id(2)
    gf = g.astype(F32)
    tm = x_ref.shape[1]
    sub = MOE_SUB
    lane = lax.broadcasted_iota(jnp.int32, (tm, LANES), 1).astype(F32)

    @pl.when(g == 0)
    def _():
        m = m_ref[0]
        h = _norm_mod(x_ref[0], g_ref[...], m[3:4], m[4:5])
        h_hi = h.astype(BF16)
        h_scr[...] = h_hi
        h_lo = (h - h_hi.astype(F32)).astype(BF16)
        p2 = jnp.dot(h_hi, rw_ref[...], preferred_element_type=F32)
        p1 = jnp.dot(h_lo, rw_ref[:, :LANES], preferred_element_type=F32)
        ne = N_EXPERTS
        logits = p2[:, :ne] + p2[:, ne:2 * ne] + p1[:, :ne]
        gates, grp = _route(logits, rb_ref[...])
        ghi = gates.astype(BF16)
        ghi_scr[...] = ghi
        glo_scr[...] = (gates - ghi.astype(F32)).astype(BF16)
        grp_b = jnp.broadcast_to(grp, (tm, LANES))
        onehot = (lane == grp_b).astype(BF16)
        ranks = []
        for rb_ in range(tm // sub):
            r = lax.broadcasted_iota(jnp.int32, (sub, tm), 0) + rb_ * sub
            c = lax.broadcasted_iota(jnp.int32, (sub, tm), 1)
            ranks.append(jnp.dot((c < r).astype(BF16), onehot, preferred_element_type=F32))
        rank = jnp.concatenate(ranks, axis=0)
        pos = jnp.sum(jnp.where(lane == grp_b, rank, 0.0), axis=-1, keepdims=True)
        pos_b = jnp.broadcast_to(pos, (tm, LANES))
        grp_scr[...] = grp_b
        pos_scr[...] = pos_b
        grpt_scr[...] = grp_b.T
        post_scr[...] = pos_b.T
        acc_scr[...] = jnp.zeros_like(acc_scr)

    in_group = grp_scr[...] == gf
    n_g = jnp.sum(jnp.where(in_group[:, 0:1], 1.0, 0.0)).astype(jnp.int32)
    n_sub = (n_g + (sub - 1)) // sub
    d_iota = lax.broadcasted_iota(jnp.int32, (sub, tm), 0).astype(F32)
    lane16 = lax.broadcasted_iota(jnp.int32, (sub, N_EXPERTS), 1)

    def body(k, carry):
        kf = (k * sub).astype(F32)
        p = ((grpt_scr[...] == gf) & (post_scr[...] - kf == d_iota)).astype(BF16)
        hd = jnp.dot(p, h_scr[...], preferred_element_type=F32).astype(BF16)
        gd = (jnp.dot(p, ghi_scr[...], preferred_element_type=F32)
              + jnp.dot(p, glo_scr[...], preferred_element_type=F32))
        experts = range(EXPERTS_PER_GROUP)
        a = [jnp.dot(hd, wg_ref[j], preferred_element_type=F32) for j in experts]
        b = [jnp.dot(hd, wu_ref[j], preferred_element_type=F32) for j in experts]
        hid = [(a[j] * _sigmoid(a[j]) * b[j]).astype(BF16) for j in experts]
        dn = [jnp.dot(hid[j], wd_ref[j], preferred_element_type=F32) for j in experts]
        y = jnp.zeros((sub, D_MODEL), F32)
        for j in experts:
            ge = jnp.sum(jnp.where(lane16 == g * EXPERTS_PER_GROUP + j, gd, 0.0),
                         axis=-1, keepdims=True)
            y = y + ge * dn[j]
        pt = (in_group & (pos_scr[...] - kf == lane)).astype(BF16)
        acc_scr[...] += jnp.dot(pt, y.astype(BF16), preferred_element_type=F32)
        return carry

    lax.fori_loop(0, n_sub, body, 0)

    @pl.when(g == N_GROUPS - 1)
    def _():
        o_ref[0] = x_ref[0] + m_ref[0][5:6] * acc_scr[...]


def _moe(x, mods, g, rw, rb, wg, wu, wd, tm):
    bm, r, _ = x.shape
    row = pl.BlockSpec((1, tm, D_MODEL), lambda b, i, e: (b, i, 0))
    return pl.pallas_call(
        _moe_body,
        grid=(bm, r // tm, N_GROUPS),
        in_specs=[
            row,
            pl.BlockSpec((1, 6, D_MODEL), lambda b, i, e: (b, 0, 0)),
            pl.BlockSpec((1, D_MODEL), lambda b, i, e: (0, 0)),
            pl.BlockSpec((D_MODEL, 2 * LANES), lambda b, i, e: (0, 0)),
            pl.BlockSpec((1, N_EXPERTS), lambda b, i, e: (0, 0)),
            pl.BlockSpec((EXPERTS_PER_GROUP, D_MODEL, D_EXPERT), lambda b, i, e: (e, 0, 0)),
            pl.BlockSpec((EXPERTS_PER_GROUP, D_MODEL, D_EXPERT), lambda b, i, e: (e, 0, 0)),
            pl.BlockSpec((EXPERTS_PER_GROUP, D_EXPERT, D_MODEL), lambda b, i, e: (e, 0, 0)),
        ],
        out_specs=row,
        out_shape=jax.ShapeDtypeStruct(x.shape, F32),
        scratch_shapes=[
            pltpu.VMEM((tm, D_MODEL), BF16),
            pltpu.VMEM((tm, N_EXPERTS), BF16),
            pltpu.VMEM((tm, N_EXPERTS), BF16),
            pltpu.VMEM((tm, LANES), F32),
            pltpu.VMEM((tm, LANES), F32),
            pltpu.VMEM((LANES, tm), F32),
            pltpu.VMEM((LANES, tm), F32),
            pltpu.VMEM((tm, D_MODEL), F32),
        ],
        compiler_params=_params("parallel", "parallel", "arbitrary"),
        name="moe",
    )(x, mods, g, rw, rb, wg, wu, wd)


def _rope_tables(n_tokens):
    rows = n_tokens // GRID_W
    row = jnp.broadcast_to(jnp.arange(rows)[:, None], (rows, GRID_W)).reshape(-1).astype(F32)
    col = jnp.broadcast_to(jnp.arange(GRID_W)[None, :], (rows, GRID_W)).reshape(-1).astype(F32)
    inv_freq = ROPE_THETA ** (-jnp.arange(ROPE_PAIRS_PER_AXIS, dtype=F32) / ROPE_PAIRS_PER_AXIS)
    ang = jnp.concatenate([row[:, None] * inv_freq, col[:, None] * inv_freq], axis=-1)
    cos, sin = jnp.cos(ang), jnp.sin(ang)
    cos2, sin2 = jnp.concatenate([cos, cos], axis=-1), jnp.concatenate([-sin, sin], axis=-1)
    return cos2, sin2, cos2.T, sin2.T


def _split_w_in(w):
    sizes = (ATT_KV_W, ATT_KV_W, GLA_K_W, GLA_V_W, 2 * GLA_RANK,
             ATT_Q_W, GLA_K_W, GLA_V_W, 3 * HY_WIDTH, 3 * D_MODEL)
    cuts = [int(v) for v in np.cumsum(sizes)[:-1]]
    a_k, a_v, g_k, g_v, g_a, a_q, g_q, g_og, hy_u, br_g = jnp.split(w, cuts, axis=1)
    perm = np.concatenate([np.arange(0, HEAD_DIM, 2), np.arange(1, HEAD_DIM, 2)])
    perm_q = np.concatenate([h * HEAD_DIM + perm for h in range(ATT_HEADS)])
    perm_k = np.concatenate([h * HEAD_DIM + perm for h in range(ATT_KV_HEADS)])
    w_qv_t = jnp.concatenate([a_q[:, perm_q], a_v], axis=1).T.astype(BF16)
    w_k = a_k[:, perm_k].astype(BF16)
    pad = jnp.zeros((D_MODEL, GLA_A_PAD - 2 * GLA_RANK), w.dtype)
    w_gla = jnp.concatenate([g_q, g_k, g_v, g_og, g_a, pad], axis=1).astype(BF16)
    return w_qv_t, w_k, w_gla, hy_u.T.astype(BF16), br_g.astype(BF16), perm


def kernel(x, c, ctx, c_ctx, w_mod, b_mod, norm1_g, norm2_g, w_in, q_norm_g, k_norm_g, gla_wa2, gla_ba, gla_norm_g, hy_conv_w, hy_conv_b, hy_pos_w1, hy_pos_b1, hy_sin_freq, hy_pos_w2, hy_pos_b2, hy_pos_w3, hy_decay, hy_skip, w_br_hy, w_br_gla, w_br_att, w_out, router_w, router_b, moe_w_gate, moe_w_up, moe_w_down):
    nb, seq, d = x.shape
    nctx = ctx.shape[1]
    n1_lat = 2 * seq // LANES

    c_all = jnp.concatenate([c, c_ctx[None, :], jnp.zeros((16 - nb - 1, d), F32)], axis=0)
    mods_all = _mods(c_all, w_mod, b_mod)

    rope_tabs = _rope_tables(seq)
    consts = _dft_consts(n1_lat)
    fc, fi = _hyc_consts(nctx)
    tri_lat = _gla_tri(256)
    zero_state = jnp.zeros((nb, GLA_DV, GLA_K_W), F32)
    rb = router_b.reshape(1, N_EXPERTS)
    rw_hi = router_w.astype(BF16)
    rw_lo = (router_w - rw_hi.astype(F32)).astype(BF16)
    rw2 = jnp.concatenate([rw_hi, rw_lo, jnp.zeros((d, 2 * LANES - 2 * N_EXPERTS), BF16)], axis=1)

    xc = ctx.reshape(1, nb * nctx, d)
    for l in range(DEPTH):
        last = l == DEPTH - 1
        mods_lat = mods_all[l, :nb].reshape(nb, 6, d)
        mods_ctx = mods_all[l, nb:nb + 1].reshape(1, 6, d)
        g1 = norm1_g[l].reshape(1, d)
        g2 = norm2_g[l].reshape(1, d)
        w_qv_t, w_k, w_gla, w_hyt, w_brg, perm = _split_w_in(w_in[l])
        qg = (q_norm_g[l][perm] * Q_SCALE).reshape(HEAD_DIM, 1)
        kg = k_norm_g[l][perm].reshape(1, HEAD_DIM)
        wa2 = gla_wa2[l]
        ba = gla_ba[l].reshape(2, 1, GLA_K_W)
        gn = gla_norm_g[l].reshape(1, GLA_DV)
        wh, wg, wa, wo = (w_br_hy[l].astype(BF16), w_br_gla[l].astype(BF16),
                          w_br_att[l].astype(BF16), w_out[l].astype(BF16))
        cw, cb = hy_conv_w[l], hy_conv_b[l]
        sec = lambda s: [cw[0, s], cw[1, s], cw[2, s], cb[s]]
        w_ = HY_WIDTH
        prm = jnp.stack(sec(slice(0, w_)) + sec(slice(w_, 2 * w_)) + sec(slice(2 * w_, 3 * w_))
                        + [hy_skip[l, 0], hy_skip[l, 1], jnp.zeros((w_,), F32), jnp.zeros((w_,), F32)],
                        axis=1)
        filt_args = (hy_pos_w1[l], hy_pos_b1[l], hy_sin_freq[l], hy_pos_w2[l], hy_pos_b2[l],
                     hy_pos_w3[l], hy_decay[l])

        qt, k, vt = _proj_att(x, mods_lat, g1, w_qv_t, w_k, qg, kg, rope_tabs, 512)
        qct, kc, vct = _proj_att(xc, mods_ctx, g1, w_qv_t, w_k, qg, kg, None, 256)
        pg = _proj_plain(x, mods_lat, g1, w_gla, 512)
        pgc = _proj_plain(xc, mods_ctx, g1, w_gla, 256)
        ut = _proj_t(x, mods_lat, g1, w_hyt, 512)

        y_att = _attn(qt, k, vt, kc, vct, nb, 256)

        cof, cob, s_f, s_b = _gla(pgc.reshape(nb, nctx, GLA_COLS), wa2, ba, _gla_tri(nctx),
                                  zero_state, zero_state, nctx)
        of, ob, _, _ = _gla(pg, wa2, ba, tri_lat, s_f, s_b, 256)

        ff, fb = _hy_filters(seq, *filt_args)
        hr, hi = _hy_spec(ff.reshape(-1, seq), fb.reshape(-1, seq), consts, n1_lat)
        hr = hr.reshape(HY_ORDER, HY_WIDTH * n1_lat, LANES)
        hi = hi.reshape(HY_ORDER, HY_WIDTH * n1_lat, LANES)
        y_hyt = _hy_main(ut, prm, hr, hi, consts, n1_lat)

        x = _merge(x, mods_lat, g1, w_brg, y_hyt, of, ob, pg, gn, y_att, wh, wg, wa, wo, 256)
        moe_w = (moe_w_gate[l].astype(BF16), moe_w_up[l].astype(BF16), moe_w_down[l].astype(BF16))
        x = _moe(x, mods_lat, g2, rw2, rb, *moe_w, 1024)

        if not last:
            yc_att = _attn(qct, None, None, kc, vct, nb, nctx)
            uct = _proj_t(xc, mods_ctx, g1, w_hyt, 256)
            cff, cfb = _hy_filters(nctx, *filt_args)
            chr_, chi = _hyc_spec(cff.reshape(-1, nctx), cfb.reshape(-1, nctx), fc)
            chr_ = chr_.reshape(HY_ORDER, HY_WIDTH, 2 * nctx)
            chi = chi.reshape(HY_ORDER, HY_WIDTH, 2 * nctx)
            yc_hyt = _hyc_main(uct, prm, chr_, chi, fc, fi, nb, nctx)
            xc = _merge(xc, mods_ctx, g1, w_brg, yc_hyt,
                        cof.reshape(1, nb * nctx, GLA_V_W), cob.reshape(1, nb * nctx, GLA_V_W),
                        pgc, gn, yc_att, wh, wg, wa, wo, 256)
            xc = _moe(xc, mods_ctx, g2, rw2, rb, *moe_w, min(1024, nb * nctx))
    return x
```

```python
import functools
import math

import numpy as np
import jax
import jax.numpy as jnp
from jax import lax
from jax.experimental import pallas as pl
from jax.experimental.pallas import tpu as pltpu

F32 = jnp.float32
BF16 = jnp.bfloat16
HIGHEST = lax.Precision.HIGHEST

D_MODEL = 1024
DEPTH = 2
GRID_W = 64
EPS = 1e-6

ATT_HEADS = 8
ATT_KV_HEADS = 2
ATT_GROUP = ATT_HEADS // ATT_KV_HEADS
HEAD_DIM = 128
ROPE_PAIRS_PER_AXIS = HEAD_DIM // 4
ROPE_THETA = 10000.0
Q_SCALE = HEAD_DIM ** -0.5 * math.log2(math.e)

GLA_HEADS = 4
GLA_DK = 64
GLA_DV = 128
GLA_RANK = 16
GLA_TAU = 16.0
GLA_CHUNK = 64
GLA_BATCH_ROWS = 4

HY_WIDTH = 512
HY_ORDER = 2
HY_BANDS = 16
HY_EMB = 1 + 2 * HY_BANDS
HY_EMB_PAD = 40
HY_FFN = 64

N_EXPERTS = 16
N_GROUPS = 4
EXPERTS_PER_GROUP = N_EXPERTS // N_GROUPS
D_EXPERT = 512

ATT_Q_W = ATT_HEADS * HEAD_DIM
ATT_KV_W = ATT_KV_HEADS * HEAD_DIM
GLA_K_W = GLA_HEADS * GLA_DK
GLA_V_W = GLA_HEADS * GLA_DV
GLA_A_PAD = 128
GLA_COLS = 2 * GLA_K_W + 2 * GLA_V_W + GLA_A_PAD
ATT_COLS = ATT_Q_W + 2 * ATT_KV_W

LANES = 128
SUBLANES = 8
MOE_SUB = LANES
VMEM_LIMIT_BYTES = 60 * 1024 * 1024

DFT_GROUP = SUBLANES
HY_INTERLEAVE = 4

NT_DIMS = (((1,), (1,)), ((), ()))
TN_DIMS = (((0,), (0,)), ((), ()))


def _params(*sem):
    return pltpu.CompilerParams(dimension_semantics=sem, vmem_limit_bytes=VMEM_LIMIT_BYTES)


def _sigmoid(x):
    return 1.0 / (1.0 + jnp.exp(-x))


def _norm_mod(x, g, shift, scale):
    ms = jnp.mean(x * x, axis=-1, keepdims=True)
    return (x * lax.rsqrt(ms + EPS) * g) * (1.0 + scale) + shift


def _mod_body(c_ref, w_ref, b_ref, o_ref):
    c = c_ref[...]
    s = c * _sigmoid(c)
    o_ref[0] = jnp.dot(s, w_ref[0], precision=HIGHEST, preferred_element_type=F32) + b_ref[0]


def _mods(c_all, w_mod, b_mod):
    tn = 512
    rows = c_all.shape[0]
    return pl.pallas_call(
        _mod_body,
        grid=(DEPTH, 6 * D_MODEL // tn),
        in_specs=[
            pl.BlockSpec((rows, D_MODEL), lambda l, j: (0, 0)),
            pl.BlockSpec((1, D_MODEL, tn), lambda l, j: (l, 0, j)),
            pl.BlockSpec((1, 1, tn), lambda l, j: (l, 0, j)),
        ],
        out_specs=pl.BlockSpec((1, rows, tn), lambda l, j: (l, 0, j)),
        out_shape=jax.ShapeDtypeStruct((DEPTH, rows, 6 * D_MODEL), F32),
        compiler_params=_params("parallel", "parallel"),
        name="mods",
    )(c_all, w_mod, b_mod.reshape(DEPTH, 1, 6 * D_MODEL))


def _proj_att_body(*refs, rope):
    if rope:
        (x_ref, m_ref, g_ref, wqv_ref, wk_ref, qg_ref, kg_ref, cos_ref, sin_ref, cost_ref, sint_ref,
         qt_ref, k_ref, vt_ref) = refs
    else:
        x_ref, m_ref, g_ref, wqv_ref, wk_ref, qg_ref, kg_ref, qt_ref, k_ref, vt_ref = refs
    m = m_ref[0]
    h = _norm_mod(x_ref[0], g_ref[...], m[0:1], m[1:2]).astype(BF16)
    pt = lax.dot_general(wqv_ref[...], h, NT_DIMS, preferred_element_type=F32)
    pk = jnp.dot(h, wk_ref[...], preferred_element_type=F32)
    half = HEAD_DIM // 2

    for i in range(ATT_HEADS):
        sl = slice(i * HEAD_DIM, (i + 1) * HEAD_DIM)
        t = pt[sl]
        t = t * lax.rsqrt(jnp.mean(t * t, axis=0, keepdims=True) + EPS) * qg_ref[...]
        if rope:
            t = t * cost_ref[...] + pltpu.roll(t, half, 0) * sint_ref[...]
        qt_ref[0, sl, :] = t.astype(BF16)
    for i in range(ATT_KV_HEADS):
        sl = slice(i * HEAD_DIM, (i + 1) * HEAD_DIM)
        t = pk[:, sl]
        t = t * lax.rsqrt(jnp.mean(t * t, axis=-1, keepdims=True) + EPS) * kg_ref[...]
        if rope:
            t = t * cos_ref[...] + pltpu.roll(t, half, 1) * sin_ref[...]
        k_ref[0, :, sl] = t.astype(BF16)
    vt_ref[0] = pt[ATT_Q_W:].astype(BF16)


def _proj_att(x, mods, g, wqv_t, wk, qg_col, kg, rope_tabs, tm):
    bm, r, _ = x.shape
    rope = rope_tabs is not None
    in_specs = [
        pl.BlockSpec((1, tm, D_MODEL), lambda b, i: (b, i, 0)),
        pl.BlockSpec((1, 6, D_MODEL), lambda b, i: (b, 0, 0)),
        pl.BlockSpec((1, D_MODEL), lambda b, i: (0, 0)),
        pl.BlockSpec((ATT_Q_W + ATT_KV_W, D_MODEL), lambda b, i: (0, 0)),
        pl.BlockSpec((D_MODEL, ATT_KV_W), lambda b, i: (0, 0)),
        pl.BlockSpec((HEAD_DIM, 1), lambda b, i: (0, 0)),
        pl.BlockSpec((1, HEAD_DIM), lambda b, i: (0, 0)),
    ]
    args = [x, mods, g, wqv_t, wk, qg_col, kg]
    if rope:
        in_specs += [pl.BlockSpec((tm, HEAD_DIM), lambda b, i: (i, 0))] * 2
        in_specs += [pl.BlockSpec((HEAD_DIM, tm), lambda b, i: (0, i))] * 2
        args += list(rope_tabs)
    return pl.pallas_call(
        functools.partial(_proj_att_body, rope=rope),
        grid=(bm, r // tm),
        in_specs=in_specs,
        out_specs=[
            pl.BlockSpec((1, ATT_Q_W, tm), lambda b, i: (b, 0, i)),
            pl.BlockSpec((1, tm, ATT_KV_W), lambda b, i: (b, i, 0)),
            pl.BlockSpec((1, ATT_KV_W, tm), lambda b, i: (b, 0, i)),
        ],
        out_shape=[
            jax.ShapeDtypeStruct((bm, ATT_Q_W, r), BF16),
            jax.ShapeDtypeStruct((bm, r, ATT_KV_W), BF16),
            jax.ShapeDtypeStruct((bm, ATT_KV_W, r), BF16),
        ],
        compiler_params=_params("parallel", "parallel"),
        name="proj_att",
    )(*args)


def _proj_plain_body(x_ref, m_ref, g_ref, w_ref, o_ref):
    m = m_ref[0]
    h = _norm_mod(x_ref[0], g_ref[...], m[0:1], m[1:2]).astype(BF16)
    o_ref[0] = jnp.dot(h, w_ref[...], preferred_element_type=F32)


def _proj_plain(x, mods, g, w, tm):
    bm, r, _ = x.shape
    n = w.shape[1]
    return pl.pallas_call(
        _proj_plain_body,
        grid=(bm, r // tm),
        in_specs=[
            pl.BlockSpec((1, tm, D_MODEL), lambda b, i: (b, i, 0)),
            pl.BlockSpec((1, 6, D_MODEL), lambda b, i: (b, 0, 0)),
            pl.BlockSpec((1, D_MODEL), lambda b, i: (0, 0)),
            pl.BlockSpec((D_MODEL, n), lambda b, i: (0, 0)),
        ],
        out_specs=pl.BlockSpec((1, tm, n), lambda b, i: (b, i, 0)),
        out_shape=jax.ShapeDtypeStruct((bm, r, n), F32),
        compiler_params=_params("parallel", "parallel"),
        name="proj_gla",
    )(x, mods, g, w)


def _proj_t_body(x_ref, m_ref, g_ref, wt_ref, o_ref):
    m = m_ref[0]
    h = _norm_mod(x_ref[0], g_ref[...], m[0:1], m[1:2]).astype(BF16)
    o_ref[0] = lax.dot_general(wt_ref[...], h, NT_DIMS, preferred_element_type=F32)


def _proj_t(x, mods, g, wt, tm):
    bm, r, _ = x.shape
    n = wt.shape[0]
    return pl.pallas_call(
        _proj_t_body,
        grid=(bm, r // tm),
        in_specs=[
            pl.BlockSpec((1, tm, D_MODEL), lambda b, i: (b, i, 0)),
            pl.BlockSpec((1, 6, D_MODEL), lambda b, i: (b, 0, 0)),
            pl.BlockSpec((1, D_MODEL), lambda b, i: (0, 0)),
            pl.BlockSpec((n, D_MODEL), lambda b, i: (0, 0)),
        ],
        out_specs=pl.BlockSpec((1, n, tm), lambda b, i: (b, 0, i)),
        out_shape=jax.ShapeDtypeStruct((bm, n, r), F32),
        compiler_params=_params("parallel", "parallel"),
        name="proj_hy",
    )(x, mods, g, wt)


def _attn_body(*refs, has_lat, tk):
    if has_lat:
        qt_ref, kl_ref, vlt_ref, kc_ref, vct_ref, o_ref = refs
    else:
        qt_ref, kc_ref, vct_ref, o_ref = refs
    qt = qt_ref[0]
    tq = qt.shape[1]
    q4t = jnp.concatenate([qt[g * HEAD_DIM:(g + 1) * HEAD_DIM, :] for g in range(ATT_GROUP)], axis=1)

    def update(k_c, vt_c, state):
        s = jnp.dot(k_c, q4t, preferred_element_type=F32)
        mc = jnp.max(s, axis=0, keepdims=True)
        if state is None:
            p = jnp.exp2(s - mc)
            return mc, jnp.sum(p, axis=0, keepdims=True), jnp.dot(vt_c, p.astype(BF16),
                                                                   preferred_element_type=F32)
        m, den, acc = state
        m_new = jnp.maximum(m, mc)
        alpha = jnp.exp2(m - m_new)
        p = jnp.exp2(s - m_new)
        den = alpha * den + jnp.sum(p, axis=0, keepdims=True)
        acc = alpha * acc + jnp.dot(vt_c, p.astype(BF16), preferred_element_type=F32)
        return m_new, den, acc

    state = update(kc_ref[0], vct_ref[0], None)
    if has_lat:
        for j in range(kl_ref.shape[1] // tk):
            state = update(kl_ref[0, j * tk:(j + 1) * tk, :], vlt_ref[0, :, j * tk:(j + 1) * tk], state)
    _, den, acc = state
    o = acc / den
    for g in range(ATT_GROUP):
        o_ref[0, g * HEAD_DIM:(g + 1) * HEAD_DIM, :] = o[:, g * tq:(g + 1) * tq]


def _attn(qt, k_lat, vt_lat, k_ctx, vt_ctx, nb, tq, tk=4096):
    has_lat = k_lat is not None
    gw = ATT_GROUP * HEAD_DIM
    sc = k_ctx.shape[1] // nb
    if has_lat:
        sq = qt.shape[2]
        q_map = lambda bb, h, i: (bb, h, i)
    else:
        sq = tq
        q_map = lambda bb, h, i: (0, h, bb)
    in_specs = [pl.BlockSpec((1, gw, tq), q_map)]
    args = [qt]
    if has_lat:
        sk = k_lat.shape[1]
        in_specs += [pl.BlockSpec((1, sk, HEAD_DIM), lambda bb, h, i: (bb, 0, h)),
                     pl.BlockSpec((1, HEAD_DIM, sk), lambda bb, h, i: (bb, h, 0))]
        args += [k_lat, vt_lat]
    in_specs += [pl.BlockSpec((1, sc, HEAD_DIM), lambda bb, h, i: (0, bb, h)),
                 pl.BlockSpec((1, HEAD_DIM, sc), lambda bb, h, i: (0, h, bb))]
    args += [k_ctx, vt_ctx]
    return pl.pallas_call(
        functools.partial(_attn_body, has_lat=has_lat, tk=tk),
        grid=(nb, ATT_KV_HEADS, sq // tq),
        in_specs=in_specs,
        out_specs=pl.BlockSpec((1, gw, tq), q_map),
        out_shape=jax.ShapeDtypeStruct(qt.shape, F32),
        compiler_params=_params("parallel", "parallel", "parallel"),
        name="attn",
    )(*args)


def _gla_streams(streams):
    tt = streams[0][0].shape[0]
    nc = tt // GLA_CHUNK
    ns = range(len(streams))
    heads = range(GLA_HEADS)
    qs, ks, vs, a_s, wa2, ba, tri, rev, get_st, set_st = zip(*streams)
    zs = [jnp.dot(a_s[i], wa2[i], precision=HIGHEST, preferred_element_type=F32) + ba[i] for i in ns]
    las = [(jnp.minimum(z, 0.0) - jnp.log(1.0 + jnp.exp(-jnp.abs(z)))) * (1.0 / GLA_TAU) for z in zs]
    cums = [jnp.dot(tri[i], las[i], precision=HIGHEST, preferred_element_type=F32) for i in ns]
    qd = [(qs[i] * (GLA_DK ** -0.5) * jnp.exp(cums[i])).astype(BF16) for i in ns]
    ki = [(ks[i] * jnp.exp(-cums[i])).astype(BF16) for i in ns]
    vb = [vs[i].astype(BF16) for i in ns]
    row = lax.broadcasted_iota(jnp.int32, (GLA_CHUNK, GLA_CHUNK), 0)
    col = lax.broadcasted_iota(jnp.int32, (GLA_CHUNK, GLA_CHUNK), 1)
    masks = [(col >= row) if rev[i] else (col <= row) for i in ns]
    hk = lambda h: slice(h * GLA_DK, (h + 1) * GLA_DK)
    hv = lambda h: slice(h * GLA_DV, (h + 1) * GLA_DV)
    outs = [[None] * nc for _ in ns]
    for step in range(nc):
        chunk = [nc - 1 - step if rev[i] else step for i in ns]
        rs = [slice(c * GLA_CHUNK, (c + 1) * GLA_CHUNK) for c in chunk]
        ends = [rs[i].start if rev[i] else rs[i].stop - 1 for i in ns]
        cl = [cums[i][ends[i]:ends[i] + 1, :] for i in ns]
        kte = [(ks[i][rs[i]] * jnp.exp(cl[i] - cums[i][rs[i]])).astype(BF16) for i in ns]
        st = [get_st[i]() for i in ns]
        stb = [s.astype(BF16) for s in st]
        att = [[lax.dot_general(qd[i][rs[i], hk(h)], ki[i][rs[i], hk(h)], NT_DIMS,
                                preferred_element_type=F32) for h in heads] for i in ns]
        upd = [[lax.dot_general(vb[i][rs[i], hv(h)], kte[i][:, hk(h)], TN_DIMS,
                                preferred_element_type=F32) for h in heads] for i in ns]
        inter = [[lax.dot_general(qd[i][rs[i], hk(h)], stb[i][:, hk(h)], NT_DIMS,
                                  preferred_element_type=F32) for h in heads] for i in ns]
        intra = [[jnp.dot(jnp.where(masks[i], att[i][h], 0.0).astype(BF16), vb[i][rs[i], hv(h)],
                          preferred_element_type=F32) for h in heads] for i in ns]
        for i in ns:
            set_st[i](st[i] * jnp.exp(cl[i]) + jnp.concatenate(upd[i], axis=1))
            outs[i][chunk[i]] = jnp.concatenate([intra[i][h] + inter[i][h] for h in heads], axis=1)
    return [jnp.concatenate(o, axis=0) for o in outs]


def _gla_body(qf, kf, vf, af, qb, kb, vb, ab, wa2, ba, tri, s0f, s0b,
              of, ob, sf_out, sb_out, sf_scr, sb_scr):
    @pl.when(pl.program_id(1) == 0)
    def _():
        sf_scr[...] = s0f[...]
        sb_scr[...] = s0b[...]

    def stream(i, q, k, v, a, d, scr):
        def set_state(val):
            scr[i] = val
        a_d = a[i][:, d * GLA_RANK:(d + 1) * GLA_RANK]
        return (q[i], k[i], v[i], a_d, wa2[d], ba[d], tri[d], d == 1, lambda: scr[i], set_state)

    rows = range(qf.shape[0])
    outs = _gla_streams([stream(i, qf, kf, vf, af, 0, sf_scr) for i in rows]
                        + [stream(i, qb, kb, vb, ab, 1, sb_scr) for i in rows])
    for i in rows:
        of[i] = outs[i]
        ob[i] = outs[len(rows) + i]
    sf_out[...] = sf_scr[...]
    sb_out[...] = sb_scr[...]


def _gla(p, wa2, ba, tri, s0f, s0b, tt):
    b, s, _ = p.shape
    n = s // tt
    a_blk = (2 * GLA_K_W + 2 * GLA_V_W) // GLA_A_PAD
    rb = GLA_BATCH_ROWS if b % GLA_BATCH_ROWS == 0 else 1

    def specs(rev):
        t = (lambda j: n - 1 - j) if rev else (lambda j: j)
        return [
            pl.BlockSpec((rb, tt, GLA_K_W), lambda bb, j: (bb, t(j), 0)),
            pl.BlockSpec((rb, tt, GLA_K_W), lambda bb, j: (bb, t(j), 1)),
            pl.BlockSpec((rb, tt, GLA_V_W), lambda bb, j: (bb, t(j), 1)),
            pl.BlockSpec((rb, tt, GLA_A_PAD), lambda bb, j: (bb, t(j), a_blk)),
        ]

    st_spec = pl.BlockSpec((rb, GLA_DV, GLA_K_W), lambda bb, j: (bb, 0, 0))
    st_shape = jax.ShapeDtypeStruct((b, GLA_DV, GLA_K_W), F32)
    return pl.pallas_call(
        _gla_body,
        grid=(b // rb, n),
        in_specs=specs(False) + specs(True) + [
            pl.BlockSpec((2, GLA_RANK, GLA_K_W), lambda bb, j: (0, 0, 0)),
            pl.BlockSpec((2, 1, GLA_K_W), lambda bb, j: (0, 0, 0)),
            pl.BlockSpec((2, tt, tt), lambda bb, j: (0, 0, 0)),
            st_spec, st_spec,
        ],
        out_specs=[
            pl.BlockSpec((rb, tt, GLA_V_W), lambda bb, j: (bb, j, 0)),
            pl.BlockSpec((rb, tt, GLA_V_W), lambda bb, j: (bb, n - 1 - j, 0)),
            st_spec, st_spec,
        ],
        out_shape=[
            jax.ShapeDtypeStruct((b, s, GLA_V_W), F32),
            jax.ShapeDtypeStruct((b, s, GLA_V_W), F32),
            st_shape, st_shape,
        ],
        scratch_shapes=[pltpu.VMEM((rb, GLA_DV, GLA_K_W), F32), pltpu.VMEM((rb, GLA_DV, GLA_K_W), F32)],
        compiler_params=_params("parallel", "arbitrary"),
        name="gla",
    )(p, p, p, p, p, p, p, p, wa2, ba, tri, s0f, s0b)


def _gla_tri(tt):
    t = np.arange(tt)
    same = (t[:, None] // GLA_CHUNK) == (t[None, :] // GLA_CHUNK)
    fwd = same & (t[None, :] <= t[:, None])
    bwd = same & (t[None, :] >= t[:, None])
    return jnp.asarray(np.stack([fwd, bwd]).astype(np.float32))


def _hy_filter_body(zt_ref, w1t_ref, b1_ref, fr_ref, w2t_ref, b2_ref, w3t_ref, dec_ref,
                    f_ref, b_ref, h2_scr):
    @pl.when((pl.program_id(0) == 0) & (pl.program_id(1) == 0))
    def _():
        fr = fr_ref[...]
        h1 = jnp.sin(fr * (jnp.dot(w1t_ref[...], zt_ref[...], precision=HIGHEST,
                                   preferred_element_type=F32) + b1_ref[...]))
        h2_scr[...] = jnp.sin(fr * (jnp.dot(w2t_ref[...], h1, precision=HIGHEST,
                                            preferred_element_type=F32) + b2_ref[...]))

    h2 = h2_scr[...]
    tn = zt_ref[0:1, :]
    f = jnp.dot(w3t_ref[0, 0], h2, precision=HIGHEST, preferred_element_type=F32)
    f = f * jnp.exp(-tn * jnp.abs(dec_ref[0, 0]))
    b = jnp.dot(w3t_ref[0, 1], h2, precision=HIGHEST, preferred_element_type=F32)
    b = b * jnp.exp(-tn * jnp.abs(dec_ref[0, 1]))
    lane = lax.broadcasted_iota(jnp.int32, b.shape, 1)
    b = jnp.where(lane == 0, 0.0, b)
    den = (jnp.sum(jnp.abs(f), axis=-1, keepdims=True)
           + jnp.sum(jnp.abs(b), axis=-1, keepdims=True) + EPS)
    f_ref[0] = f / den
    b_ref[0] = b / den


def _hy_filters(n, w1, b1, fr, w2, b2, w3, dec):
    t = jnp.arange(n, dtype=F32)
    t_norm = t / n
    bands = jnp.linspace(1e-4, HY_BANDS - 1, HY_BANDS, dtype=F32)
    phase = (2 * math.pi / n) * t[:, None] * bands[None, :]
    z = jnp.concatenate([t_norm[:, None], jnp.cos(phase), -jnp.sin(phase)], axis=-1)
    zt = jnp.pad(z.T, ((0, HY_EMB_PAD - HY_EMB), (0, 0)))
    w1t = jnp.pad(w1.T, ((0, 0), (0, HY_EMB_PAD - HY_EMB)))
    w3t = w3.T.reshape(HY_ORDER, 2, HY_WIDTH, HY_FFN)
    ct = 128
    col = lambda v: v.reshape(HY_FFN, 1)
    full = lambda shape: pl.BlockSpec(shape, lambda o, c: (0,) * len(shape))
    out_spec = pl.BlockSpec((1, ct, n), lambda o, c: (o, c, 0))
    out_shape = jax.ShapeDtypeStruct((HY_ORDER, HY_WIDTH, n), F32)
    return pl.pallas_call(
        _hy_filter_body,
        grid=(HY_ORDER, HY_WIDTH // ct),
        in_specs=[
            full((HY_EMB_PAD, n)), full((HY_FFN, HY_EMB_PAD)), full((HY_FFN, 1)), full((HY_FFN, 1)),
            full((HY_FFN, HY_FFN)), full((HY_FFN, 1)),
            pl.BlockSpec((1, 2, ct, HY_FFN), lambda o, c: (o, 0, c, 0)),
            pl.BlockSpec((1, 2, ct, 1), lambda o, c: (o, 0, c, 0)),
        ],
        out_specs=[out_spec, out_spec],
        out_shape=[out_shape, out_shape],
        scratch_shapes=[pltpu.VMEM((HY_FFN, n), F32)],
        compiler_params=_params("arbitrary", "arbitrary"),
        name="hy_filter",
    )(zt, w1t, col(b1), col(fr), w2.T, col(b2), w3t, dec.reshape(HY_ORDER, 2, HY_WIDTH, 1))


def _dft_consts(n1):
    n = n1 * LANES
    half = n1 // 2
    g = DFT_GROUP
    rows = g * n1
    i1 = np.arange(half)
    g1k = np.zeros((2 * rows, g * half))
    for r in range(rows):
        c, k1 = divmod(r, n1)
        tau, j = divmod(r, SUBLANES)
        ang = 2 * np.pi * k1 * i1 / n1
        g1k[2 * SUBLANES * tau + j, i1 * g + c] = np.cos(ang)
        g1k[2 * SUBLANES * tau + SUBLANES + j, i1 * g + c] = -np.sin(ang)
    gik = g1k.T
    k1 = np.arange(n1)[:, None]
    i2 = np.arange(LANES)[None, :]
    at = 2 * np.pi * k1 * i2 / n
    twr = np.tile(np.cos(at), (g, 1))
    twi = np.tile(-np.sin(at), (g, 1))
    a2 = 2 * np.pi * np.arange(LANES)[:, None] * np.arange(LANES)[None, :] / LANES
    cplx = lambda fr, fi: np.block([[fr, fi], [-fi, fr]])
    f2 = cplx(np.cos(a2), -np.sin(a2))
    f2i = cplx(np.cos(a2), np.sin(a2))
    bf = lambda m: jnp.asarray(m, dtype=BF16)
    return dict(g1k=bf(g1k), twr=jnp.asarray(twr, F32), twi=jnp.asarray(twi, F32),
                f2=bf(f2), f2i=bf(f2i), gik=bf(gik))


def _split_tiles(a):
    nt = a.shape[0] // (2 * SUBLANES)
    re = jnp.concatenate([a[2 * SUBLANES * t:2 * SUBLANES * t + SUBLANES] for t in range(nt)], axis=0)
    im = jnp.concatenate([a[2 * SUBLANES * t + SUBLANES:2 * SUBLANES * (t + 1)] for t in range(nt)],
                         axis=0)
    return re, im


def _merge_tiles(re, im):
    nt = re.shape[0] // SUBLANES
    parts = []
    for t in range(nt):
        parts += [re[SUBLANES * t:SUBLANES * (t + 1)], im[SUBLANES * t:SUBLANES * (t + 1)]]
    return jnp.concatenate(parts, axis=0)


def _dot_lane_pairs(m, xs):
    out = []
    for i in range(0, len(xs) - 1, 2):
        r = jnp.dot(m, jnp.concatenate([xs[i], xs[i + 1]], axis=1), preferred_element_type=F32)
        out += [r[:, :LANES], r[:, LANES:]]
    if len(xs) % 2:
        out.append(jnp.dot(m, xs[-1], preferred_element_type=F32))
    return out


def _fwd_dft(xs, g1k, twr, twi, f2):
    half = xs[0].shape[1] // LANES
    rows = [jnp.concatenate([x[:, i * LANES:(i + 1) * LANES] for i in range(half)], axis=0)
            for x in xs]
    a = _dot_lane_pairs(g1k, [r.astype(BF16) for r in rows])
    t = []
    for v in a:
        ar, ai = _split_tiles(v)
        t.append(jnp.concatenate([ar * twr - ai * twi, ar * twi + ai * twr], axis=1).astype(BF16))
    p = [jnp.dot(v, f2, preferred_element_type=F32) for v in t]
    return [(v[:, :LANES], v[:, LANES:]) for v in p]


def _inv_dft(ys, twr, twi, f2i, gik):
    p = [jnp.dot(jnp.concatenate([yr, yi], axis=1).astype(BF16), f2i, preferred_element_type=F32)
         for yr, yi in ys]
    t = []
    for v in p:
        br, bi = v[:, :LANES], v[:, LANES:]
        t.append(_merge_tiles(br * twr + bi * twi, bi * twr - br * twi).astype(BF16))
    y = _dot_lane_pairs(gik, t)
    half = y[0].shape[0] // DFT_GROUP
    return [jnp.concatenate([v[i * DFT_GROUP:(i + 1) * DFT_GROUP] for i in range(half)], axis=1)
            for v in y]


def _hy_spec_body(f_ref, b_ref, g1k, twr, twi, f2, hr_ref, hi_ref, *, groups, n1):
    scale = 1.0 / (n1 * LANES)

    def body(t, carry):
        gs = [2 * t, 2 * t + 1]
        rows = [pl.ds(pl.multiple_of(g * DFT_GROUP, DFT_GROUP), DFT_GROUP) for g in gs]
        slabs = []
        for r in rows:
            slabs += [f_ref[r, :], b_ref[r, :]]
        z = _fwd_dft(slabs, g1k[...], twr[...], twi[...], f2[...])
        for i, g in enumerate(gs):
            (zfr, zfi), (zbr, zbi) = z[2 * i], z[2 * i + 1]
            orow = pl.ds(pl.multiple_of(g * (DFT_GROUP * n1), DFT_GROUP * n1), DFT_GROUP * n1)
            hr_ref[orow, :] = (zfr + zbr) * scale
            hi_ref[orow, :] = (zfi - zbi) * scale
        return carry

    lax.fori_loop(0, groups // 2, body, 0)


def _const_spec(arr, ngrid):
    nd = arr.ndim
    return pl.BlockSpec(arr.shape, lambda *_: (0,) * nd)


def _hy_spec(f, b, consts, n1):
    chans, n_half = f.shape
    ct = 64
    groups = ct // DFT_GROUP
    cs = [consts[k] for k in ("g1k", "twr", "twi", "f2")]
    in_spec = pl.BlockSpec((ct, n_half), lambda c: (c, 0))
    out_spec = pl.BlockSpec((ct * n1, LANES), lambda c: (c, 0))
    out_shape = jax.ShapeDtypeStruct((chans * n1, LANES), F32)
    return pl.pallas_call(
        functools.partial(_hy_spec_body, groups=groups, n1=n1),
        grid=(chans // ct,),
        in_specs=[in_spec, in_spec] + [_const_spec(c, 1) for c in cs],
        out_specs=[out_spec, out_spec],
        out_shape=[out_shape, out_shape],
        compiler_params=_params("parallel"),
        name="hy_spec",
    )(f, b, *cs)


def _hy_main_body(u1_ref, u2_ref, uz_ref, p_ref, hr_ref, hi_ref, g1k, twr, twi, f2, f2i, gik,
                  o_ref, *, groups, n1):
    n_half = u1_ref.shape[2]
    shape = (DFT_GROUP, n_half)
    lane = lax.broadcasted_iota(jnp.int32, shape, 1)
    t_first = lane == 0
    t_last = lane == n_half - 1

    def prev(x):
        return jnp.where(t_first, 0.0, pltpu.roll(x, 1, 1))

    def nxt(x):
        return jnp.where(t_last, 0.0, pltpu.roll(x, n_half - 1, 1))

    def body(t, carry):
        gs = [t * HY_INTERLEAVE + i for i in range(HY_INTERLEAVE)]
        cs = [pl.ds(pl.multiple_of(g * DFT_GROUP, DFT_GROUP), DFT_GROUP) for g in gs]
        hrows = [pl.ds(pl.multiple_of(g * (DFT_GROUP * n1), DFT_GROUP * n1), DFT_GROUP * n1)
                 for g in gs]
        prs = [p_ref[c, :] for c in cs]
        col = lambda i, j: prs[i][:, j:j + 1]

        def sconv(ref, i, j):
            x = ref[0, cs[i], :]
            return prev(x) * col(i, j) + x * col(i, j + 1) + nxt(x) * col(i, j + 2) + col(i, j + 3)

        idx = range(HY_INTERLEAVE)

        def conv(vs, o):
            zs = _fwd_dft(vs, g1k[...], twr[...], twi[...], f2[...])
            ys = []
            for i, (zr, zi) in enumerate(zs):
                hr = hr_ref[o, hrows[i], :]
                hi = hi_ref[o, hrows[i], :]
                ys.append((zr * hr - zi * hi, zr * hi + zi * hr))
            return _inv_dft(ys, twr[...], twi[...], f2i[...], gik[...])

        zs = [sconv(uz_ref, i, 8) for i in idx]
        cv = conv(zs, 0)
        zs = [sconv(u1_ref, i, 0) * (cv[i] + col(i, 12) * zs[i]) for i in idx]
        cv = conv(zs, 1)
        for i in idx:
            o_ref[0, cs[i], :] = sconv(u2_ref, i, 4) * (cv[i] + col(i, 13) * zs[i])
        return carry

    lax.fori_loop(0, groups // HY_INTERLEAVE, body, 0)


def _hy_main(ut, prm, hr, hi, consts, n1):
    b, _, n_half = ut.shape
    ct = 64
    nct = HY_WIDTH // ct
    cs = [consts[k] for k in ("g1k", "twr", "twi", "f2", "f2i", "gik")]
    u_spec = lambda sec: pl.BlockSpec((1, ct, n_half), lambda c, bb: (bb, sec * nct + c, 0))
    h_spec = pl.BlockSpec((HY_ORDER, ct * n1, LANES), lambda c, bb: (0, c, 0))
    return pl.pallas_call(
        functools.partial(_hy_main_body, groups=ct // DFT_GROUP, n1=n1),
        grid=(nct, b),
        in_specs=[u_spec(0), u_spec(1), u_spec(2),
                  pl.BlockSpec((ct, 16), lambda c, bb: (c, 0)),
                  h_spec, h_spec] + [_const_spec(c, 2) for c in cs],
        out_specs=pl.BlockSpec((1, ct, n_half), lambda c, bb: (bb, c, 0)),
        out_shape=jax.ShapeDtypeStruct((b, HY_WIDTH, n_half), F32),
        compiler_params=_params("parallel", "parallel"),
        name="hy_main",
    )(ut, ut, ut, prm, hr, hi, *cs)


def _hyc_consts(n):
    nn = 2 * n
    a = 2 * np.pi * np.arange(n)[:, None] * np.arange(nn)[None, :] / nn
    fc = np.concatenate([np.cos(a), -np.sin(a)], axis=1)
    fi = np.concatenate([np.cos(a.T), -np.sin(a.T)], axis=0)
    return jnp.asarray(fc, BF16), jnp.asarray(fi, BF16)


def _hyc_spec_body(f_ref, b_ref, fc_ref, hr_ref, hi_ref):
    nn = hr_ref.shape[-1]
    zf = jnp.dot(f_ref[...].astype(BF16), fc_ref[...], preferred_element_type=F32)
    zb = jnp.dot(b_ref[...].astype(BF16), fc_ref[...], preferred_element_type=F32)
    hr_ref[...] = (zf[:, :nn] + zb[:, :nn]) * (1.0 / nn)
    hi_ref[...] = (zf[:, nn:] - zb[:, nn:]) * (1.0 / nn)


def _hyc_spec(f, b, fc):
    rows, n = f.shape
    out_shape = jax.ShapeDtypeStruct((rows, 2 * n), F32)
    return pl.pallas_call(
        _hyc_spec_body,
        out_shape=[out_shape, out_shape],
        compiler_params=pltpu.CompilerParams(vmem_limit_bytes=VMEM_LIMIT_BYTES),
        name="hyc_spec",
    )(f, b, fc)


def _hyc_main_body(u1_ref, u2_ref, uz_ref, p_ref, hr_ref, hi_ref, fc_ref, fi_ref, o_ref):
    shape = u1_ref.shape[1:]
    n = shape[1]
    nn = 2 * n
    lane = lax.broadcasted_iota(jnp.int32, shape, 1)
    pr = p_ref[...]
    col = lambda j: pr[:, j:j + 1]

    def sconv(ref, j):
        x = ref[0]
        p = jnp.where(lane == 0, 0.0, pltpu.roll(x, 1, 1))
        q = jnp.where(lane == n - 1, 0.0, pltpu.roll(x, n - 1, 1))
        return p * col(j) + x * col(j + 1) + q * col(j + 2) + col(j + 3)

    def conv(v, o):
        s = jnp.dot(v.astype(BF16), fc_ref[...], preferred_element_type=F32)
        zr, zi = s[:, :nn], s[:, nn:]
        hr, hi = hr_ref[o], hi_ref[o]
        y = jnp.concatenate([zr * hr - zi * hi, zr * hi + zi * hr], axis=1).astype(BF16)
        return jnp.dot(y, fi_ref[...], preferred_element_type=F32)

    x1 = sconv(u1_ref, 0)
    x2 = sconv(u2_ref, 4)
    z = sconv(uz_ref, 8)
    z = x1 * (conv(z, 0) + col(12) * z)
    o_ref[0] = x2 * (conv(z, 1) + col(13) * z)


def _hyc_main(ut, prm, hr, hi, fc, fi, nb, n):
    ct = 128
    nct = HY_WIDTH // ct
    u_spec = lambda sec: pl.BlockSpec((1, ct, n), lambda bb, c: (0, sec * nct + c, bb))
    h_spec = pl.BlockSpec((HY_ORDER, ct, 2 * n), lambda bb, c: (0, c, 0))
    return pl.pallas_call(
        _hyc_main_body,
        grid=(nb, nct),
        in_specs=[u_spec(0), u_spec(1), u_spec(2),
                  pl.BlockSpec((ct, 16), lambda bb, c: (c, 0)),
                  h_spec, h_spec, _const_spec(fc, 2), _const_spec(fi, 2)],
        out_specs=pl.BlockSpec((1, ct, n), lambda bb, c: (0, c, bb)),
        out_shape=jax.ShapeDtypeStruct((1, HY_WIDTH, nb * n), F32),
        compiler_params=_params("parallel", "parallel"),
        name="hyc_main",
    )(ut, ut, ut, prm, hr, hi, fc, fi)


def _merge_body(x_ref, m_ref, g_ref, wbrg, yh_ref, of_ref, ob_ref, og_ref, gn_ref, ya_ref,
                wh, wg, wa, wo, o_ref):
    x = x_ref[0]
    m = m_ref[0]
    h = _norm_mod(x, g_ref[...], m[0:1], m[1:2]).astype(BF16)
    gates = jnp.dot(h, wbrg[...], preferred_element_type=F32)
    yh = yh_ref[0].T
    o = of_ref[0] + ob_ref[0]
    parts = []
    for hh in range(GLA_HEADS):
        t = o[:, hh * GLA_DV:(hh + 1) * GLA_DV]
        parts.append(t * lax.rsqrt(jnp.mean(t * t, axis=-1, keepdims=True) + EPS) * gn_ref[...])
    og = og_ref[0]
    yg = jnp.concatenate(parts, axis=1) * (og * _sigmoid(og))
    d = D_MODEL
    mm = (_sigmoid(gates[:, :d]) * jnp.dot(yh.astype(BF16), wh[...], preferred_element_type=F32)
          + _sigmoid(gates[:, d:2 * d]) * jnp.dot(yg.astype(BF16), wg[...], preferred_element_type=F32)
          + _sigmoid(gates[:, 2 * d:]) * jnp.dot(ya_ref[0].T.astype(BF16), wa[...],
                                                  preferred_element_type=F32))
    out = jnp.dot(mm.astype(BF16), wo[...], preferred_element_type=F32)
    o_ref[0] = x + m[2:3] * out


def _merge(x, mods, g, wbrg, yht, of, ob, pg, gn, yat, wh, wg, wa, wo, tm):
    bm, r, _ = x.shape
    row = lambda w: pl.BlockSpec((1, tm, w), lambda b, i: (b, i, 0))
    full = lambda a: pl.BlockSpec(a.shape, lambda b, i: (0,) * a.ndim, pipeline_mode=pl.Buffered(1))
    og_blk = (2 * GLA_K_W + GLA_V_W) // GLA_V_W
    return pl.pallas_call(
        _merge_body,
        grid=(bm, r // tm),
        in_specs=[
            row(D_MODEL),
            pl.BlockSpec((1, 6, D_MODEL), lambda b, i: (b, 0, 0)),
            full(g), full(wbrg),
            pl.BlockSpec((1, HY_WIDTH, tm), lambda b, i: (b, 0, i)),
            row(GLA_V_W), row(GLA_V_W),
            pl.BlockSpec((1, tm, GLA_V_W), lambda b, i: (b, i, og_blk)),
            full(gn),
            pl.BlockSpec((1, ATT_Q_W, tm), lambda b, i: (b, 0, i)),
            full(wh), full(wg), full(wa), full(wo),
        ],
        out_specs=row(D_MODEL),
        out_shape=jax.ShapeDtypeStruct(x.shape, F32),
        compiler_params=_params("parallel", "parallel"),
        name="merge",
    )(x, mods, g, wbrg, yht, of, ob, pg, gn, yat, wh, wg, wa, wo)


def _route(logits, rb):
    scores = _sigmoid(logits)
    sel = scores + rb
    lane = lax.broadcasted_iota(jnp.int32, sel.shape, 1).astype(F32)
    neg = -jnp.inf

    def top2(v):
        m1 = jnp.max(v, axis=-1, keepdims=True)
        i1 = jnp.min(jnp.where(v == m1, lane, float(N_EXPERTS)), axis=-1, keepdims=True)
        v2 = jnp.where(lane == i1, neg, v)
        m2 = jnp.max(v2, axis=-1, keepdims=True)
        i2 = jnp.min(jnp.where(v2 == m2, lane, float(N_EXPERTS)), axis=-1, keepdims=True)
        return m1, m2, i1, i2

    group_of = jnp.floor(lane * (1.0 / EXPERTS_PER_GROUP))
    best = None
    best_g = None
    for g in range(N_GROUPS):
        m1, m2, _, _ = top2(jnp.where(group_of == float(g), sel, neg))
        gs = m1 + m2
        if best is None:
            best, best_g = gs, jnp.zeros_like(gs)
        else:
            better = gs > best
            best_g = jnp.where(better, float(g), best_g)
            best = jnp.where(better, gs, best)
    _, _, i1, i2 = top2(jnp.where(group_of == best_g, sel, neg))
    w = jnp.where((lane == i1) | (lane == i2), scores, 0.0)
    return w / jnp.sum(w, axis=-1, keepdims=True), best_g


def _moe_body(x_ref, m_ref, g_ref, rw_ref, rb_ref, wg_ref, wu_ref, wd_ref, o_ref,
              h_scr, ghi_scr, glo_scr, grp_scr, pos_scr, grpt_scr, post_scr, acc_scr):
    g = pl.program_id(2)
    gf = g.astype(F32)
    tm = x_ref.shape[1]
    sub = MOE_SUB
    lane = lax.broadcasted_iota(jnp.int32, (tm, LANES), 1).astype(F32)

    @pl.when(g == 0)
    def _():
        m = m_ref[0]
        h = _norm_mod(x_ref[0], g_ref[...], m[3:4], m[4:5])
        h_hi = h.astype(BF16)
        h_scr[...] = h_hi
        h_lo = (h - h_hi.astype(F32)).astype(BF16)
        p2 = jnp.dot(h_hi, rw_ref[...], preferred_element_type=F32)
        p1 = jnp.dot(h_lo, rw_ref[:, :LANES], preferred_element_type=F32)
        ne = N_EXPERTS
        logits = p2[:, :ne] + p2[:, ne:2 * ne] + p1[:, :ne]
        gates, grp = _route(logits, rb_ref[...])
        ghi = gates.astype(BF16)
        ghi_scr[...] = ghi
        glo_scr[...] = (gates - ghi.astype(F32)).astype(BF16)
        grp_b = jnp.broadcast_to(grp, (tm, LANES))
        onehot = (lane == grp_b).astype(BF16)
        ranks = []
        for rb_ in range(tm // sub):
            r = lax.broadcasted_iota(jnp.int32, (sub, tm), 0) + rb_ * sub
            c = lax.broadcasted_iota(jnp.int32, (sub, tm), 1)
            ranks.append(jnp.dot((c < r).astype(BF16), onehot, preferred_element_type=F32))
        rank = jnp.concatenate(ranks, axis=0)
        pos = jnp.sum(jnp.where(lane == grp_b, rank, 0.0), axis=-1, keepdims=True)
        pos_b = jnp.broadcast_to(pos, (tm, LANES))
        grp_scr[...] = grp_b
        pos_scr[...] = pos_b
        grpt_scr[...] = grp_b.T
        post_scr[...] = pos_b.T
        acc_scr[...] = jnp.zeros_like(acc_scr)

    in_group = grp_scr[...] == gf
    n_g = jnp.sum(jnp.where(in_group[:, 0:1], 1.0, 0.0)).astype(jnp.int32)
    n_sub = (n_g + (sub - 1)) // sub
    d_iota = lax.broadcasted_iota(jnp.int32, (sub, tm), 0).astype(F32)
    lane16 = lax.broadcasted_iota(jnp.int32, (sub, N_EXPERTS), 1)

    def body(k, carry):
        kf = (k * sub).astype(F32)
        p = ((grpt_scr[...] == gf) & (post_scr[...] - kf == d_iota)).astype(BF16)
        hd = jnp.dot(p, h_scr[...], preferred_element_type=F32).astype(BF16)
        gd = (jnp.dot(p, ghi_scr[...], preferred_element_type=F32)
              + jnp.dot(p, glo_scr[...], preferred_element_type=F32))
        experts = range(EXPERTS_PER_GROUP)
        a = [jnp.dot(hd, wg_ref[j], preferred_element_type=F32) for j in experts]
        b = [jnp.dot(hd, wu_ref[j], preferred_element_type=F32) for j in experts]
        hid = [(a[j] * _sigmoid(a[j]) * b[j]).astype(BF16) for j in experts]
        dn = [jnp.dot(hid[j], wd_ref[j], preferred_element_type=F32) for j in experts]
        y = jnp.zeros((sub, D_MODEL), F32)
        for j in experts:
            ge = jnp.sum(jnp.where(lane16 == g * EXPERTS_PER_GROUP + j, gd, 0.0),
                         axis=-1, keepdims=True)
            y = y + ge * dn[j]
        pt = (in_group & (pos_scr[...] - kf == lane)).astype(BF16)
        acc_scr[...] += jnp.dot(pt, y.astype(BF16), preferred_element_type=F32)
        return carry

    lax.fori_loop(0, n_sub, body, 0)

    @pl.when(g == N_GROUPS - 1)
    def _():
        o_ref[0] = x_ref[0] + m_ref[0][5:6] * acc_scr[...]


def _moe(x, mods, g, rw, rb, wg, wu, wd, tm):
    bm, r, _ = x.shape
    row = pl.BlockSpec((1, tm, D_MODEL), lambda b, i, e: (b, i, 0))
    return pl.pallas_call(
        _moe_body,
        grid=(bm, r // tm, N_GROUPS),
        in_specs=[
            row,
            pl.BlockSpec((1, 6, D_MODEL), lambda b, i, e: (b, 0, 0)),
            pl.BlockSpec((1, D_MODEL), lambda b, i, e: (0, 0)),
            pl.BlockSpec((D_MODEL, 2 * LANES), lambda b, i, e: (0, 0)),
            pl.BlockSpec((1, N_EXPERTS), lambda b, i, e: (0, 0)),
            pl.BlockSpec((EXPERTS_PER_GROUP, D_MODEL, D_EXPERT), lambda b, i, e: (e, 0, 0)),
            pl.BlockSpec((EXPERTS_PER_GROUP, D_MODEL, D_EXPERT), lambda b, i, e: (e, 0, 0)),
            pl.BlockSpec((EXPERTS_PER_GROUP, D_EXPERT, D_MODEL), lambda b, i, e: (e, 0, 0)),
        ],
        out_specs=row,
        out_shape=jax.ShapeDtypeStruct(x.shape, F32),
        scratch_shapes=[
            pltpu.VMEM((tm, D_MODEL), BF16),
            pltpu.VMEM((tm, N_EXPERTS), BF16),
            pltpu.VMEM((tm, N_EXPERTS), BF16),
            pltpu.VMEM((tm, LANES), F32),
            pltpu.VMEM((tm, LANES), F32),
            pltpu.VMEM((LANES, tm), F32),
            pltpu.VMEM((LANES, tm), F32),
            pltpu.VMEM((tm, D_MODEL), F32),
        ],
        compiler_params=_params("parallel", "parallel", "arbitrary"),
        name="moe",
    )(x, mods, g, rw, rb, wg, wu, wd)


def _rope_tables(n_tokens):
    rows = n_tokens // GRID_W
    row = jnp.broadcast_to(jnp.arange(rows)[:, None], (rows, GRID_W)).reshape(-1).astype(F32)
    col = jnp.broadcast_to(jnp.arange(GRID_W)[None, :], (rows, GRID_W)).reshape(-1).astype(F32)
    inv_freq = ROPE_THETA ** (-jnp.arange(ROPE_PAIRS_PER_AXIS, dtype=F32) / ROPE_PAIRS_PER_AXIS)
    ang = jnp.concatenate([row[:, None] * inv_freq, col[:, None] * inv_freq], axis=-1)
    cos, sin = jnp.cos(ang), jnp.sin(ang)
    cos2, sin2 = jnp.concatenate([cos, cos], axis=-1), jnp.concatenate([-sin, sin], axis=-1)
    return cos2, sin2, cos2.T, sin2.T


def _split_w_in(w):
    sizes = (ATT_KV_W, ATT_KV_W, GLA_K_W, GLA_V_W, 2 * GLA_RANK,
             ATT_Q_W, GLA_K_W, GLA_V_W, 3 * HY_WIDTH, 3 * D_MODEL)
    cuts = [int(v) for v in np.cumsum(sizes)[:-1]]
    a_k, a_v, g_k, g_v, g_a, a_q, g_q, g_og, hy_u, br_g = jnp.split(w, cuts, axis=1)
    perm = np.concatenate([np.arange(0, HEAD_DIM, 2), np.arange(1, HEAD_DIM, 2)])
    perm_q = np.concatenate([h * HEAD_DIM + perm for h in range(ATT_HEADS)])
    perm_k = np.concatenate([h * HEAD_DIM + perm for h in range(ATT_KV_HEADS)])
    w_qv_t = jnp.concatenate([a_q[:, perm_q], a_v], axis=1).T.astype(BF16)
    w_k = a_k[:, perm_k].astype(BF16)
    pad = jnp.zeros((D_MODEL, GLA_A_PAD - 2 * GLA_RANK), w.dtype)
    w_gla = jnp.concatenate([g_q, g_k, g_v, g_og, g_a, pad], axis=1).astype(BF16)
    return w_qv_t, w_k, w_gla, hy_u.T.astype(BF16), br_g.astype(BF16), perm


def kernel(x, c, ctx, c_ctx, w_mod, b_mod, norm1_g, norm2_g, w_in, q_norm_g, k_norm_g, gla_wa2, gla_ba, gla_norm_g, hy_conv_w, hy_conv_b, hy_pos_w1, hy_pos_b1, hy_sin_freq, hy_pos_w2, hy_pos_b2, hy_pos_w3, hy_decay, hy_skip, w_br_hy, w_br_gla, w_br_att, w_out, router_w, router_b, moe_w_gate, moe_w_up, moe_w_down):
    nb, seq, d = x.shape
    nctx = ctx.shape[1]
    n1_lat = 2 * seq // LANES

    c_all = jnp.concatenate([c, c_ctx[None, :], jnp.zeros((16 - nb - 1, d), F32)], axis=0)
    mods_all = _mods(c_all, w_mod, b_mod)

    rope_tabs = _rope_tables(seq)
    consts = _dft_consts(n1_lat)
    fc, fi = _hyc_consts(nctx)
    tri_lat = _gla_tri(256)
    zero_state = jnp.zeros((nb, GLA_DV, GLA_K_W), F32)
    rb = router_b.reshape(1, N_EXPERTS)
    rw_hi = router_w.astype(BF16)
    rw_lo = (router_w - rw_hi.astype(F32)).astype(BF16)
    rw2 = jnp.concatenate([rw_hi, rw_lo, jnp.zeros((d, 2 * LANES - 2 * N_EXPERTS), BF16)], axis=1)

    xc = ctx.reshape(1, nb * nctx, d)
    for l in range(DEPTH):
        last = l == DEPTH - 1
        mods_lat = mods_all[l, :nb].reshape(nb, 6, d)
        mods_ctx = mods_all[l, nb:nb + 1].reshape(1, 6, d)
        g1 = norm1_g[l].reshape(1, d)
        g2 = norm2_g[l].reshape(1, d)
        w_qv_t, w_k, w_gla, w_hyt, w_brg, perm = _split_w_in(w_in[l])
        qg = (q_norm_g[l][perm] * Q_SCALE).reshape(HEAD_DIM, 1)
        kg = k_norm_g[l][perm].reshape(1, HEAD_DIM)
        wa2 = gla_wa2[l]
        ba = gla_ba[l].reshape(2, 1, GLA_K_W)
        gn = gla_norm_g[l].reshape(1, GLA_DV)
        wh, wg, wa, wo = (w_br_hy[l].astype(BF16), w_br_gla[l].astype(BF16),
                          w_br_att[l].astype(BF16), w_out[l].astype(BF16))
        cw, cb = hy_conv_w[l], hy_conv_b[l]
        sec = lambda s: [cw[0, s], cw[1, s], cw[2, s], cb[s]]
        w_ = HY_WIDTH
        prm = jnp.stack(sec(slice(0, w_)) + sec(slice(w_, 2 * w_)) + sec(slice(2 * w_, 3 * w_))
                        + [hy_skip[l, 0], hy_skip[l, 1], jnp.zeros((w_,), F32), jnp.zeros((w_,), F32)],
                        axis=1)
        filt_args = (hy_pos_w1[l], hy_pos_b1[l], hy_sin_freq[l], hy_pos_w2[l], hy_pos_b2[l],
                     hy_pos_w3[l], hy_decay[l])

        qt, k, vt = _proj_att(x, mods_lat, g1, w_qv_t, w_k, qg, kg, rope_tabs, 512)
        qct, kc, vct = _proj_att(xc, mods_ctx, g1, w_qv_t, w_k, qg, kg, None, 256)
        pg = _proj_plain(x, mods_lat, g1, w_gla, 512)
        pgc = _proj_plain(xc, mods_ctx, g1, w_gla, 256)
        ut = _proj_t(x, mods_lat, g1, w_hyt, 512)

        y_att = _attn(qt, k, vt, kc, vct, nb, 512, tk=2048)

        cof, cob, s_f, s_b = _gla(pgc.reshape(nb, nctx, GLA_COLS), wa2, ba, _gla_tri(nctx),
                                  zero_state, zero_state, nctx)
        of, ob, _, _ = _gla(pg, wa2, ba, tri_lat, s_f, s_b, 256)

        ff, fb = _hy_filters(seq, *filt_args)
        hr, hi = _hy_spec(ff.reshape(-1, seq), fb.reshape(-1, seq), consts, n1_lat)
        hr = hr.reshape(HY_ORDER, HY_WIDTH * n1_lat, LANES)
        hi = hi.reshape(HY_ORDER, HY_WIDTH * n1_lat, LANES)
        y_hyt = _hy_main(ut, prm, hr, hi, consts, n1_lat)

        x = _merge(x, mods_lat, g1, w_brg, y_hyt, of, ob, pg, gn, y_att, wh, wg, wa, wo, 512)
        moe_w = (moe_w_gate[l].astype(BF16), moe_w_up[l].astype(BF16), moe_w_down[l].astype(BF16))
        x = _moe(x, mods_lat, g2, rw2, rb, *moe_w, 1024)

        if not last:
            yc_att = _attn(qct, None, None, kc, vct, nb, nctx)
            uct = _proj_t(xc, mods_ctx, g1, w_hyt, 256)
            cff, cfb = _hy_filters(nctx, *filt_args)
            chr_, chi = _hyc_spec(cff.reshape(-1, nctx), cfb.reshape(-1, nctx), fc)
            chr_ = chr_.reshape(HY_ORDER, HY_WIDTH, 2 * nctx)
            chi = chi.reshape(HY_ORDER, HY_WIDTH, 2 * nctx)
            yc_hyt = _hyc_main(uct, prm, chr_, chi, fc, fi, nb, nctx)
            xc = _merge(xc, mods_ctx, g1, w_brg, yc_hyt,
                        cof.reshape(1, nb * nctx, GLA_V_W), cob.reshape(1, nb * nctx, GLA_V_W),
                        pgc, gn, yc_att, wh, wg, wa, wo, 256)
            xc = _moe(xc, mods_ctx, g2, rw2, rb, *moe_w, min(1024, nb * nctx))
    return x
```

```python
import functools
import math

import numpy as np
import jax
import jax.numpy as jnp
from jax import lax
from jax.experimental import pallas as pl
from jax.experimental.pallas import tpu as pltpu

F32 = jnp.float32
BF16 = jnp.bfloat16
HIGHEST = lax.Precision.HIGHEST

D_MODEL = 1024
DEPTH = 2
GRID_W = 64
EPS = 1e-6

ATT_HEADS = 8
ATT_KV_HEADS = 2
ATT_GROUP = ATT_HEADS // ATT_KV_HEADS
HEAD_DIM = 128
ROPE_PAIRS_PER_AXIS = HEAD_DIM // 4
ROPE_THETA = 10000.0
Q_SCALE = HEAD_DIM ** -0.5 * math.log2(math.e)

GLA_HEADS = 4
GLA_DK = 64
GLA_DV = 128
GLA_RANK = 16
GLA_TAU = 16.0
GLA_CHUNK = 64
GLA_BATCH_ROWS = 4

HY_WIDTH = 512
HY_ORDER = 2
HY_BANDS = 16
HY_EMB = 1 + 2 * HY_BANDS
HY_EMB_PAD = 40
HY_FFN = 64

N_EXPERTS = 16
N_GROUPS = 4
EXPERTS_PER_GROUP = N_EXPERTS // N_GROUPS
D_EXPERT = 512

ATT_Q_W = ATT_HEADS * HEAD_DIM
ATT_KV_W = ATT_KV_HEADS * HEAD_DIM
GLA_K_W = GLA_HEADS * GLA_DK
GLA_V_W = GLA_HEADS * GLA_DV
GLA_A_PAD = 128
GLA_COLS = 2 * GLA_K_W + 2 * GLA_V_W + GLA_A_PAD
ATT_COLS = ATT_Q_W + 2 * ATT_KV_W

LANES = 128
SUBLANES = 8
MOE_SUB = LANES
VMEM_LIMIT_BYTES = 60 * 1024 * 1024

DFT_GROUP = SUBLANES
HY_INTERLEAVE = 4

NT_DIMS = (((1,), (1,)), ((), ()))
TN_DIMS = (((0,), (0,)), ((), ()))


def _params(*sem):
    return pltpu.CompilerParams(dimension_semantics=sem, vmem_limit_bytes=VMEM_LIMIT_BYTES)


def _sigmoid(x):
    return 1.0 / (1.0 + jnp.exp(-x))


def _norm_mod(x, g, shift, scale):
    ms = jnp.mean(x * x, axis=-1, keepdims=True)
    return (x * lax.rsqrt(ms + EPS) * g) * (1.0 + scale) + shift


def _mod_body(c_ref, w_ref, b_ref, o_ref):
    c = c_ref[...]
    s = c * _sigmoid(c)
    o_ref[0] = jnp.dot(s, w_ref[0], precision=HIGHEST, preferred_element_type=F32) + b_ref[0]


def _mods(c_all, w_mod, b_mod):
    tn = 512
    rows = c_all.shape[0]
    return pl.pallas_call(
        _mod_body,
        grid=(DEPTH, 6 * D_MODEL // tn),
        in_specs=[
            pl.BlockSpec((rows, D_MODEL), lambda l, j: (0, 0)),
            pl.BlockSpec((1, D_MODEL, tn), lambda l, j: (l, 0, j)),
            pl.BlockSpec((1, 1, tn), lambda l, j: (l, 0, j)),
        ],
        out_specs=pl.BlockSpec((1, rows, tn), lambda l, j: (l, 0, j)),
        out_shape=jax.ShapeDtypeStruct((DEPTH, rows, 6 * D_MODEL), F32),
        compiler_params=_params("parallel", "parallel"),
        name="mods",
    )(c_all, w_mod, b_mod.reshape(DEPTH, 1, 6 * D_MODEL))


def _proj_att_body(*refs, rope):
    if rope:
        (x_ref, m_ref, g_ref, wqv_ref, wk_ref, qg_ref, kg_ref, cos_ref, sin_ref, cost_ref, sint_ref,
         qt_ref, k_ref, vt_ref) = refs
    else:
        x_ref, m_ref, g_ref, wqv_ref, wk_ref, qg_ref, kg_ref, qt_ref, k_ref, vt_ref = refs
    m = m_ref[0]
    h = _norm_mod(x_ref[0], g_ref[...], m[0:1], m[1:2]).astype(BF16)
    pt = lax.dot_general(wqv_ref[...], h, NT_DIMS, preferred_element_type=F32)
    pk = jnp.dot(h, wk_ref[...], preferred_element_type=F32)
    half = HEAD_DIM // 2

    for i in range(ATT_HEADS):
        sl = slice(i * HEAD_DIM, (i + 1) * HEAD_DIM)
        t = pt[sl]
        t = t * lax.rsqrt(jnp.mean(t * t, axis=0, keepdims=True) + EPS) * qg_ref[...]
        if rope:
            t = t * cost_ref[...] + pltpu.roll(t, half, 0) * sint_ref[...]
        qt_ref[0, sl, :] = t.astype(BF16)
    for i in range(ATT_KV_HEADS):
        sl = slice(i * HEAD_DIM, (i + 1) * HEAD_DIM)
        t = pk[:, sl]
        t = t * lax.rsqrt(jnp.mean(t * t, axis=-1, keepdims=True) + EPS) * kg_ref[...]
        if rope:
            t = t * cos_ref[...] + pltpu.roll(t, half, 1) * sin_ref[...]
        k_ref[0, :, sl] = t.astype(BF16)
    vt_ref[0] = pt[ATT_Q_W:].astype(BF16)


def _proj_att(x, mods, g, wqv_t, wk, qg_col, kg, rope_tabs, tm):
    bm, r, _ = x.shape
    rope = rope_tabs is not None
    in_specs = [
        pl.BlockSpec((1, tm, D_MODEL), lambda b, i: (b, i, 0)),
        pl.BlockSpec((1, 6, D_MODEL), lambda b, i: (b, 0, 0)),
        pl.BlockSpec((1, D_MODEL), lambda b, i: (0, 0)),
        pl.BlockSpec((ATT_Q_W + ATT_KV_W, D_MODEL), lambda b, i: (0, 0)),
        pl.BlockSpec((D_MODEL, ATT_KV_W), lambda b, i: (0, 0)),
        pl.BlockSpec((HEAD_DIM, 1), lambda b, i: (0, 0)),
        pl.BlockSpec((1, HEAD_DIM), lambda b, i: (0, 0)),
    ]
    args = [x, mods, g, wqv_t, wk, qg_col, kg]
    if rope:
        in_specs += [pl.BlockSpec((tm, HEAD_DIM), lambda b, i: (i, 0))] * 2
        in_specs += [pl.BlockSpec((HEAD_DIM, tm), lambda b, i: (0, i))] * 2
        args += list(rope_tabs)
    return pl.pallas_call(
        functools.partial(_proj_att_body, rope=rope),
        grid=(bm, r // tm),
        in_specs=in_specs,
        out_specs=[
            pl.BlockSpec((1, ATT_Q_W, tm), lambda b, i: (b, 0, i)),
            pl.BlockSpec((1, tm, ATT_KV_W), lambda b, i: (b, i, 0)),
            pl.BlockSpec((1, ATT_KV_W, tm), lambda b, i: (b, 0, i)),
        ],
        out_shape=[
            jax.ShapeDtypeStruct((bm, ATT_Q_W, r), BF16),
            jax.ShapeDtypeStruct((bm, r, ATT_KV_W), BF16),
            jax.ShapeDtypeStruct((bm, ATT_KV_W, r), BF16),
        ],
        compiler_params=_params("parallel", "parallel"),
        name="proj_att",
    )(*args)


def _proj_plain_body(x_ref, m_ref, g_ref, w_ref, o_ref):
    m = m_ref[0]
    h = _norm_mod(x_ref[0], g_ref[...], m[0:1], m[1:2]).astype(BF16)
    o_ref[0] = jnp.dot(h, w_ref[...], preferred_element_type=F32)


def _proj_plain(x, mods, g, w, tm):
    bm, r, _ = x.shape
    n = w.shape[1]
    return pl.pallas_call(
        _proj_plain_body,
        grid=(bm, r // tm),
        in_specs=[
            pl.BlockSpec((1, tm, D_MODEL), lambda b, i: (b, i, 0)),
            pl.BlockSpec((1, 6, D_MODEL), lambda b, i: (b, 0, 0)),
            pl.BlockSpec((1, D_MODEL), lambda b, i: (0, 0)),
            pl.BlockSpec((D_MODEL, n), lambda b, i: (0, 0)),
        ],
        out_specs=pl.BlockSpec((1, tm, n), lambda b, i: (b, i, 0)),
        out_shape=jax.ShapeDtypeStruct((bm, r, n), F32),
        compiler_params=_params("parallel", "parallel"),
        name="proj_gla",
    )(x, mods, g, w)


def _proj_t_body(x_ref, m_ref, g_ref, wt_ref, o_ref):
    m = m_ref[0]
    h = _norm_mod(x_ref[0], g_ref[...], m[0:1], m[1:2]).astype(BF16)
    o_ref[0] = lax.dot_general(wt_ref[...], h, NT_DIMS, preferred_element_type=F32)


def _proj_t(x, mods, g, wt, tm):
    bm, r, _ = x.shape
    n = wt.shape[0]
    return pl.pallas_call(
        _proj_t_body,
        grid=(bm, r // tm),
        in_specs=[
            pl.BlockSpec((1, tm, D_MODEL), lambda b, i: (b, i, 0)),
            pl.BlockSpec((1, 6, D_MODEL), lambda b, i: (b, 0, 0)),
            pl.BlockSpec((1, D_MODEL), lambda b, i: (0, 0)),
            pl.BlockSpec((n, D_MODEL), lambda b, i: (0, 0)),
        ],
        out_specs=pl.BlockSpec((1, n, tm), lambda b, i: (b, 0, i)),
        out_shape=jax.ShapeDtypeStruct((bm, n, r), F32),
        compiler_params=_params("parallel", "parallel"),
        name="proj_hy",
    )(x, mods, g, wt)


def _proj_gla_hy_body(x_ref, m_ref, g_ref, w_ref, wt_ref, o_ref, ot_ref):
    m = m_ref[0]
    h = _norm_mod(x_ref[0], g_ref[...], m[0:1], m[1:2]).astype(BF16)
    o_ref[0] = jnp.dot(h, w_ref[...], preferred_element_type=F32)
    ot_ref[0] = lax.dot_general(wt_ref[...], h, NT_DIMS, preferred_element_type=F32)


def _proj_gla_hy(x, mods, g, w, wt, tm):
    bm, r, _ = x.shape
    n, nt = w.shape[1], wt.shape[0]
    return pl.pallas_call(
        _proj_gla_hy_body,
        grid=(bm, r // tm),
        in_specs=[
            pl.BlockSpec((1, tm, D_MODEL), lambda b, i: (b, i, 0)),
            pl.BlockSpec((1, 6, D_MODEL), lambda b, i: (b, 0, 0)),
            pl.BlockSpec((1, D_MODEL), lambda b, i: (0, 0)),
            pl.BlockSpec((D_MODEL, n), lambda b, i: (0, 0)),
            pl.BlockSpec((nt, D_MODEL), lambda b, i: (0, 0)),
        ],
        out_specs=[pl.BlockSpec((1, tm, n), lambda b, i: (b, i, 0)),
                   pl.BlockSpec((1, nt, tm), lambda b, i: (b, 0, i))],
        out_shape=[jax.ShapeDtypeStruct((bm, r, n), F32), jax.ShapeDtypeStruct((bm, nt, r), F32)],
        compiler_params=_params("parallel", "parallel"),
        name="proj_gla_hy",
    )(x, mods, g, w, wt)


def _attn_body(*refs, has_lat, tk):
    if has_lat:
        qt_ref, kl_ref, vlt_ref, kc_ref, vct_ref, o_ref = refs
    else:
        qt_ref, kc_ref, vct_ref, o_ref = refs
    qt = qt_ref[0]
    tq = qt.shape[1]
    q4t = jnp.concatenate([qt[g * HEAD_DIM:(g + 1) * HEAD_DIM, :] for g in range(ATT_GROUP)], axis=1)

    def update(k_c, vt_c, state):
        s = jnp.dot(k_c, q4t, preferred_element_type=F32)
        mc = jnp.max(s, axis=0, keepdims=True)
        if state is None:
            p = jnp.exp2(s - mc)
            return mc, jnp.sum(p, axis=0, keepdims=True), jnp.dot(vt_c, p.astype(BF16),
                                                                   preferred_element_type=F32)
        m, den, acc = state
        m_new = jnp.maximum(m, mc)
        alpha = jnp.exp2(m - m_new)
        p = jnp.exp2(s - m_new)
        den = alpha * den + jnp.sum(p, axis=0, keepdims=True)
        acc = alpha * acc + jnp.dot(vt_c, p.astype(BF16), preferred_element_type=F32)
        return m_new, den, acc

    state = update(kc_ref[0], vct_ref[0], None)
    if has_lat:
        for j in range(kl_ref.shape[1] // tk):
            state = update(kl_ref[0, j * tk:(j + 1) * tk, :], vlt_ref[0, :, j * tk:(j + 1) * tk], state)
    _, den, acc = state
    o = acc / den
    for g in range(ATT_GROUP):
        o_ref[0, g * HEAD_DIM:(g + 1) * HEAD_DIM, :] = o[:, g * tq:(g + 1) * tq]


def _attn(qt, k_lat, vt_lat, k_ctx, vt_ctx, nb, tq, tk=4096):
    has_lat = k_lat is not None
    gw = ATT_GROUP * HEAD_DIM
    sc = k_ctx.shape[1] // nb
    if has_lat:
        sq = qt.shape[2]
        q_map = lambda bb, h, i: (bb, h, i)
    else:
        sq = tq
        q_map = lambda bb, h, i: (0, h, bb)
    in_specs = [pl.BlockSpec((1, gw, tq), q_map)]
    args = [qt]
    if has_lat:
        sk = k_lat.shape[1]
        in_specs += [pl.BlockSpec((1, sk, HEAD_DIM), lambda bb, h, i: (bb, 0, h)),
                     pl.BlockSpec((1, HEAD_DIM, sk), lambda bb, h, i: (bb, h, 0))]
        args += [k_lat, vt_lat]
    in_specs += [pl.BlockSpec((1, sc, HEAD_DIM), lambda bb, h, i: (0, bb, h)),
                 pl.BlockSpec((1, HEAD_DIM, sc), lambda bb, h, i: (0, h, bb))]
    args += [k_ctx, vt_ctx]
    return pl.pallas_call(
        functools.partial(_attn_body, has_lat=has_lat, tk=tk),
        grid=(nb, ATT_KV_HEADS, sq // tq),
        in_specs=in_specs,
        out_specs=pl.BlockSpec((1, gw, tq), q_map),
        out_shape=jax.ShapeDtypeStruct(qt.shape, F32),
        compiler_params=_params("parallel", "parallel", "parallel"),
        name="attn",
    )(*args)


def _gla_streams(streams):
    tt = streams[0][0].shape[0]
    nc = tt // GLA_CHUNK
    ns = range(len(streams))
    heads = range(GLA_HEADS)
    qs, ks, vs, a_s, wa2, ba, tri, rev, get_st, set_st = zip(*streams)
    zs = [jnp.dot(a_s[i], wa2[i], precision=HIGHEST, preferred_element_type=F32) + ba[i] for i in ns]
    las = [(jnp.minimum(z, 0.0) - jnp.log(1.0 + jnp.exp(-jnp.abs(z)))) * (1.0 / GLA_TAU) for z in zs]
    cums = [jnp.dot(tri[i], las[i], precision=HIGHEST, preferred_element_type=F32) for i in ns]
    qd = [(qs[i] * (GLA_DK ** -0.5) * jnp.exp(cums[i])).astype(BF16) for i in ns]
    ki = [(ks[i] * jnp.exp(-cums[i])).astype(BF16) for i in ns]
    vb = [vs[i].astype(BF16) for i in ns]
    row = lax.broadcasted_iota(jnp.int32, (GLA_CHUNK, GLA_CHUNK), 0)
    col = lax.broadcasted_iota(jnp.int32, (GLA_CHUNK, GLA_CHUNK), 1)
    masks = [(col >= row) if rev[i] else (col <= row) for i in ns]
    hk = lambda h: slice(h * GLA_DK, (h + 1) * GLA_DK)
    hv = lambda h: slice(h * GLA_DV, (h + 1) * GLA_DV)
    outs = [[None] * nc for _ in ns]
    for step in range(nc):
        chunk = [nc - 1 - step if rev[i] else step for i in ns]
        rs = [slice(c * GLA_CHUNK, (c + 1) * GLA_CHUNK) for c in chunk]
        ends = [rs[i].start if rev[i] else rs[i].stop - 1 for i in ns]
        cl = [cums[i][ends[i]:ends[i] + 1, :] for i in ns]
        kte = [(ks[i][rs[i]] * jnp.exp(cl[i] - cums[i][rs[i]])).astype(BF16) for i in ns]
        st = [get_st[i]() for i in ns]
        stb = [s.astype(BF16) for s in st]
        att = [[lax.dot_general(qd[i][rs[i], hk(h)], ki[i][rs[i], hk(h)], NT_DIMS,
                                preferred_element_type=F32) for h in heads] for i in ns]
        upd = [[lax.dot_general(vb[i][rs[i], hv(h)], kte[i][:, hk(h)], TN_DIMS,
                                preferred_element_type=F32) for h in heads] for i in ns]
        inter = [[lax.dot_general(qd[i][rs[i], hk(h)], stb[i][:, hk(h)], NT_DIMS,
                                  preferred_element_type=F32) for h in heads] for i in ns]
        intra = [[jnp.dot(jnp.where(masks[i], att[i][h], 0.0).astype(BF16), vb[i][rs[i], hv(h)],
                          preferred_element_type=F32) for h in heads] for i in ns]
        for i in ns:
            set_st[i](st[i] * jnp.exp(cl[i]) + jnp.concatenate(upd[i], axis=1))
            outs[i][chunk[i]] = jnp.concatenate([intra[i][h] + inter[i][h] for h in heads], axis=1)
    return [jnp.concatenate(o, axis=0) for o in outs]


def _gla_body(qf, kf, vf, af, qb, kb, vb, ab, wa2, ba, tri, s0f, s0b,
              of, ob, sf_out, sb_out, sf_scr, sb_scr):
    @pl.when(pl.program_id(1) == 0)
    def _():
        sf_scr[...] = s0f[...]
        sb_scr[...] = s0b[...]

    def stream(i, q, k, v, a, d, scr):
        def set_state(val):
            scr[i] = val
        a_d = a[i][:, d * GLA_RANK:(d + 1) * GLA_RANK]
        return (q[i], k[i], v[i], a_d, wa2[d], ba[d], tri[d], d == 1, lambda: scr[i], set_state)

    rows = range(qf.shape[0])
    outs = _gla_streams([stream(i, qf, kf, vf, af, 0, sf_scr) for i in rows]
                        + [stream(i, qb, kb, vb, ab, 1, sb_scr) for i in rows])
    for i in rows:
        of[i] = outs[i]
        ob[i] = outs[len(rows) + i]
    sf_out[...] = sf_scr[...]
    sb_out[...] = sb_scr[...]


def _gla(p, wa2, ba, tri, s0f, s0b, tt):
    b, s, _ = p.shape
    n = s // tt
    a_blk = (2 * GLA_K_W + 2 * GLA_V_W) // GLA_A_PAD
    rb = GLA_BATCH_ROWS if b % GLA_BATCH_ROWS == 0 else 1

    def specs(rev):
        t = (lambda j: n - 1 - j) if rev else (lambda j: j)
        return [
            pl.BlockSpec((rb, tt, GLA_K_W), lambda bb, j: (bb, t(j), 0)),
            pl.BlockSpec((rb, tt, GLA_K_W), lambda bb, j: (bb, t(j), 1)),
            pl.BlockSpec((rb, tt, GLA_V_W), lambda bb, j: (bb, t(j), 1)),
            pl.BlockSpec((rb, tt, GLA_A_PAD), lambda bb, j: (bb, t(j), a_blk)),
        ]

    st_spec = pl.BlockSpec((rb, GLA_DV, GLA_K_W), lambda bb, j: (bb, 0, 0))
    st_shape = jax.ShapeDtypeStruct((b, GLA_DV, GLA_K_W), F32)
    return pl.pallas_call(
        _gla_body,
        grid=(b // rb, n),
        in_specs=specs(False) + specs(True) + [
            pl.BlockSpec((2, GLA_RANK, GLA_K_W), lambda bb, j: (0, 0, 0)),
            pl.BlockSpec((2, 1, GLA_K_W), lambda bb, j: (0, 0, 0)),
            pl.BlockSpec((2, tt, tt), lambda bb, j: (0, 0, 0)),
            st_spec, st_spec,
        ],
        out_specs=[
            pl.BlockSpec((rb, tt, GLA_V_W), lambda bb, j: (bb, j, 0)),
            pl.BlockSpec((rb, tt, GLA_V_W), lambda bb, j: (bb, n - 1 - j, 0)),
            st_spec, st_spec,
        ],
        out_shape=[
            jax.ShapeDtypeStruct((b, s, GLA_V_W), F32),
            jax.ShapeDtypeStruct((b, s, GLA_V_W), F32),
            st_shape, st_shape,
        ],
        scratch_shapes=[pltpu.VMEM((rb, GLA_DV, GLA_K_W), F32), pltpu.VMEM((rb, GLA_DV, GLA_K_W), F32)],
        compiler_params=_params("parallel", "arbitrary"),
        name="gla",
    )(p, p, p, p, p, p, p, p, wa2, ba, tri, s0f, s0b)


def _gla_tri(tt):
    t = np.arange(tt)
    same = (t[:, None] // GLA_CHUNK) == (t[None, :] // GLA_CHUNK)
    fwd = same & (t[None, :] <= t[:, None])
    bwd = same & (t[None, :] >= t[:, None])
    return jnp.asarray(np.stack([fwd, bwd]).astype(np.float32))


def _hy_filter_body(zt_ref, w1t_ref, b1_ref, fr_ref, w2t_ref, b2_ref, w3t_ref, dec_ref,
                    f_ref, b_ref, h2_scr):
    @pl.when((pl.program_id(0) == 0) & (pl.program_id(1) == 0))
    def _():
        fr = fr_ref[...]
        h1 = jnp.sin(fr * (jnp.dot(w1t_ref[...], zt_ref[...], precision=HIGHEST,
                                   preferred_element_type=F32) + b1_ref[...]))
        h2_scr[...] = jnp.sin(fr * (jnp.dot(w2t_ref[...], h1, precision=HIGHEST,
                                            preferred_element_type=F32) + b2_ref[...]))

    h2 = h2_scr[...]
    tn = zt_ref[0:1, :]
    f = jnp.dot(w3t_ref[0, 0], h2, precision=HIGHEST, preferred_element_type=F32)
    f = f * jnp.exp(-tn * jnp.abs(dec_ref[0, 0]))
    b = jnp.dot(w3t_ref[0, 1], h2, precision=HIGHEST, preferred_element_type=F32)
    b = b * jnp.exp(-tn * jnp.abs(dec_ref[0, 1]))
    lane = lax.broadcasted_iota(jnp.int32, b.shape, 1)
    b = jnp.where(lane == 0, 0.0, b)
    den = (jnp.sum(jnp.abs(f), axis=-1, keepdims=True)
           + jnp.sum(jnp.abs(b), axis=-1, keepdims=True) + EPS)
    f_ref[0] = f / den
    b_ref[0] = b / den


def _hy_filters(n, w1, b1, fr, w2, b2, w3, dec):
    t = jnp.arange(n, dtype=F32)
    t_norm = t / n
    bands = jnp.linspace(1e-4, HY_BANDS - 1, HY_BANDS, dtype=F32)
    phase = (2 * math.pi / n) * t[:, None] * bands[None, :]
    z = jnp.concatenate([t_norm[:, None], jnp.cos(phase), -jnp.sin(phase)], axis=-1)
    zt = jnp.pad(z.T, ((0, HY_EMB_PAD - HY_EMB), (0, 0)))
    w1t = jnp.pad(w1.T, ((0, 0), (0, HY_EMB_PAD - HY_EMB)))
    w3t = w3.T.reshape(HY_ORDER, 2, HY_WIDTH, HY_FFN)
    ct = 128
    col = lambda v: v.reshape(HY_FFN, 1)
    full = lambda shape: pl.BlockSpec(shape, lambda o, c: (0,) * len(shape))
    out_spec = pl.BlockSpec((1, ct, n), lambda o, c: (o, c, 0))
    out_shape = jax.ShapeDtypeStruct((HY_ORDER, HY_WIDTH, n), F32)
    return pl.pallas_call(
        _hy_filter_body,
        grid=(HY_ORDER, HY_WIDTH // ct),
        in_specs=[
            full((HY_EMB_PAD, n)), full((HY_FFN, HY_EMB_PAD)), full((HY_FFN, 1)), full((HY_FFN, 1)),
            full((HY_FFN, HY_FFN)), full((HY_FFN, 1)),
            pl.BlockSpec((1, 2, ct, HY_FFN), lambda o, c: (o, 0, c, 0)),
            pl.BlockSpec((1, 2, ct, 1), lambda o, c: (o, 0, c, 0)),
        ],
        out_specs=[out_spec, out_spec],
        out_shape=[out_shape, out_shape],
        scratch_shapes=[pltpu.VMEM((HY_FFN, n), F32)],
        compiler_params=_params("arbitrary", "arbitrary"),
        name="hy_filter",
    )(zt, w1t, col(b1), col(fr), w2.T, col(b2), w3t, dec.reshape(HY_ORDER, 2, HY_WIDTH, 1))


def _dft_consts(n1):
    n = n1 * LANES
    half = n1 // 2
    g = DFT_GROUP
    rows = g * n1
    i1 = np.arange(half)
    g1k = np.zeros((2 * rows, g * half))
    for r in range(rows):
        c, k1 = divmod(r, n1)
        tau, j = divmod(r, SUBLANES)
        ang = 2 * np.pi * k1 * i1 / n1
        g1k[2 * SUBLANES * tau + j, i1 * g + c] = np.cos(ang)
        g1k[2 * SUBLANES * tau + SUBLANES + j, i1 * g + c] = -np.sin(ang)
    gik = g1k.T
    k1 = np.arange(n1)[:, None]
    i2 = np.arange(LANES)[None, :]
    at = 2 * np.pi * k1 * i2 / n
    twr = np.tile(np.cos(at), (g, 1))
    twi = np.tile(-np.sin(at), (g, 1))
    a2 = 2 * np.pi * np.arange(LANES)[:, None] * np.arange(LANES)[None, :] / LANES
    cplx = lambda fr, fi: np.block([[fr, fi], [-fi, fr]])
    f2 = cplx(np.cos(a2), -np.sin(a2))
    f2i = cplx(np.cos(a2), np.sin(a2))
    bf = lambda m: jnp.asarray(m, dtype=BF16)
    return dict(g1k=bf(g1k), twr=jnp.asarray(twr, F32), twi=jnp.asarray(twi, F32),
                f2=bf(f2), f2i=bf(f2i), gik=bf(gik))


def _split_tiles(a):
    nt = a.shape[0] // (2 * SUBLANES)
    re = jnp.concatenate([a[2 * SUBLANES * t:2 * SUBLANES * t + SUBLANES] for t in range(nt)], axis=0)
    im = jnp.concatenate([a[2 * SUBLANES * t + SUBLANES:2 * SUBLANES * (t + 1)] for t in range(nt)],
                         axis=0)
    return re, im


def _merge_tiles(re, im):
    nt = re.shape[0] // SUBLANES
    parts = []
    for t in range(nt):
        parts += [re[SUBLANES * t:SUBLANES * (t + 1)], im[SUBLANES * t:SUBLANES * (t + 1)]]
    return jnp.concatenate(parts, axis=0)


def _dot_lane_pairs(m, xs):
    out = []
    for i in range(0, len(xs) - 1, 2):
        r = jnp.dot(m, jnp.concatenate([xs[i], xs[i + 1]], axis=1), preferred_element_type=F32)
        out += [r[:, :LANES], r[:, LANES:]]
    if len(xs) % 2:
        out.append(jnp.dot(m, xs[-1], preferred_element_type=F32))
    return out


def _fwd_dft(xs, g1k, twr, twi, f2):
    half = xs[0].shape[1] // LANES
    rows = [jnp.concatenate([x[:, i * LANES:(i + 1) * LANES] for i in range(half)], axis=0)
            for x in xs]
    a = _dot_lane_pairs(g1k, [r.astype(BF16) for r in rows])
    t = []
    for v in a:
        ar, ai = _split_tiles(v)
        t.append(jnp.concatenate([ar * twr - ai * twi, ar * twi + ai * twr], axis=1).astype(BF16))
    p = [jnp.dot(v, f2, preferred_element_type=F32) for v in t]
    return [(v[:, :LANES], v[:, LANES:]) for v in p]


def _inv_dft(ys, twr, twi, f2i, gik):
    p = [jnp.dot(jnp.concatenate([yr, yi], axis=1).astype(BF16), f2i, preferred_element_type=F32)
         for yr, yi in ys]
    t = []
    for v in p:
        br, bi = v[:, :LANES], v[:, LANES:]
        t.append(_merge_tiles(br * twr + bi * twi, bi * twr - br * twi).astype(BF16))
    y = _dot_lane_pairs(gik, t)
    half = y[0].shape[0] // DFT_GROUP
    return [jnp.concatenate([v[i * DFT_GROUP:(i + 1) * DFT_GROUP] for i in range(half)], axis=1)
            for v in y]


def _hy_spec_body(f_ref, b_ref, g1k, twr, twi, f2, hr_ref, hi_ref, *, groups, n1):
    scale = 1.0 / (n1 * LANES)

    def body(t, carry):
        gs = [2 * t, 2 * t + 1]
        rows = [pl.ds(pl.multiple_of(g * DFT_GROUP, DFT_GROUP), DFT_GROUP) for g in gs]
        slabs = []
        for r in rows:
            slabs += [f_ref[r, :], b_ref[r, :]]
        z = _fwd_dft(slabs, g1k[...], twr[...], twi[...], f2[...])
        for i, g in enumerate(gs):
            (zfr, zfi), (zbr, zbi) = z[2 * i], z[2 * i + 1]
            orow = pl.ds(pl.multiple_of(g * (DFT_GROUP * n1), DFT_GROUP * n1), DFT_GROUP * n1)
            hr_ref[orow, :] = (zfr + zbr) * scale
            hi_ref[orow, :] = (zfi - zbi) * scale
        return carry

    lax.fori_loop(0, groups // 2, body, 0)


def _const_spec(arr, ngrid):
    nd = arr.ndim
    return pl.BlockSpec(arr.shape, lambda *_: (0,) * nd)


def _hy_spec(f, b, consts, n1):
    chans, n_half = f.shape
    ct = 64
    groups = ct // DFT_GROUP
    cs = [consts[k] for k in ("g1k", "twr", "twi", "f2")]
    in_spec = pl.BlockSpec((ct, n_half), lambda c: (c, 0))
    out_spec = pl.BlockSpec((ct * n1, LANES), lambda c: (c, 0))
    out_shape = jax.ShapeDtypeStruct((chans * n1, LANES), F32)
    return pl.pallas_call(
        functools.partial(_hy_spec_body, groups=groups, n1=n1),
        grid=(chans // ct,),
        in_specs=[in_spec, in_spec] + [_const_spec(c, 1) for c in cs],
        out_specs=[out_spec, out_spec],
        out_shape=[out_shape, out_shape],
        compiler_params=_params("parallel"),
        name="hy_spec",
    )(f, b, *cs)


def _hy_main_body(u1_ref, u2_ref, uz_ref, p_ref, hr_ref, hi_ref, g1k, twr, twi, f2, f2i, gik,
                  o_ref, *, groups, n1):
    n_half = u1_ref.shape[2]
    shape = (DFT_GROUP, n_half)
    lane = lax.broadcasted_iota(jnp.int32, shape, 1)
    t_first = lane == 0
    t_last = lane == n_half - 1

    def prev(x):
        return jnp.where(t_first, 0.0, pltpu.roll(x, 1, 1))

    def nxt(x):
        return jnp.where(t_last, 0.0, pltpu.roll(x, n_half - 1, 1))

    def body(t, carry):
        gs = [t * HY_INTERLEAVE + i for i in range(HY_INTERLEAVE)]
        cs = [pl.ds(pl.multiple_of(g * DFT_GROUP, DFT_GROUP), DFT_GROUP) for g in gs]
        hrows = [pl.ds(pl.multiple_of(g * (DFT_GROUP * n1), DFT_GROUP * n1), DFT_GROUP * n1)
                 for g in gs]
        prs = [p_ref[c, :] for c in cs]
        col = lambda i, j: prs[i][:, j:j + 1]

        def sconv(ref, i, j):
            x = ref[0, cs[i], :]
            return prev(x) * col(i, j) + x * col(i, j + 1) + nxt(x) * col(i, j + 2) + col(i, j + 3)

        idx = range(HY_INTERLEAVE)

        def conv(vs, o):
            zs = _fwd_dft(vs, g1k[...], twr[...], twi[...], f2[...])
            ys = []
            for i, (zr, zi) in enumerate(zs):
                hr = hr_ref[o, hrows[i], :]
                hi = hi_ref[o, hrows[i], :]
                ys.append((zr * hr - zi * hi, zr * hi + zi * hr))
            return _inv_dft(ys, twr[...], twi[...], f2i[...], gik[...])

        zs = [sconv(uz_ref, i, 8) for i in idx]
        cv = conv(zs, 0)
        zs = [sconv(u1_ref, i, 0) * (cv[i] + col(i, 12) * zs[i]) for i in idx]
        cv = conv(zs, 1)
        for i in idx:
            o_ref[0, cs[i], :] = sconv(u2_ref, i, 4) * (cv[i] + col(i, 13) * zs[i])
        return carry

    lax.fori_loop(0, groups // HY_INTERLEAVE, body, 0)


def _hy_main(ut, prm, hr, hi, consts, n1):
    b, _, n_half = ut.shape
    ct = 64
    nct = HY_WIDTH // ct
    cs = [consts[k] for k in ("g1k", "twr", "twi", "f2", "f2i", "gik")]
    u_spec = lambda sec: pl.BlockSpec((1, ct, n_half), lambda c, bb: (bb, sec * nct + c, 0))
    h_spec = pl.BlockSpec((HY_ORDER, ct * n1, LANES), lambda c, bb: (0, c, 0))
    return pl.pallas_call(
        functools.partial(_hy_main_body, groups=ct // DFT_GROUP, n1=n1),
        grid=(nct, b),
        in_specs=[u_spec(0), u_spec(1), u_spec(2),
                  pl.BlockSpec((ct, 16), lambda c, bb: (c, 0)),
                  h_spec, h_spec] + [_const_spec(c, 2) for c in cs],
        out_specs=pl.BlockSpec((1, ct, n_half), lambda c, bb: (bb, c, 0)),
        out_shape=jax.ShapeDtypeStruct((b, HY_WIDTH, n_half), F32),
        compiler_params=_params("parallel", "parallel"),
        name="hy_main",
    )(ut, ut, ut, prm, hr, hi, *cs)


def _hyc_consts(n):
    nn = 2 * n
    a = 2 * np.pi * np.arange(n)[:, None] * np.arange(nn)[None, :] / nn
    fc = np.concatenate([np.cos(a), -np.sin(a)], axis=1)
    fi = np.concatenate([np.cos(a.T), -np.sin(a.T)], axis=0)
    return jnp.asarray(fc, BF16), jnp.asarray(fi, BF16)


def _hyc_spec_body(f_ref, b_ref, fc_ref, hr_ref, hi_ref):
    nn = hr_ref.shape[-1]
    zf = jnp.dot(f_ref[...].astype(BF16), fc_ref[...], preferred_element_type=F32)
    zb = jnp.dot(b_ref[...].astype(BF16), fc_ref[...], preferred_element_type=F32)
    hr_ref[...] = (zf[:, :nn] + zb[:, :nn]) * (1.0 / nn)
    hi_ref[...] = (zf[:, nn:] - zb[:, nn:]) * (1.0 / nn)


def _hyc_spec(f, b, fc):
    rows, n = f.shape
    out_shape = jax.ShapeDtypeStruct((rows, 2 * n), F32)
    return pl.pallas_call(
        _hyc_spec_body,
        out_shape=[out_shape, out_shape],
        compiler_params=pltpu.CompilerParams(vmem_limit_bytes=VMEM_LIMIT_BYTES),
        name="hyc_spec",
    )(f, b, fc)


def _hyc_main_body(u1_ref, u2_ref, uz_ref, p_ref, hr_ref, hi_ref, fc_ref, fi_ref, o_ref):
    shape = u1_ref.shape[1:]
    n = shape[1]
    nn = 2 * n
    lane = lax.broadcasted_iota(jnp.int32, shape, 1)
    pr = p_ref[...]
    col = lambda j: pr[:, j:j + 1]

    def sconv(ref, j):
        x = ref[0]
        p = jnp.where(lane == 0, 0.0, pltpu.roll(x, 1, 1))
        q = jnp.where(lane == n - 1, 0.0, pltpu.roll(x, n - 1, 1))
        return p * col(j) + x * col(j + 1) + q * col(j + 2) + col(j + 3)

    def conv(v, o):
        s = jnp.dot(v.astype(BF16), fc_ref[...], preferred_element_type=F32)
        zr, zi = s[:, :nn], s[:, nn:]
        hr, hi = hr_ref[o], hi_ref[o]
        y = jnp.concatenate([zr * hr - zi * hi, zr * hi + zi * hr], axis=1).astype(BF16)
        return jnp.dot(y, fi_ref[...], preferred_element_type=F32)

    x1 = sconv(u1_ref, 0)
    x2 = sconv(u2_ref, 4)
    z = sconv(uz_ref, 8)
    z = x1 * (conv(z, 0) + col(12) * z)
    o_ref[0] = x2 * (conv(z, 1) + col(13) * z)


def _hyc_main(ut, prm, hr, hi, fc, fi, nb, n):
    ct = 128
    nct = HY_WIDTH // ct
    u_spec = lambda sec: pl.BlockSpec((1, ct, n), lambda bb, c: (0, sec * nct + c, bb))
    h_spec = pl.BlockSpec((HY_ORDER, ct, 2 * n), lambda bb, c: (0, c, 0))
    return pl.pallas_call(
        _hyc_main_body,
        grid=(nb, nct),
        in_specs=[u_spec(0), u_spec(1), u_spec(2),
                  pl.BlockSpec((ct, 16), lambda bb, c: (c, 0)),
                  h_spec, h_spec, _const_spec(fc, 2), _const_spec(fi, 2)],
        out_specs=pl.BlockSpec((1, ct, n), lambda bb, c: (0, c, bb)),
        out_shape=jax.ShapeDtypeStruct((1, HY_WIDTH, nb * n), F32),
        compiler_params=_params("parallel", "parallel"),
        name="hyc_main",
    )(ut, ut, ut, prm, hr, hi, fc, fi)


def _merge_body(x_ref, m_ref, g_ref, wbrg, yh_ref, of_ref, ob_ref, og_ref, gn_ref, ya_ref,
                wh, wg, wa, wo, o_ref):
    x = x_ref[0]
    m = m_ref[0]
    h = _norm_mod(x, g_ref[...], m[0:1], m[1:2]).astype(BF16)
    gates = jnp.dot(h, wbrg[...], preferred_element_type=F32)
    yh = yh_ref[0].T
    o = of_ref[0] + ob_ref[0]
    parts = []
    for hh in range(GLA_HEADS):
        t = o[:, hh * GLA_DV:(hh + 1) * GLA_DV]
        parts.append(t * lax.rsqrt(jnp.mean(t * t, axis=-1, keepdims=True) + EPS) * gn_ref[...])
    og = og_ref[0]
    yg = jnp.concatenate(parts, axis=1) * (og * _sigmoid(og))
    d = D_MODEL
    mm = (_sigmoid(gates[:, :d]) * jnp.dot(yh.astype(BF16), wh[...], preferred_element_type=F32)
          + _sigmoid(gates[:, d:2 * d]) * jnp.dot(yg.astype(BF16), wg[...], preferred_element_type=F32)
          + _sigmoid(gates[:, 2 * d:]) * jnp.dot(ya_ref[0].T.astype(BF16), wa[...],
                                                  preferred_element_type=F32))
    out = jnp.dot(mm.astype(BF16), wo[...], preferred_element_type=F32)
    o_ref[0] = x + m[2:3] * out


def _merge(x, mods, g, wbrg, yht, of, ob, pg, gn, yat, wh, wg, wa, wo, tm):
    bm, r, _ = x.shape
    row = lambda w: pl.BlockSpec((1, tm, w), lambda b, i: (b, i, 0))
    full = lambda a: pl.BlockSpec(a.shape, lambda b, i: (0,) * a.ndim, pipeline_mode=pl.Buffered(1))
    og_blk = (2 * GLA_K_W + GLA_V_W) // GLA_V_W
    return pl.pallas_call(
        _merge_body,
        grid=(bm, r // tm),
        in_specs=[
            row(D_MODEL),
            pl.BlockSpec((1, 6, D_MODEL), lambda b, i: (b, 0, 0)),
            full(g), full(wbrg),
            pl.BlockSpec((1, HY_WIDTH, tm), lambda b, i: (b, 0, i)),
            row(GLA_V_W), row(GLA_V_W),
            pl.BlockSpec((1, tm, GLA_V_W), lambda b, i: (b, i, og_blk)),
            full(gn),
            pl.BlockSpec((1, ATT_Q_W, tm), lambda b, i: (b, 0, i)),
            full(wh), full(wg), full(wa), full(wo),
        ],
        out_specs=row(D_MODEL),
        out_shape=jax.ShapeDtypeStruct(x.shape, F32),
        compiler_params=_params("parallel", "parallel"),
        name="merge",
    )(x, mods, g, wbrg, yht, of, ob, pg, gn, yat, wh, wg, wa, wo)


def _route(logits, rb):
    scores = _sigmoid(logits)
    sel = scores + rb
    lane = lax.broadcasted_iota(jnp.int32, sel.shape, 1).astype(F32)
    neg = -jnp.inf

    def top2(v):
        m1 = jnp.max(v, axis=-1, keepdims=True)
        i1 = jnp.min(jnp.where(v == m1, lane, float(N_EXPERTS)), axis=-1, keepdims=True)
        v2 = jnp.where(lane == i1, neg, v)
        m2 = jnp.max(v2, axis=-1, keepdims=True)
        i2 = jnp.min(jnp.where(v2 == m2, lane, float(N_EXPERTS)), axis=-1, keepdims=True)
        return m1, m2, i1, i2

    group_of = jnp.floor(lane * (1.0 / EXPERTS_PER_GROUP))
    best = None
    best_g = None
    for g in range(N_GROUPS):
        m1, m2, _, _ = top2(jnp.where(group_of == float(g), sel, neg))
        gs = m1 + m2
        if best is None:
            best, best_g = gs, jnp.zeros_like(gs)
        else:
            better = gs > best
            best_g = jnp.where(better, float(g), best_g)
            best = jnp.where(better, gs, best)
    _, _, i1, i2 = top2(jnp.where(group_of == best_g, sel, neg))
    w = jnp.where((lane == i1) | (lane == i2), scores, 0.0)
    return w / jnp.sum(w, axis=-1, keepdims=True), best_g


def _moe_body(x_ref, m_ref, g_ref, rw_ref, rb_ref, wg_ref, wu_ref, wd_ref, o_ref,
              h_scr, ghi_scr, glo_scr, grp_scr, pos_scr, grpt_scr, post_scr, acc_scr):
    g = pl.program_id(2)
    gf = g.astype(F32)
    tm = x_ref.shape[1]
    sub = MOE_SUB
    lane = lax.broadcasted_iota(jnp.int32, (tm, LANES), 1).astype(F32)

    @pl.when(g == 0)
    def _():
        m = m_ref[0]
        h = _norm_mod(x_ref[0], g_ref[...], m[3:4], m[4:5])
        h_hi = h.astype(BF16)
        h_scr[...] = h_hi
        h_lo = (h - h_hi.astype(F32)).astype(BF16)
        p2 = jnp.dot(h_hi, rw_ref[...], preferred_element_type=F32)
        p1 = jnp.dot(h_lo, rw_ref[:, :LANES], preferred_element_type=F32)
        ne = N_EXPERTS
        logits = p2[:, :ne] + p2[:, ne:2 * ne] + p1[:, :ne]
        gates, grp = _route(logits, rb_ref[...])
        ghi = gates.astype(BF16)
        ghi_scr[...] = ghi
        glo_scr[...] = (gates - ghi.astype(F32)).astype(BF16)
        grp_b = jnp.broadcast_to(grp, (tm, LANES))
        onehot = (lane == grp_b).astype(BF16)
        ranks = []
        for rb_ in range(tm // sub):
            r = lax.broadcasted_iota(jnp.int32, (sub, tm), 0) + rb_ * sub
            c = lax.broadcasted_iota(jnp.int32, (sub, tm), 1)
            ranks.append(jnp.dot((c < r).astype(BF16), onehot, preferred_element_type=F32))
        rank = jnp.concatenate(ranks, axis=0)
        pos = jnp.sum(jnp.where(lane == grp_b, rank, 0.0), axis=-1, keepdims=True)
        pos_b = jnp.broadcast_to(pos, (tm, LANES))
        grp_scr[...] = grp_b
        pos_scr[...] = pos_b
        grpt_scr[...] = grp_b.T
        post_scr[...] = pos_b.T
        acc_scr[...] = jnp.zeros_like(acc_scr)

    in_group = grp_scr[...] == gf
    n_g = jnp.sum(jnp.where(in_group[:, 0:1], 1.0, 0.0)).astype(jnp.int32)
    n_sub = (n_g + (sub - 1)) // sub
    d_iota = lax.broadcasted_iota(jnp.int32, (sub, tm), 0).astype(F32)
    lane16 = lax.broadcasted_iota(jnp.int32, (sub, N_EXPERTS), 1)

    def body(k, carry):
        kf = (k * sub).astype(F32)
        p = ((grpt_scr[...] == gf) & (post_scr[...] - kf == d_iota)).astype(BF16)
        hd = jnp.dot(p, h_scr[...], preferred_element_type=F32).astype(BF16)
        gd = (jnp.dot(p, ghi_scr[...], preferred_element_type=F32)
              + jnp.dot(p, glo_scr[...], preferred_element_type=F32))
        experts = range(EXPERTS_PER_GROUP)
        a = [jnp.dot(hd, wg_ref[j], preferred_element_type=F32) for j in experts]
        b = [jnp.dot(hd, wu_ref[j], preferred_element_type=F32) for j in experts]
        hid = [(a[j] * _sigmoid(a[j]) * b[j]).astype(BF16) for j in experts]
        dn = [jnp.dot(hid[j], wd_ref[j], preferred_element_type=F32) for j in experts]
        y = jnp.zeros((sub, D_MODEL), F32)
        for j in experts:
            ge = jnp.sum(jnp.where(lane16 == g * EXPERTS_PER_GROUP + j, gd, 0.0),
                         axis=-1, keepdims=True)
            y = y + ge * dn[j]
        pt = (in_group & (pos_scr[...] - kf == lane)).astype(BF16)
        acc_scr[...] += jnp.dot(pt, y.astype(BF16), preferred_element_type=F32)
        return carry

    lax.fori_loop(0, n_sub, body, 0)

    @pl.when(g == N_GROUPS - 1)
    def _():
        o_ref[0] = x_ref[0] + m_ref[0][5:6] * acc_scr[...]


def _moe(x, mods, g, rw, rb, wg, wu, wd, tm):
    bm, r, _ = x.shape
    row = pl.BlockSpec((1, tm, D_MODEL), lambda b, i, e: (b, i, 0))
    return pl.pallas_call(
        _moe_body,
        grid=(bm, r // tm, N_GROUPS),
        in_specs=[
            row,
            pl.BlockSpec((1, 6, D_MODEL), lambda b, i, e: (b, 0, 0)),
            pl.BlockSpec((1, D_MODEL), lambda b, i, e: (0, 0)),
            pl.BlockSpec((D_MODEL, 2 * LANES), lambda b, i, e: (0, 0)),
            pl.BlockSpec((1, N_EXPERTS), lambda b, i, e: (0, 0)),
            pl.BlockSpec((EXPERTS_PER_GROUP, D_MODEL, D_EXPERT), lambda b, i, e: (e, 0, 0)),
            pl.BlockSpec((EXPERTS_PER_GROUP, D_MODEL, D_EXPERT), lambda b, i, e: (e, 0, 0)),
            pl.BlockSpec((EXPERTS_PER_GROUP, D_EXPERT, D_MODEL), lambda b, i, e: (e, 0, 0)),
        ],
        out_specs=row,
        out_shape=jax.ShapeDtypeStruct(x.shape, F32),
        scratch_shapes=[
            pltpu.VMEM((tm, D_MODEL), BF16),
            pltpu.VMEM((tm, N_EXPERTS), BF16),
            pltpu.VMEM((tm, N_EXPERTS), BF16),
            pltpu.VMEM((tm, LANES), F32),
            pltpu.VMEM((tm, LANES), F32),
            pltpu.VMEM((LANES, tm), F32),
            pltpu.VMEM((LANES, tm), F32),
            pltpu.VMEM((tm, D_MODEL), F32),
        ],
        compiler_params=_params("parallel", "parallel", "arbitrary"),
        name="moe",
    )(x, mods, g, rw, rb, wg, wu, wd)


def _rope_tables(n_tokens):
    rows = n_tokens // GRID_W
    row = jnp.broadcast_to(jnp.arange(rows)[:, None], (rows, GRID_W)).reshape(-1).astype(F32)
    col = jnp.broadcast_to(jnp.arange(GRID_W)[None, :], (rows, GRID_W)).reshape(-1).astype(F32)
    inv_freq = ROPE_THETA ** (-jnp.arange(ROPE_PAIRS_PER_AXIS, dtype=F32) / ROPE_PAIRS_PER_AXIS)
    ang = jnp.concatenate([row[:, None] * inv_freq, col[:, None] * inv_freq], axis=-1)
    cos, sin = jnp.cos(ang), jnp.sin(ang)
    cos2, sin2 = jnp.concatenate([cos, cos], axis=-1), jnp.concatenate([-sin, sin], axis=-1)
    return cos2, sin2, cos2.T, sin2.T


def _split_w_in(w):
    sizes = (ATT_KV_W, ATT_KV_W, GLA_K_W, GLA_V_W, 2 * GLA_RANK,
             ATT_Q_W, GLA_K_W, GLA_V_W, 3 * HY_WIDTH, 3 * D_MODEL)
    cuts = [int(v) for v in np.cumsum(sizes)[:-1]]
    a_k, a_v, g_k, g_v, g_a, a_q, g_q, g_og, hy_u, br_g = jnp.split(w, cuts, axis=1)
    perm = np.concatenate([np.arange(0, HEAD_DIM, 2), np.arange(1, HEAD_DIM, 2)])
    perm_q = np.concatenate([h * HEAD_DIM + perm for h in range(ATT_HEADS)])
    perm_k = np.concatenate([h * HEAD_DIM + perm for h in range(ATT_KV_HEADS)])
    w_qv_t = jnp.concatenate([a_q[:, perm_q], a_v], axis=1).T.astype(BF16)
    w_k = a_k[:, perm_k].astype(BF16)
    pad = jnp.zeros((D_MODEL, GLA_A_PAD - 2 * GLA_RANK), w.dtype)
    w_gla = jnp.concatenate([g_q, g_k, g_v, g_og, g_a, pad], axis=1).astype(BF16)
    return w_qv_t, w_k, w_gla, hy_u.T.astype(BF16), br_g.astype(BF16), perm


def kernel(x, c, ctx, c_ctx, w_mod, b_mod, norm1_g, norm2_g, w_in, q_norm_g, k_norm_g, gla_wa2, gla_ba, gla_norm_g, hy_conv_w, hy_conv_b, hy_pos_w1, hy_pos_b1, hy_sin_freq, hy_pos_w2, hy_pos_b2, hy_pos_w3, hy_decay, hy_skip, w_br_hy, w_br_gla, w_br_att, w_out, router_w, router_b, moe_w_gate, moe_w_up, moe_w_down):
    nb, seq, d = x.shape
    nctx = ctx.shape[1]
    n1_lat = 2 * seq // LANES

    c_all = jnp.concatenate([c, c_ctx[None, :], jnp.zeros((16 - nb - 1, d), F32)], axis=0)
    mods_all = _mods(c_all, w_mod, b_mod)

    rope_tabs = _rope_tables(seq)
    consts = _dft_consts(n1_lat)
    fc, fi = _hyc_consts(nctx)
    tri_lat = _gla_tri(256)
    zero_state = jnp.zeros((nb, GLA_DV, GLA_K_W), F32)
    rb = router_b.reshape(1, N_EXPERTS)
    rw_hi = router_w.astype(BF16)
    rw_lo = (router_w - rw_hi.astype(F32)).astype(BF16)
    rw2 = jnp.concatenate([rw_hi, rw_lo, jnp.zeros((d, 2 * LANES - 2 * N_EXPERTS), BF16)], axis=1)

    xc = ctx.reshape(1, nb * nctx, d)
    for l in range(DEPTH):
        last = l == DEPTH - 1
        mods_lat = mods_all[l, :nb].reshape(nb, 6, d)
        mods_ctx = mods_all[l, nb:nb + 1].reshape(1, 6, d)
        g1 = norm1_g[l].reshape(1, d)
        g2 = norm2_g[l].reshape(1, d)
        w_qv_t, w_k, w_gla, w_hyt, w_brg, perm = _split_w_in(w_in[l])
        qg = (q_norm_g[l][perm] * Q_SCALE).reshape(HEAD_DIM, 1)
        kg = k_norm_g[l][perm].reshape(1, HEAD_DIM)
        wa2 = gla_wa2[l]
        ba = gla_ba[l].reshape(2, 1, GLA_K_W)
        gn = gla_norm_g[l].reshape(1, GLA_DV)
        wh, wg, wa, wo = (w_br_hy[l].astype(BF16), w_br_gla[l].astype(BF16),
                          w_br_att[l].astype(BF16), w_out[l].astype(BF16))
        cw, cb = hy_conv_w[l], hy_conv_b[l]
        sec = lambda s: [cw[0, s], cw[1, s], cw[2, s], cb[s]]
        w_ = HY_WIDTH
        prm = jnp.stack(sec(slice(0, w_)) + sec(slice(w_, 2 * w_)) + sec(slice(2 * w_, 3 * w_))
                        + [hy_skip[l, 0], hy_skip[l, 1], jnp.zeros((w_,), F32), jnp.zeros((w_,), F32)],
                        axis=1)
        filt_args = (hy_pos_w1[l], hy_pos_b1[l], hy_sin_freq[l], hy_pos_w2[l], hy_pos_b2[l],
                     hy_pos_w3[l], hy_decay[l])

        qt, k, vt = _proj_att(x, mods_lat, g1, w_qv_t, w_k, qg, kg, rope_tabs, 512)
        qct, kc, vct = _proj_att(xc, mods_ctx, g1, w_qv_t, w_k, qg, kg, None, 256)
        pg, ut = _proj_gla_hy(x, mods_lat, g1, w_gla, w_hyt, 512)
        pgc = _proj_plain(xc, mods_ctx, g1, w_gla, 256)

        y_att = _attn(qt, k, vt, kc, vct, nb, 512, tk=2048)

        cof, cob, s_f, s_b = _gla(pgc.reshape(nb, nctx, GLA_COLS), wa2, ba, _gla_tri(nctx),
                                  zero_state, zero_state, nctx)
        of, ob, _, _ = _gla(pg, wa2, ba, tri_lat, s_f, s_b, 256)

        ff, fb = _hy_filters(seq, *filt_args)
        hr, hi = _hy_spec(ff.reshape(-1, seq), fb.reshape(-1, seq), consts, n1_lat)
        hr = hr.reshape(HY_ORDER, HY_WIDTH * n1_lat, LANES)
        hi = hi.reshape(HY_ORDER, HY_WIDTH * n1_lat, LANES)
        y_hyt = _hy_main(ut, prm, hr, hi, consts, n1_lat)

        x = _merge(x, mods_lat, g1, w_brg, y_hyt, of, ob, pg, gn, y_att, wh, wg, wa, wo, 512)
        moe_w = (moe_w_gate[l].astype(BF16), moe_w_up[l].astype(BF16), moe_w_down[l].astype(BF16))
        x = _moe(x, mods_lat, g2, rw2, rb, *moe_w, 1024)

        if not last:
            yc_att = _attn(qct, None, None, kc, vct, nb, nctx)
            uct = _proj_t(xc, mods_ctx, g1, w_hyt, 256)
            cff, cfb = _hy_filters(nctx, *filt_args)
            chr_, chi = _hyc_spec(cff.reshape(-1, nctx), cfb.reshape(-1, nctx), fc)
            chr_ = chr_.reshape(HY_ORDER, HY_WIDTH, 2 * nctx)
            chi = chi.reshape(HY_ORDER, HY_WIDTH, 2 * nctx)
            yc_hyt = _hyc_main(uct, prm, chr_, chi, fc, fi, nb, nctx)
            xc = _merge(xc, mods_ctx, g1, w_brg, yc_hyt,
                        cof.reshape(1, nb * nctx, GLA_V_W), cob.reshape(1, nb * nctx, GLA_V_W),
                        pgc, gn, yc_att, wh, wg, wa, wo, 256)
            xc = _moe(xc, mods_ctx, g2, rw2, rb, *moe_w, min(1024, nb * nctx))
    return x
```
